```python
import math, functools
import jax, jax.numpy as jnp
from jax import lax
import numpy as np

D_MODEL = 2048
BATCH = 4
SEQ = 4096
DEPTH = 2

GRID_W = 64
CTX_LEN = 256
HEAD_DIM = 128
MIX_WIDTH = D_MODEL
MIX_HEADS = MIX_WIDTH // HEAD_DIM
NA_HEADS = MIX_HEADS // 4
GQA_HEADS = MIX_HEADS // 2
GQA_KV_HEADS = GQA_HEADS // 4
GQA_GROUP = GQA_HEADS // GQA_KV_HEADS
DIFF_HEADS = MIX_HEADS // 4
DIFF_QK_DIM = HEAD_DIM // 2
NA_ROWS_MAX = 8
NA_COLS = 16
Q_BLOCK = 128
ROPE_THETA = 10000.0
D_FF = int(2.75 * D_MODEL)
N_EXPERTS = 8
TOP_K = 2
N_DENSE = (DEPTH + 1) // 2
N_MOE = DEPTH // 2
EPS = 1e-6
SPLIT_SIZES = (NA_HEADS * HEAD_DIM, NA_HEADS * HEAD_DIM, NA_HEADS * HEAD_DIM,
               GQA_HEADS * HEAD_DIM, GQA_KV_HEADS * HEAD_DIM, GQA_KV_HEADS * HEAD_DIM,
               DIFF_HEADS * 2 * DIFF_QK_DIM, DIFF_HEADS * 2 * DIFF_QK_DIM, DIFF_HEADS * HEAD_DIM)
IN_WIDTH = sum(SPLIT_SIZES)
SPLIT_POINTS = [int(i) for i in np.cumsum(SPLIT_SIZES)[:-1]]

kernel_name = "hybrid_dit_na_gqa_diff_moe"


def rms_norm(x, g):
    xf = x.astype(jnp.float32)
    y = xf * lax.rsqrt(jnp.mean(xf * xf, axis=-1, keepdims=True) + EPS)
    return (y * g.astype(jnp.float32)).astype(x.dtype)


def adaln(cvec, w, b):
    m = jax.nn.silu(cvec) @ w + b
    return jnp.split(m[..., None, :], 6, axis=-1)


def modulate(h, shift, scale):
    return h * (1.0 + scale) + shift


def rope_1d(x, pos):
    half = x.shape[-1] // 2
    inv_freq = ROPE_THETA ** (-jnp.arange(half, dtype=jnp.float32) / half)
    ang = pos.astype(jnp.float32)[:, None] * inv_freq
    ang = ang.reshape((1, pos.shape[0]) + (1,) * (x.ndim - 3) + (half,))
    cos, sin = jnp.cos(ang), jnp.sin(ang)
    xf = x.astype(jnp.float32)
    x1, x2 = xf[..., :half], xf[..., half:]
    return jnp.concatenate([x1 * cos - x2 * sin, x2 * cos + x1 * sin], axis=-1).astype(x.dtype)


def axial_rope(x, row_pos, col_pos):
    d2 = x.shape[-1] // 2
    return jnp.concatenate([rope_1d(x[..., :d2], row_pos), rope_1d(x[..., d2:], col_pos)], axis=-1)


def sweep_query_blocks(fn, q):
    bsz, lq = q.shape[:2]
    nb = lq // Q_BLOCK
    qb = jnp.moveaxis(q.reshape((bsz, nb, Q_BLOCK) + q.shape[2:]), 1, 0)
    ob = lax.map(fn, qb)
    return jnp.moveaxis(ob, 0, 1).reshape((bsz, lq) + ob.shape[3:])


def gqa_attend(q, k, v):
    s = jnp.einsum('bqkgd,bskd->bkgqs', q, k) * (q.shape[-1] ** -0.5)
    p = jax.nn.softmax(s.astype(jnp.float32), axis=-1).astype(v.dtype)
    return jnp.einsum('bkgqs,bskd->bqkgd', p, v)


def diff_attend(q, k, v, lam):
    s = jnp.einsum('bqhid,bshid->bhiqs', q, k) * (q.shape[-1] ** -0.5)
    p = jax.nn.softmax(s.astype(jnp.float32), axis=-1)
    a = (p[:, :, 0] - lam * p[:, :, 1]).astype(v.dtype)
    return jnp.einsum('bhqs,bshd->bqhd', a, v)


def neighbourhood_attention(q, k, v, k_ctx, v_ctx, rpb):
    bsz, L, H, d = q.shape
    rows = L // GRID_W
    win_r = min(NA_ROWS_MAX, rows)
    qg = q.reshape(bsz, rows, GRID_W, H, d)
    kg = k.reshape(bsz, rows, GRID_W, H, d)
    vg = v.reshape(bsz, rows, GRID_W, H, d)
    r = jnp.arange(rows)
    row_start = jnp.clip(r - win_r // 2, 0, rows - win_r)
    key_rows = row_start[:, None] + jnp.arange(win_r)[None, :]
    k_band = kg[:, key_rows]
    v_band = vg[:, key_rows]
    col = jnp.arange(GRID_W)
    col_start = jnp.clip(col - NA_COLS // 2, 0, GRID_W - NA_COLS)
    col_ok = (col[None, :] >= col_start[:, None]) & (col[None, :] < col_start[:, None] + NA_COLS)
    dr = key_rows - r[:, None]
    dc = jnp.clip(col[None, :] - col[:, None], -(NA_COLS - 1), NA_COLS - 1)
    bias = rpb[:, dr[:, None, :, None] + NA_ROWS_MAX - 1,
               dc[None, :, None, :] + NA_COLS - 1]
    scale = d ** -0.5
    s = jnp.einsum('brqhd,brkwhd->bhrqkw', qg, k_band).astype(jnp.float32) * scale + bias.astype(jnp.float32)
    s = jnp.where(col_ok[:, None, :], s, -jnp.inf)
    s = s.reshape(bsz, H, rows, GRID_W, win_r * GRID_W)
    s_ctx = jnp.einsum('brqhd,bchd->bhrqc', qg, k_ctx).astype(jnp.float32) * scale
    p = jax.nn.softmax(jnp.concatenate([s, s_ctx], axis=-1), axis=-1).astype(v.dtype)
    p_band = p[..., :win_r * GRID_W].reshape(bsz, H, rows, GRID_W, win_r, GRID_W)
    p_ctx = p[..., win_r * GRID_W:]
    out = (jnp.einsum('bhrqkw,brkwhd->brqhd', p_band, v_band)
           + jnp.einsum('bhrqc,bchd->brqhd', p_ctx, v_ctx))
    return out.reshape(bsz, L, H, d)


def split_heads(p):
    bsz, n = p.shape[:2]
    aq, ak, av, bq, bk, bv, cq, ck, cv = jnp.split(p, SPLIT_POINTS, axis=-1)
    return (aq.reshape(bsz, n, NA_HEADS, HEAD_DIM),
            ak.reshape(bsz, n, NA_HEADS, HEAD_DIM),
            av.reshape(bsz, n, NA_HEADS, HEAD_DIM),
            bq.reshape(bsz, n, GQA_KV_HEADS, GQA_GROUP, HEAD_DIM),
            bk.reshape(bsz, n, GQA_KV_HEADS, HEAD_DIM),
            bv.reshape(bsz, n, GQA_KV_HEADS, HEAD_DIM),
            cq.reshape(bsz, n, DIFF_HEADS, 2, DIFF_QK_DIM),
            ck.reshape(bsz, n, DIFF_HEADS, 2, DIFF_QK_DIM),
            cv.reshape(bsz, n, DIFF_HEADS, HEAD_DIM))


def hybrid_mixer(h, h_ctx, row_pos, col_pos, w_in, w_out, rpb, q_norm, k_norm,
                 lq1, lk1, lq2, lk2, subln, lambda_init, with_ctx_out):
    bsz, L, _ = h.shape
    cn = h_ctx.shape[1]
    aq, ak, av, bq, bk, bv, cq, ck, cv = split_heads(h @ w_in)
    aqc, akc, avc, bqc, bkc, bvc, cqc, ckc, cvc = split_heads(h_ctx @ w_in)
    bq = axial_rope(rms_norm(bq, q_norm), row_pos, col_pos)
    bk = axial_rope(rms_norm(bk, k_norm), row_pos, col_pos)
    bkc = rms_norm(bkc, k_norm)
    cq = axial_rope(cq, row_pos, col_pos)
    ck = axial_rope(ck, row_pos, col_pos)
    lam = (jnp.exp(jnp.sum(lq1.astype(jnp.float32) * lk1.astype(jnp.float32)))
           - jnp.exp(jnp.sum(lq2.astype(jnp.float32) * lk2.astype(jnp.float32))) + lambda_init)
    oa = neighbourhood_attention(aq, ak, av, akc, avc, rpb)
    kb_all = jnp.concatenate([bk, bkc], axis=1)
    vb_all = jnp.concatenate([bv, bvc], axis=1)
    ob = sweep_query_blocks(lambda qb: gqa_attend(qb, kb_all, vb_all), bq)
    kc_all = jnp.concatenate([ck, ckc], axis=1)
    vc_all = jnp.concatenate([cv, cvc], axis=1)
    oc = sweep_query_blocks(lambda qb: diff_attend(qb, kc_all, vc_all, lam), cq)
    oc = rms_norm(oc, subln) * (1.0 - lambda_init)
    out = jnp.concatenate([oa.reshape(bsz, L, -1), ob.reshape(bsz, L, -1), oc.reshape(bsz, L, -1)], axis=-1) @ w_out
    out_ctx = None
    if with_ctx_out:
        oac = gqa_attend(aqc[:, :, :, None, :], akc, avc)
        obc = gqa_attend(rms_norm(bqc, q_norm), bkc, bvc)
        occ = rms_norm(diff_attend(cqc, ckc, cvc, lam), subln) * (1.0 - lambda_init)
        out_ctx = jnp.concatenate([oac.reshape(bsz, cn, -1), obc.reshape(bsz, cn, -1), occ.reshape(bsz, cn, -1)], axis=-1) @ w_out
    return out, out_ctx


def swiglu(h, w1, w3, w2):
    return (jax.nn.silu(h @ w1) * (h @ w3)) @ w2


def moe_swiglu(h, router, w1, w3, w2):
    logits = (h @ router).astype(jnp.float32)
    top_v, top_i = lax.top_k(logits, TOP_K)
    top_g = jax.nn.softmax(top_v, axis=-1)
    gates = jnp.sum(jax.nn.one_hot(top_i, N_EXPERTS, dtype=jnp.float32) * top_g[..., None], axis=-2)
    out = jnp.zeros(h.shape[:-1] + (w2.shape[-1],), h.dtype)
    for e in range(N_EXPERTS):
        out = out + gates[..., e:e + 1].astype(h.dtype) * swiglu(h, w1[e], w3[e], w2[e])
    return out


def setup_inputs(seed: int = 0) -> dict:
    key = jax.random.key(seed)
    ks = jax.random.split(key, 32)

    def nrm(k, shape, scale):
        return scale * jax.random.normal(k, shape, jnp.float32)

    return {
        "x": nrm(ks[0], (BATCH, SEQ, D_MODEL), 1.0),
        "c": nrm(ks[1], (BATCH, D_MODEL), 1.0),
        "ctx": nrm(ks[2], (BATCH, CTX_LEN, D_MODEL), 1.0),
        "c_ctx": nrm(ks[3], (D_MODEL,), 1.0),
        "w_mod": nrm(ks[4], (DEPTH, D_MODEL, 6 * D_MODEL), 0.5 * D_MODEL ** -0.5),
        "b_mod": nrm(ks[5], (DEPTH, 6 * D_MODEL), 0.02),
        "g_mix_pre": 1.0 + nrm(ks[6], (DEPTH, D_MODEL), 0.1),
        "g_mix_post": 1.0 + nrm(ks[7], (DEPTH, D_MODEL), 0.1),
        "g_ffn_pre": 1.0 + nrm(ks[8], (DEPTH, D_MODEL), 0.1),
        "g_ffn_post": 1.0 + nrm(ks[9], (DEPTH, D_MODEL), 0.1),
        "w_in": nrm(ks[10], (DEPTH, D_MODEL, IN_WIDTH), D_MODEL ** -0.5),
        "w_out": nrm(ks[11], (DEPTH, MIX_WIDTH, D_MODEL), MIX_WIDTH ** -0.5),
        "na_rpb": nrm(ks[12], (DEPTH, NA_HEADS, 2 * NA_ROWS_MAX - 1, 2 * NA_COLS - 1), 0.5),
        "gqa_q_norm": 1.0 + nrm(ks[13], (DEPTH, HEAD_DIM), 0.1),
        "gqa_k_norm": 1.0 + nrm(ks[14], (DEPTH, HEAD_DIM), 0.1),
        "diff_lambda_q1": nrm(ks[15], (DEPTH, DIFF_QK_DIM), 0.1),
        "diff_lambda_k1": nrm(ks[16], (DEPTH, DIFF_QK_DIM), 0.1),
        "diff_lambda_q2": nrm(ks[17], (DEPTH, DIFF_QK_DIM), 0.1),
        "diff_lambda_k2": nrm(ks[18], (DEPTH, DIFF_QK_DIM), 0.1),
        "diff_subln": 1.0 + nrm(ks[19], (DEPTH, HEAD_DIM), 0.1),
        "ffn_w1": nrm(ks[20], (N_DENSE, D_MODEL, D_FF), D_MODEL ** -0.5),
        "ffn_w3": nrm(ks[21], (N_DENSE, D_MODEL, D_FF), D_MODEL ** -0.5),
        "ffn_w2": nrm(ks[22], (N_DENSE, D_FF, D_MODEL), D_FF ** -0.5),
        "moe_router": nrm(ks[23], (N_MOE, D_MODEL, N_EXPERTS), D_MODEL ** -0.5),
        "moe_w1": nrm(ks[24], (N_MOE, N_EXPERTS, D_MODEL, D_FF), D_MODEL ** -0.5),
        "moe_w3": nrm(ks[25], (N_MOE, N_EXPERTS, D_MODEL, D_FF), D_MODEL ** -0.5),
        "moe_w2": nrm(ks[26], (N_MOE, N_EXPERTS, D_FF, D_MODEL), D_FF ** -0.5),
    }


def reference(x, c, ctx, c_ctx, w_mod, b_mod, g_mix_pre, g_mix_post, g_ffn_pre, g_ffn_post,
              w_in, w_out, na_rpb, gqa_q_norm, gqa_k_norm, diff_lambda_q1, diff_lambda_k1,
              diff_lambda_q2, diff_lambda_k2, diff_subln, ffn_w1, ffn_w3, ffn_w2,
              moe_router, moe_w1, moe_w3, moe_w2):
    L = x.shape[1]
    pos = jnp.arange(L)
    row_pos = pos // GRID_W
    col_pos = pos % GRID_W
    x_ctx = ctx
    for l in range(DEPTH):
        last = l == DEPTH - 1
        lambda_init = 0.8 - 0.6 * math.exp(-0.3 * l)
        sh1, sc1, gt1, sh2, sc2, gt2 = adaln(c, w_mod[l], b_mod[l])
        csh1, csc1, cgt1, csh2, csc2, cgt2 = adaln(c_ctx, w_mod[l], b_mod[l])
        h = modulate(rms_norm(x, g_mix_pre[l]), sh1, sc1)
        h_ctx = modulate(rms_norm(x_ctx, g_mix_pre[l]), csh1, csc1)
        o, o_ctx = hybrid_mixer(h, h_ctx, row_pos, col_pos, w_in[l], w_out[l], na_rpb[l],
                                gqa_q_norm[l], gqa_k_norm[l], diff_lambda_q1[l], diff_lambda_k1[l],
                                diff_lambda_q2[l], diff_lambda_k2[l], diff_subln[l], lambda_init,
                                not last)
        x = x + gt1 * rms_norm(o, g_mix_post[l])
        if l % 2 == 0:
            ffn = functools.partial(swiglu, w1=ffn_w1[l // 2], w3=ffn_w3[l // 2], w2=ffn_w2[l // 2])
        else:
            ffn = functools.partial(moe_swiglu, router=moe_router[l // 2], w1=moe_w1[l // 2],
                                    w3=moe_w3[l // 2], w2=moe_w2[l // 2])
        h2 = modulate(rms_norm(x, g_ffn_pre[l]), sh2, sc2)
        x = x + gt2 * rms_norm(ffn(h2), g_ffn_post[l])
        if not last:
            x_ctx = x_ctx + cgt1 * rms_norm(o_ctx, g_mix_post[l])
            h2c = modulate(rms_norm(x_ctx, g_ffn_pre[l]), csh2, csc2)
            x_ctx = x_ctx + cgt2 * rms_norm(ffn(h2c), g_ffn_post[l])
    return x
```

```python
import functools
import math

import jax
import jax.numpy as jnp
from jax import lax
from jax.experimental import pallas as pl
from jax.experimental.pallas import tpu as pltpu

GRID_W = 64
HEAD_DIM = 128
DIFF_QK_DIM = HEAD_DIM // 2
NA_ROWS = 8
NA_COLS = 16
ROPE_THETA = 10000.0
N_EXPERTS = 8
TOP_K = 2
EPS = 1e-6

LANES = 128
NEG_BIG = -1e30
VMEM_LIMIT = 56 * 1024 * 1024

BF16 = jnp.bfloat16
F32 = jnp.float32
NT_DIMS = (((1,), (1,)), ((), ()))


def _cparams(sem):
    return pltpu.CompilerParams(dimension_semantics=sem, vmem_limit_bytes=VMEM_LIMIT)


def _pick(n, cands):
    for c in cands:
        if n % c == 0:
            return c
    raise ValueError(f"no tile in {cands} divides {n}")


def _rms(y):
    return y * lax.rsqrt(jnp.mean(y * y, axis=-1, keepdims=True) + EPS)


def _mod_kernel(c_ref, w_ref, b_ref, o_ref):
    c = c_ref[...]
    s = c / (1.0 + jnp.exp(-c))
    o_ref[0] = jnp.dot(s, w_ref[0], preferred_element_type=F32,
                       precision=lax.Precision.HIGHEST) + b_ref[0]


def _modulation(cvec, w_mod, b_mod):
    depth, d, n = w_mod.shape
    r = cvec.shape[0]
    tn = _pick(n, (1024, 512, 256, 128))
    return pl.pallas_call(
        _mod_kernel,
        out_shape=jax.ShapeDtypeStruct((depth, r, n), F32),
        grid=(depth, n // tn),
        in_specs=[pl.BlockSpec((r, d), lambda l, j: (0, 0)),
                  pl.BlockSpec((1, d, tn), lambda l, j: (l, 0, j)),
                  pl.BlockSpec((1, 1, tn), lambda l, j: (l, 0, j))],
        out_specs=pl.BlockSpec((1, r, tn), lambda l, j: (l, 0, j)),
        compiler_params=_cparams(("parallel", "parallel")),
        name="adaln_mod",
    )(cvec, w_mod, b_mod.reshape(depth, 1, n))


def _na_bias_kernel(rpb_ref, o_ref):
    n_dc = 2 * NA_COLS - 1
    base = pl.program_id(0) * ((2 * NA_ROWS - 1) * n_dc)
    shape = (GRID_W, LANES)
    qc = lax.broadcasted_iota(jnp.int32, shape, 0)
    lane = lax.broadcasted_iota(jnp.int32, shape, 1)
    half = lane >> 6
    kc = lane & (GRID_W - 1)
    dc = jnp.clip(kc - qc, -(NA_COLS - 1), NA_COLS - 1) + (NA_COLS - 1)
    cs = jnp.clip(qc - NA_COLS // 2, 0, GRID_W - NA_COLS)
    ok = (kc >= cs) & (kc < cs + NA_COLS)
    code = jnp.where(ok, half * 32 + dc, -1)
    for d in range(2 * NA_ROWS - 2):
        acc = jnp.full(shape, NEG_BIG, F32)
        for hf in range(2):
            for j in range(n_dc):
                val = rpb_ref[base + (d + hf) * n_dc + j]
                acc = jnp.where(code == hf * 32 + j, val, acc)
        o_ref[0, d] = acc


def _na_bias_table(rpb):
    nh = rpb.shape[0]
    return pl.pallas_call(
        _na_bias_kernel,
        out_shape=jax.ShapeDtypeStruct((nh, 2 * NA_ROWS - 2, GRID_W, LANES), F32),
        grid=(nh,),
        in_specs=[pl.BlockSpec(memory_space=pltpu.SMEM)],
        out_specs=pl.BlockSpec((1, 2 * NA_ROWS - 2, GRID_W, LANES), lambda h: (h, 0, 0, 0)),
        compiler_params=_cparams(("parallel",)),
        name="na_bias_table",
    )(rpb.reshape(-1))


def _rope(y, cos, sin_hi, sin_lo, dist):
    return (y * cos + pltpu.roll(y, LANES - dist, 1) * sin_hi
            + pltpu.roll(y, dist, 1) * sin_lo)


def _in_proj_kernel(x_ref, g_ref, sh_ref, sc_ref, w_ref, qn_ref, kn_ref,
                    cg_ref, sga_ref, sgb_ref, cd_ref, sda_ref, sdb_ref,
                    naq_ref, nak_ref, nav_ref, gq_ref, gk_ref, gv_ref,
                    dq_ref, dk_ref, dv_ref, *, n_na, n_gq, n_gkv, n_diff):
    x = x_ref[0]
    h = _rms(x) * g_ref[...]
    h = h * (1.0 + sc_ref[0]) + sh_ref[0]
    hb = h.astype(BF16)

    def proj(c0, nheads):
        return jnp.dot(hb, w_ref[:, c0:c0 + nheads * HEAD_DIM], preferred_element_type=F32)

    def heads(y, nheads):
        return [y[:, i * HEAD_DIM:(i + 1) * HEAD_DIM] for i in range(nheads)]

    na_scale = HEAD_DIM ** -0.5
    diff_scale = DIFF_QK_DIM ** -0.5
    c0 = 0
    for i, y in enumerate(heads(proj(c0, n_na), n_na)):
        naq_ref[0, i] = (y * na_scale).astype(BF16)
    c0 += n_na * HEAD_DIM
    for i, y in enumerate(heads(proj(c0, n_na), n_na)):
        nak_ref[0, i] = y.astype(BF16)
    c0 += n_na * HEAD_DIM
    for i, y in enumerate(heads(proj(c0, n_na), n_na)):
        nav_ref[0, i] = y.astype(BF16)
    c0 += n_na * HEAD_DIM

    cg, sga, sgb = cg_ref[...], sga_ref[...], sgb_ref[...]
    for i, y in enumerate(heads(proj(c0, n_gq), n_gq)):
        y = _rope(_rms(y) * qn_ref[...], cg, sga, sgb, 32)
        gq_ref[0, i] = (y * na_scale).astype(BF16)
    c0 += n_gq * HEAD_DIM
    for i, y in enumerate(heads(proj(c0, n_gkv), n_gkv)):
        y = _rope(_rms(y) * kn_ref[...], cg, sga, sgb, 32)
        gk_ref[0, i] = y.astype(BF16)
    c0 += n_gkv * HEAD_DIM
    for i, y in enumerate(heads(proj(c0, n_gkv), n_gkv)):
        gv_ref[0, i] = y.astype(BF16)
    c0 += n_gkv * HEAD_DIM

    cd, sda, sdb = cd_ref[...], sda_ref[...], sdb_ref[...]
    for i, y in enumerate(heads(proj(c0, n_diff), n_diff)):
        dq_ref[0, i] = (_rope(y, cd, sda, sdb, 16) * diff_scale).astype(BF16)
    c0 += n_diff * HEAD_DIM
    for i, y in enumerate(heads(proj(c0, n_diff), n_diff)):
        dk_ref[0, i] = _rope(y, cd, sda, sdb, 16).astype(BF16)
    c0 += n_diff * HEAD_DIM
    for i, y in enumerate(heads(proj(c0, n_diff), n_diff)):
        dv_ref[0, i] = y.astype(BF16)


def _in_proj(x, g, sh, sc, w_in, qn, kn, tabs_g, tabs_d, dims):
    b, l, d = x.shape
    n_na, n_gq, n_gkv, n_diff = dims
    n_in = w_in.shape[1]
    tm = _pick(l, (256, 128))
    vec = pl.BlockSpec((1, d), lambda bi, i: (0, 0))
    mod = pl.BlockSpec((1, 1, d), lambda bi, i: (bi, 0, 0))
    hvec = pl.BlockSpec((1, HEAD_DIM), lambda bi, i: (0, 0))
    tab = pl.BlockSpec((tm, LANES), lambda bi, i: (i, 0))

    def out(nh):
        return (jax.ShapeDtypeStruct((b, nh, l, HEAD_DIM), BF16),
                pl.BlockSpec((1, nh, tm, HEAD_DIM), lambda bi, i: (bi, 0, i, 0)))

    outs = [out(n) for n in (n_na, n_na, n_na, n_gq, n_gkv, n_gkv, n_diff, n_diff, n_diff)]
    kern = functools.partial(_in_proj_kernel, n_na=n_na, n_gq=n_gq, n_gkv=n_gkv, n_diff=n_diff)
    return pl.pallas_call(
        kern,
        out_shape=[o[0] for o in outs],
        grid=(b, l // tm),
        in_specs=[pl.BlockSpec((1, tm, d), lambda bi, i: (bi, i, 0)), vec, mod, mod,
                  pl.BlockSpec((d, n_in), lambda bi, i: (0, 0), pipeline_mode=pl.Buffered(1)),
                  hvec, hvec, tab, tab, tab, tab, tab, tab],
        out_specs=[o[1] for o in outs],
        compiler_params=_cparams(("parallel", "parallel")),
        name="in_proj",
    )(x, g, sh, sc, w_in, qn, kn, *tabs_g, *tabs_d)


def _attn_kernel(*refs, n_seg, group, tq, chunks, diff, lam_init):
    q_ref = refs[0]
    kv_refs = refs[1:1 + 2 * n_seg]
    rest = refs[1 + 2 * n_seg:]
    if diff:
        lq1_ref, lk1_ref, lq2_ref, lk2_ref, sub_ref, o_ref = rest
        q = q_ref[0, 0].astype(F32)
        lane = lax.broadcasted_iota(jnp.int32, q.shape, 1)
        qq = jnp.concatenate([jnp.where(lane < DIFF_QK_DIM, q, 0.0),
                              jnp.where(lane >= DIFF_QK_DIM, q, 0.0)], axis=0).astype(BF16)
    else:
        (o_ref,) = rest
        qq = q_ref[0].reshape(group * tq, HEAD_DIM)
    m_rows = qq.shape[0]

    def step(k, v, carry):
        m, l, acc = carry
        s = lax.dot_general(qq, k, NT_DIMS, preferred_element_type=F32)
        m_new = jnp.maximum(m, jnp.max(s, axis=-1, keepdims=True))
        alpha = jnp.exp(m - m_new)
        p = jnp.exp(s - m_new)
        l = alpha * l + jnp.sum(p, axis=-1, keepdims=True)
        acc = alpha * acc + jnp.dot(p.astype(BF16), v, preferred_element_type=F32)
        return m_new, l, acc

    carry = (jnp.full((m_rows, 1), NEG_BIG, F32), jnp.zeros((m_rows, 1), F32),
             jnp.zeros((m_rows, HEAD_DIM), F32))
    for si in range(n_seg):
        k_ref, v_ref = kv_refs[2 * si], kv_refs[2 * si + 1]
        n_chunks, tk = chunks[si]
        if n_chunks == 1:
            carry = step(k_ref[0, 0], v_ref[0, 0], carry)
        else:
            def body(c, carry, k_ref=k_ref, v_ref=v_ref, tk=tk):
                rows = pl.ds(pl.multiple_of(c * tk, tk), tk)
                return step(k_ref[0, 0, rows, :], v_ref[0, 0, rows, :], carry)
            carry = lax.fori_loop(0, n_chunks, body, carry)
    _, l, acc = carry
    o = acc / l
    if diff:
        lam = (jnp.exp(jnp.sum(lq1_ref[...] * lk1_ref[...], axis=-1, keepdims=True))
               - jnp.exp(jnp.sum(lq2_ref[...] * lk2_ref[...], axis=-1, keepdims=True)) + lam_init)
        od = o[:tq] - lam * o[tq:]
        o_ref[0] = (_rms(od) * sub_ref[...] * (1.0 - lam_init)).astype(BF16)
    else:
        for g in range(group):
            o_ref[0, :, g * HEAD_DIM:(g + 1) * HEAD_DIM] = o[g * tq:(g + 1) * tq].astype(BF16)


def _attention(q, segs, *, group, tq, diff=None):
    b, hq, lq, _ = q.shape
    hkv = hq // group
    kv_args, kv_specs, chunks = [], [], []
    for k, v in segs:
        s = k.shape[2]
        tk = _pick(s, (512, 256, 128))
        chunks.append((s // tk, tk))
        spec = pl.BlockSpec((1, 1, s, HEAD_DIM), lambda bi, h, i: (bi, h, 0, 0))
        kv_args += [k, v]
        kv_specs += [spec, spec]
    extra_args, extra_specs, lam_init = [], [], 0.0
    if diff is not None:
        lq1, lk1, lq2, lk2, subln, lam_init = diff
        extra_args = [lq1, lk1, lq2, lk2, subln]
        extra_specs = [pl.BlockSpec((1, a.shape[1]), lambda bi, h, i: (0, 0)) for a in extra_args]
    kern = functools.partial(_attn_kernel, n_seg=len(segs), group=group, tq=tq,
                             chunks=tuple(chunks), diff=diff is not None, lam_init=lam_init)
    return pl.pallas_call(
        kern,
        out_shape=jax.ShapeDtypeStruct((b, lq, hq * HEAD_DIM), BF16),
        grid=(b, hkv, lq // tq),
        in_specs=[pl.BlockSpec((1, group, tq, HEAD_DIM), lambda bi, h, i: (bi, h, i, 0))]
                 + kv_specs + extra_specs,
        out_specs=pl.BlockSpec((1, tq, group * HEAD_DIM), lambda bi, h, i: (bi, i, h)),
        compiler_params=_cparams(("parallel", "parallel", "parallel")),
        name="diff_attn" if diff is not None else "gqa_attn",
    )(q, *kv_args, *extra_args)


def _na_kernel(q_ref, k_ref, v_ref, kc_ref, vc_ref, bias_ref, o_ref, *, rows):
    band = NA_ROWS * GRID_W
    kc = kc_ref[0, 0]
    vc = vc_ref[0, 0]

    def body(r, _):
        rs = jnp.clip(r - NA_ROWS // 2, 0, rows - NA_ROWS)
        off = rs - r + (NA_ROWS - 1)
        qrows = pl.ds(pl.multiple_of(r * GRID_W, GRID_W), GRID_W)
        krows = pl.ds(pl.multiple_of(rs * GRID_W, GRID_W), band)
        q = q_ref[0, 0, qrows, :]
        bias = jnp.concatenate([bias_ref[0, off + 2 * j] for j in range(NA_ROWS // 2)], axis=1)
        s = lax.dot_general(q, k_ref[0, 0, krows, :], NT_DIMS, preferred_element_type=F32) + bias
        sc = lax.dot_general(q, kc, NT_DIMS, preferred_element_type=F32)
        m = jnp.maximum(jnp.max(s, axis=-1, keepdims=True), jnp.max(sc, axis=-1, keepdims=True))
        p = jnp.exp(s - m)
        pc = jnp.exp(sc - m)
        l = jnp.sum(p, axis=-1, keepdims=True) + jnp.sum(pc, axis=-1, keepdims=True)
        o = (jnp.dot(p.astype(BF16), v_ref[0, 0, krows, :], preferred_element_type=F32)
             + jnp.dot(pc.astype(BF16), vc, preferred_element_type=F32))
        o_ref[0, qrows, :] = (o / l).astype(BF16)
        return 0

    lax.fori_loop(0, rows, body, 0)


def _na_attention(q, k, v, kc, vc, bias):
    b, nh, l, _ = q.shape
    c = kc.shape[2]
    rows = l // GRID_W
    full = pl.BlockSpec((1, 1, l, HEAD_DIM), lambda bi, h: (bi, h, 0, 0))
    ctx = pl.BlockSpec((1, 1, c, HEAD_DIM), lambda bi, h: (bi, h, 0, 0))
    return pl.pallas_call(
        functools.partial(_na_kernel, rows=rows),
        out_shape=jax.ShapeDtypeStruct((b, l, nh * HEAD_DIM), BF16),
        grid=(b, nh),
        in_specs=[full, full, full, ctx, ctx,
                  pl.BlockSpec((1,) + bias.shape[1:], lambda bi, h: (h, 0, 0, 0))],
        out_specs=pl.BlockSpec((1, l, HEAD_DIM), lambda bi, h: (bi, 0, h)),
        compiler_params=_cparams(("parallel", "parallel")),
        name="na_attn",
    )(q, k, v, kc, vc, bias)


def _out_proj_kernel(*refs, n_o, routed):
    o_refs = refs[:n_o]
    (w_ref, x_ref, gpost_ref, gt_ref, gpre_ref, sh_ref, sc_ref) = refs[n_o:n_o + 7]
    rest = refs[n_o + 7:]
    if routed:
        router_ref, x1_ref, h2_ref, route_ref = rest
    else:
        x1_ref, h2_ref = rest
    acc = None
    c0 = 0
    for o_ref in o_refs:
        width = o_ref.shape[-1]
        part = jnp.dot(o_ref[0], w_ref[c0:c0 + width, :], preferred_element_type=F32)
        acc = part if acc is None else acc + part
        c0 += width
    x1 = x_ref[0] + gt_ref[0] * (_rms(acc) * gpost_ref[...])
    x1_ref[0] = x1
    h2 = (_rms(x1) * gpre_ref[...]) * (1.0 + sc_ref[0]) + sh_ref[0]
    h2_ref[0] = h2.astype(h2_ref.dtype)
    if routed:
        logits = jnp.dot(h2, router_ref[...], preferred_element_type=F32,
                         precision=lax.Precision.HIGHEST)
        lane = lax.broadcasted_iota(jnp.int32, logits.shape, 1)
        logits = jnp.where(lane < N_EXPERTS, logits, NEG_BIG)
        v1 = jnp.max(logits, axis=-1, keepdims=True)
        i1 = jnp.min(jnp.where(logits == v1, lane, LANES), axis=-1, keepdims=True)
        rest_l = jnp.where(lane == i1, NEG_BIG, logits)
        v2 = jnp.max(rest_l, axis=-1, keepdims=True)
        i2 = jnp.min(jnp.where(rest_l == v2, lane, LANES), axis=-1, keepdims=True)
        e2 = jnp.exp(v2 - v1)
        g1 = 1.0 / (1.0 + e2)
        g2 = e2 / (1.0 + e2)
        route_ref[0] = jnp.where(lane == 0, i1.astype(F32),
                       jnp.where(lane == 1, i2.astype(F32),
                       jnp.where(lane == 2, g1, jnp.where(lane == 3, g2, 0.0))))


def _out_proj(o_list, w_out, x, gpost, gt, gpre, sh, sc, router=None):
    b, l, d = x.shape
    tm = _pick(l, (256, 128))
    routed = router is not None
    vec = pl.BlockSpec((1, d), lambda bi, i: (0, 0))
    mod = pl.BlockSpec((1, 1, d), lambda bi, i: (bi, 0, 0))
    row = pl.BlockSpec((1, tm, d), lambda bi, i: (bi, i, 0))
    in_specs = [pl.BlockSpec((1, tm, o.shape[-1]), lambda bi, i: (bi, i, 0)) for o in o_list]
    in_specs += [pl.BlockSpec(w_out.shape, lambda bi, i: (0, 0), pipeline_mode=pl.Buffered(1)),
                 row, vec, mod, vec, mod, mod]
    args = list(o_list) + [w_out, x, gpost, gt, gpre, sh, sc]
    out_shape = [jax.ShapeDtypeStruct((b, l, d), F32),
                 jax.ShapeDtypeStruct((b, l, d), F32 if routed else BF16)]
    out_specs = [row, row]
    if routed:
        in_specs.append(pl.BlockSpec(router.shape, lambda bi, i: (0, 0)))
        args.append(router)
        out_shape.append(jax.ShapeDtypeStruct((b, l, LANES), F32))
        out_specs.append(pl.BlockSpec((1, tm, LANES), lambda bi, i: (bi, i, 0)))
    return pl.pallas_call(
        functools.partial(_out_proj_kernel, n_o=len(o_list), routed=routed),
        out_shape=out_shape,
        grid=(b, l // tm),
        in_specs=in_specs,
        out_specs=out_specs,
        compiler_params=_cparams(("parallel", "parallel")),
        name="out_proj",
    )(*args)


def _swiglu_step(hb, w1, w3, w2):
    a = jnp.dot(hb, w1, preferred_element_type=F32)
    g = jnp.dot(hb, w3, preferred_element_type=F32)
    act = (a / (1.0 + jnp.exp(-a))) * g
    return jnp.dot(act.astype(BF16), w2, preferred_element_type=F32)


def _ffn_dense_kernel(h_ref, w1_ref, w3_ref, w2_ref, x_ref, gpost_ref, gt_ref, o_ref, acc_ref):
    f = pl.program_id(2)

    @pl.when(f == 0)
    def _():
        acc_ref[...] = jnp.zeros_like(acc_ref)

    acc_ref[...] += _swiglu_step(h_ref[0], w1_ref[...], w3_ref[...], w2_ref[...])

    @pl.when(f == pl.num_programs(2) - 1)
    def _():
        o_ref[0] = x_ref[0] + gt_ref[0] * (_rms(acc_ref[...]) * gpost_ref[...])


def _ffn_dense(h2, w1, w3, w2, x1, gpost, gt):
    b, l, d = x1.shape
    dff = w1.shape[1]
    tm = _pick(l, (512, 256, 128))
    tf = _pick(dff, (512, 256, 128))
    row = pl.BlockSpec((1, tm, d), lambda bi, i, f: (bi, i, 0))
    return pl.pallas_call(
        _ffn_dense_kernel,
        out_shape=jax.ShapeDtypeStruct((b, l, d), F32),
        grid=(b, l // tm, dff // tf),
        in_specs=[row,
                  pl.BlockSpec((d, tf), lambda bi, i, f: (0, f)),
                  pl.BlockSpec((d, tf), lambda bi, i, f: (0, f)),
                  pl.BlockSpec((tf, d), lambda bi, i, f: (f, 0)),
                  row,
                  pl.BlockSpec((1, d), lambda bi, i, f: (0, 0)),
                  pl.BlockSpec((1, 1, d), lambda bi, i, f: (bi, 0, 0))],
        out_specs=row,
        scratch_shapes=[pltpu.VMEM((tm, d), F32)],
        compiler_params=_cparams(("parallel", "parallel", "arbitrary")),
        name="ffn_dense",
    )(h2, w1, w3, w2, x1, gpost, gt)


def _ffn_expert_kernel(tile_e_ref, nact_ref, x_ref, w1_ref, w3_ref, w2_ref, o_ref, hb_ref, acc_ref):
    j = pl.program_id(0)
    f = pl.program_id(1)

    @pl.when(j < nact_ref[0])
    def _():
        @pl.when(f == 0)
        def _():
            hb_ref[...] = x_ref[...].astype(BF16)
            acc_ref[...] = jnp.zeros_like(acc_ref)

        acc_ref[...] += _swiglu_step(hb_ref[...], w1_ref[0], w3_ref[0], w2_ref[0])

        @pl.when(f == pl.num_programs(1) - 1)
        def _():
            o_ref[...] = acc_ref[...]

    @pl.when((j >= nact_ref[0]) & (f == pl.num_programs(1) - 1))
    def _():
        o_ref[...] = jnp.zeros_like(o_ref)


def _ffn_expert(xs, w1, w3, w2, tile_e, nact, tm):
    p, d = xs.shape
    dff = w1.shape[2]
    tf = _pick(dff, (512, 256, 128))
    nf = dff // tf

    def row_map(j, f, te, na):
        return (jnp.minimum(j, na[0] - 1), 0)

    def fsel(j, f, na):
        return jnp.where(j < na[0], f, nf - 1)

    return pl.pallas_call(
        _ffn_expert_kernel,
        out_shape=jax.ShapeDtypeStruct((p, d), F32),
        grid_spec=pltpu.PrefetchScalarGridSpec(
            num_scalar_prefetch=2,
            grid=(p // tm, nf),
            in_specs=[pl.BlockSpec((tm, d), row_map),
                      pl.BlockSpec((1, d, tf), lambda j, f, te, na: (te[j], 0, fsel(j, f, na))),
                      pl.BlockSpec((1, d, tf), lambda j, f, te, na: (te[j], 0, fsel(j, f, na))),
                      pl.BlockSpec((1, tf, d), lambda j, f, te, na: (te[j], fsel(j, f, na), 0))],
            out_specs=pl.BlockSpec((tm, d), lambda j, f, te, na: (j, 0)),
            scratch_shapes=[pltpu.VMEM((tm, d), BF16), pltpu.VMEM((tm, d), F32)]),
        compiler_params=_cparams(("arbitrary", "arbitrary")),
        name="ffn_expert",
    )(tile_e, nact, xs, w1, w3, w2)


def _rank_kernel(route_ref, rank_ref, cnt_ref, carry_ref):
    i = pl.program_id(0)

    @pl.when(i == 0)
    def _():
        carry_ref[...] = jnp.zeros_like(carry_ref)

    r = route_ref[...]
    t = r.shape[0]
    lane = lax.broadcasted_iota(jnp.int32, r.shape, 1)
    e1 = r[:, 0:1].astype(jnp.int32)
    e2 = r[:, 1:2].astype(jnp.int32)
    oh1 = (lane == e1).astype(F32)
    oh2 = (lane == e2).astype(F32)
    c = (oh1 + oh2).astype(BF16)
    row = lax.broadcasted_iota(jnp.int32, (t, t), 0)
    col = lax.broadcasted_iota(jnp.int32, (t, t), 1)
    tri = (col < row).astype(F32).astype(BF16)
    prefix = jnp.dot(tri, c, preferred_element_type=F32) + carry_ref[0:1, :]
    r1 = jnp.sum(prefix * oh1, axis=-1, keepdims=True)
    r2 = jnp.sum(prefix * oh2, axis=-1, keepdims=True)
    rank_ref[...] = jnp.where(lane == 0, r1, jnp.where(lane == 1, r2, 0.0))
    total = carry_ref[0:1, :] + jnp.sum(oh1 + oh2, axis=0, keepdims=True)
    carry_ref[...] = jnp.broadcast_to(total, carry_ref.shape)
    cnt_ref[...] = jnp.broadcast_to(total, cnt_ref.shape)


def _expert_ranks(route):
    n = route.shape[0]
    t = _pick(n, (512, 256, 128))
    return pl.pallas_call(
        _rank_kernel,
        out_shape=[jax.ShapeDtypeStruct((n, LANES), F32), jax.ShapeDtypeStruct((8, LANES), F32)],
        grid=(n // t,),
        in_specs=[pl.BlockSpec((t, LANES), lambda i: (i, 0))],
        out_specs=[pl.BlockSpec((t, LANES), lambda i: (i, 0)),
                   pl.BlockSpec((8, LANES), lambda i: (0, 0))],
        scratch_shapes=[pltpu.VMEM((8, LANES), F32)],
        compiler_params=_cparams(("arbitrary",)),
        name="expert_ranks",
    )(route)


def _dispatch_kernel(pos_ref, h_hbm, zeros_hbm, xs_hbm, sem, *, tt):
    del zeros_hbm
    t0 = pl.program_id(0) * tt

    def row_copy(t, p):
        return pltpu.make_async_copy(h_hbm.at[pl.ds(t, 1)], xs_hbm.at[pl.ds(p, 1)], sem)

    def issue(i, _):
        for k in range(TOP_K):
            row_copy(t0 + i, pos_ref[TOP_K * i + k]).start()
        return 0

    lax.fori_loop(0, tt, issue, 0)

    def drain(i, _):
        row_copy(0, 0).wait()
        return 0

    lax.fori_loop(0, TOP_K * tt, drain, 0)


def _dispatch(h2, pos_flat, p_rows):
    n, d = h2.shape
    tt = _pick(n, (512, 256, 128))
    zeros = jnp.zeros((p_rows, d), h2.dtype)
    return pl.pallas_call(
        functools.partial(_dispatch_kernel, tt=tt),
        out_shape=jax.ShapeDtypeStruct((p_rows, d), h2.dtype),
        grid=(n // tt,),
        in_specs=[pl.BlockSpec((TOP_K * tt,), lambda i: (i,), memory_space=pltpu.SMEM),
                  pl.BlockSpec(memory_space=pl.ANY),
                  pl.BlockSpec(memory_space=pl.ANY)],
        out_specs=pl.BlockSpec(memory_space=pl.ANY),
        scratch_shapes=[pltpu.SemaphoreType.DMA],
        input_output_aliases={2: 0},
        compiler_params=_cparams(("arbitrary",)),
        name="moe_dispatch",
    )(pos_flat, h2, zeros)


def _combine_kernel(pos_ref, y_hbm, route_ref, x_ref, gpost_ref, gt_ref, o_ref, buf_ref, sem, *, tt):
    def row_copy(p, k, i):
        return pltpu.make_async_copy(y_hbm.at[pl.ds(p, 1)], buf_ref.at[k, pl.ds(i, 1)], sem)

    def issue(i, _):
        for k in range(TOP_K):
            row_copy(pos_ref[TOP_K * i + k], k, i).start()
        return 0

    lax.fori_loop(0, tt, issue, 0)

    def drain(i, _):
        row_copy(0, 0, 0).wait()
        return 0

    lax.fori_loop(0, TOP_K * tt, drain, 0)
    r = route_ref[0]
    y = r[:, 2:3] * buf_ref[0] + r[:, 3:4] * buf_ref[1]
    o_ref[0] = x_ref[0] + gt_ref[0] * (_rms(y) * gpost_ref[...])


def _combine(y, pos_flat, route, x1, gpost, gt):
    b, l, d = x1.shape
    tt = _pick(l, (256, 128))
    nt = l // tt
    row = pl.BlockSpec((1, tt, d), lambda bi, i: (bi, i, 0))
    return pl.pallas_call(
        functools.partial(_combine_kernel, tt=tt),
        out_shape=jax.ShapeDtypeStruct((b, l, d), F32),
        grid=(b, nt),
        in_specs=[pl.BlockSpec((TOP_K * tt,), lambda bi, i: (bi * nt + i,), memory_space=pltpu.SMEM),
                  pl.BlockSpec(memory_space=pl.ANY),
                  pl.BlockSpec((1, tt, LANES), lambda bi, i: (bi, i, 0)),
                  row,
                  pl.BlockSpec((1, d), lambda bi, i: (0, 0)),
                  pl.BlockSpec((1, 1, d), lambda bi, i: (bi, 0, 0))],
        out_specs=row,
        scratch_shapes=[pltpu.VMEM((TOP_K, tt, d), F32), pltpu.SemaphoreType.DMA],
        compiler_params=_cparams(("arbitrary", "arbitrary")),
        name="moe_combine",
    )(pos_flat, y, route, x1, gpost, gt)


def _moe_ffn(h2, route, w1, w3, w2, x1, gpost, gt):
    b, l, d = x1.shape
    n = b * l
    tm = 512
    route2 = route.reshape(n, LANES)
    rank, counts = _expert_ranks(route2)
    cnt = counts[0, :N_EXPERTS].astype(jnp.int32)
    tiles = (cnt + tm - 1) // tm
    tile_end = jnp.cumsum(tiles)
    row_off = (tile_end - tiles) * tm
    e_idx = route2[:, :TOP_K].astype(jnp.int32)
    pos = (row_off[e_idx] + rank[:, :TOP_K].astype(jnp.int32)).reshape(-1)
    n_tiles = (n * TOP_K) // tm + N_EXPERTS
    nact = tile_end[-1:]
    tile_ids = jnp.minimum(jnp.arange(n_tiles, dtype=jnp.int32), nact[0] - 1)
    tile_e = jnp.sum((tile_ids[:, None] >= tile_end[None, :]).astype(jnp.int32), axis=1)
    xs = _dispatch(h2.reshape(n, d), pos, n_tiles * tm)
    y = _ffn_expert(xs, w1, w3, w2, tile_e, nact.astype(jnp.int32), tm)
    return _combine(y, pos, route, x1, gpost, gt)


def _rope_tables(l, dims_per_map):
    lane = jnp.arange(LANES)
    d = lane % dims_per_map
    part_w = dims_per_map // 2
    half = part_w // 2
    is_col = (d // part_w) == 1
    first = (d % part_w) < half
    inv_freq = ROPE_THETA ** (-jnp.arange(half, dtype=F32) / half)
    freq = inv_freq[d % half]
    pos = jnp.arange(l)
    p = jnp.where(is_col[None, :], (pos % GRID_W)[:, None], (pos // GRID_W)[:, None]).astype(F32)
    ang = p * freq[None, :]
    cos, sin = jnp.cos(ang), jnp.sin(ang)
    zero = jnp.zeros_like(sin)
    return cos, jnp.where(first[None, :], -sin, zero), jnp.where(first[None, :], zero, sin)


def _identity_tables(l):
    return jnp.ones((l, LANES), F32), jnp.zeros((l, LANES), F32), jnp.zeros((l, LANES), F32)


def kernel(x, c, ctx, c_ctx, w_mod, b_mod, g_mix_pre, g_mix_post, g_ffn_pre, g_ffn_post,
           w_in, w_out, na_rpb, gqa_q_norm, gqa_k_norm, diff_lambda_q1, diff_lambda_k1,
           diff_lambda_q2, diff_lambda_k2, diff_subln, ffn_w1, ffn_w3, ffn_w2,
           moe_router, moe_w1, moe_w3, moe_w2):
    b, l, d = x.shape
    cn = ctx.shape[1]
    depth = w_mod.shape[0]
    n_heads = d // HEAD_DIM
    n_na, n_gq, n_diff = n_heads // 4, n_heads // 2, n_heads // 4
    n_gkv = n_gq // 4
    group = n_gq // n_gkv
    dims = (n_na, n_gq, n_gkv, n_diff)

    rows_pad = -(-(b + 1) // 8) * 8
    cvec = jnp.zeros((rows_pad, d), F32).at[:b].set(c).at[b].set(c_ctx)
    mods = _modulation(cvec, w_mod, b_mod)
    bias_tab = _na_bias_table(na_rpb.reshape((depth * n_na,) + na_rpb.shape[2:]))
    bias_tab = bias_tab.reshape((depth, n_na) + bias_tab.shape[1:])

    tabs_g = _rope_tables(l, HEAD_DIM)
    tabs_d = _rope_tables(l, DIFF_QK_DIM)
    tabs_id = _identity_tables(cn)

    x_ctx = ctx
    for li in range(depth):
        last = li == depth - 1
        lam_init = 0.8 - 0.6 * math.exp(-0.3 * li)
        m6 = mods[li].reshape(rows_pad, 6, d)
        sh1, sc1, gt1, sh2, sc2, gt2 = [m6[:b, k][:, None, :] for k in range(6)]
        csh1, csc1, cgt1, csh2, csc2, cgt2 = [
            jnp.broadcast_to(m6[b, k][None, None, :], (b, 1, d)) for k in range(6)]
        vec = lambda a: a[li].reshape(1, -1)
        w_in_b = w_in[li].astype(BF16)
        w_out_b = w_out[li].astype(BF16)
        qn, kn = vec(gqa_q_norm), vec(gqa_k_norm)
        diff_args = (vec(diff_lambda_q1), vec(diff_lambda_k1), vec(diff_lambda_q2),
                     vec(diff_lambda_k2), vec(diff_subln), lam_init)

        (naq, nak, nav, gq, gk, gv, dq, dk, dv) = _in_proj(
            x, vec(g_mix_pre), sh1, sc1, w_in_b, qn, kn, tabs_g, tabs_d, dims)
        (naqc, nakc, navc, gqc, gkc, gvc, dqc, dkc, dvc) = _in_proj(
            x_ctx, vec(g_mix_pre), csh1, csc1, w_in_b, qn, kn, tabs_id, tabs_id, dims)

        oa = _na_attention(naq, nak, nav, nakc, navc, bias_tab[li])
        ob = _attention(gq, [(gk, gv), (gkc, gvc)], group=group, tq=_pick(l, (128,)))
        oc = _attention(dq, [(dk, dv), (dkc, dvc)], group=1, tq=_pick(l, (256, 128)), diff=diff_args)

        routed = li % 2 == 1
        if routed:
            router = jnp.zeros((d, LANES), F32).at[:, :N_EXPERTS].set(moe_router[li // 2])
            x1, h2, route = _out_proj([oa, ob, oc], w_out_b, x, vec(g_mix_post), gt1,
                                      vec(g_ffn_pre), sh2, sc2, router=router)
        else:
            x1, h2 = _out_proj([oa, ob, oc], w_out_b, x, vec(g_mix_post), gt1,
                               vec(g_ffn_pre), sh2, sc2)

        if not last:
            tqc = _pick(cn, (256, 128))
            oac = _attention(naqc, [(nakc, navc)], group=1, tq=tqc)
            obc = _attention(gqc, [(gkc, gvc)], group=group, tq=_pick(cn, (128,)))
            occ = _attention(dqc, [(dkc, dvc)], group=1, tq=tqc, diff=diff_args)
            if routed:
                router = jnp.zeros((d, LANES), F32).at[:, :N_EXPERTS].set(moe_router[li // 2])
                xc1, h2c, route_c = _out_proj([oac, obc, occ], w_out_b, x_ctx, vec(g_mix_post),
                                              cgt1, vec(g_ffn_pre), csh2, csc2, router=router)
            else:
                xc1, h2c = _out_proj([oac, obc, occ], w_out_b, x_ctx, vec(g_mix_post), cgt1,
                                     vec(g_ffn_pre), csh2, csc2)

        if routed:
            w1 = moe_w1[li // 2].astype(BF16)
            w3 = moe_w3[li // 2].astype(BF16)
            w2 = moe_w2[li // 2].astype(BF16)
            x = _moe_ffn(h2, route, w1, w3, w2, x1, vec(g_ffn_post), gt2)
            if not last:
                x_ctx = _moe_ffn(h2c, route_c, w1, w3, w2, xc1, vec(g_ffn_post), cgt2)
        else:
            w1 = ffn_w1[li // 2].astype(BF16)
            w3 = ffn_w3[li // 2].astype(BF16)
            w2 = ffn_w2[li // 2].astype(BF16)
            x = _ffn_dense(h2, w1, w3, w2, x1, vec(g_ffn_post), gt2)
            if not last:
                x_ctx = _ffn_dense(h2c, w1, w3, w2, xc1, vec(g_ffn_post), cgt2)
    return x
```

```python
import functools
import math

import jax
import jax.numpy as jnp
from jax import lax
from jax.experimental import pallas as pl
from jax.experimental.pallas import tpu as pltpu

GRID_W = 64
HEAD_DIM = 128
DIFF_QK_DIM = HEAD_DIM // 2
NA_ROWS = 8
NA_COLS = 16
ROPE_THETA = 10000.0
N_EXPERTS = 8
TOP_K = 2
EPS = 1e-6

LANES = 128
NEG_BIG = -1e30
LOG2E = math.log2(math.e)
VMEM_LIMIT = 56 * 1024 * 1024

BF16 = jnp.bfloat16
F32 = jnp.float32
NT_DIMS = (((1,), (1,)), ((), ()))


def _cparams(sem):
    return pltpu.CompilerParams(dimension_semantics=sem, vmem_limit_bytes=VMEM_LIMIT)


def _pick(n, cands):
    for c in cands:
        if n % c == 0:
            return c
    raise ValueError(f"no tile in {cands} divides {n}")


def _rms(y):
    return y * lax.rsqrt(jnp.mean(y * y, axis=-1, keepdims=True) + EPS)


def _mod_kernel(c_ref, w_ref, b_ref, o_ref):
    c = c_ref[...]
    s = c / (1.0 + jnp.exp(-c))
    o_ref[0] = jnp.dot(s, w_ref[0], preferred_element_type=F32,
                       precision=lax.Precision.HIGHEST) + b_ref[0]


def _modulation(cvec, w_mod, b_mod):
    depth, d, n = w_mod.shape
    r = cvec.shape[0]
    tn = _pick(n, (1024, 512, 256, 128))
    return pl.pallas_call(
        _mod_kernel,
        out_shape=jax.ShapeDtypeStruct((depth, r, n), F32),
        grid=(depth, n // tn),
        in_specs=[pl.BlockSpec((r, d), lambda l, j: (0, 0)),
                  pl.BlockSpec((1, d, tn), lambda l, j: (l, 0, j)),
                  pl.BlockSpec((1, 1, tn), lambda l, j: (l, 0, j))],
        out_specs=pl.BlockSpec((1, r, tn), lambda l, j: (l, 0, j)),
        compiler_params=_cparams(("parallel", "parallel")),
        name="adaln_mod",
    )(cvec, w_mod, b_mod.reshape(depth, 1, n))


def _na_bias_kernel(rpb_ref, o_ref):
    n_dc = 2 * NA_COLS - 1
    base = pl.program_id(0) * ((2 * NA_ROWS - 1) * n_dc)
    shape = (GRID_W, LANES)
    qc = lax.broadcasted_iota(jnp.int32, shape, 0)
    lane = lax.broadcasted_iota(jnp.int32, shape, 1)
    half = lane >> 6
    kc = lane & (GRID_W - 1)
    dc = jnp.clip(kc - qc, -(NA_COLS - 1), NA_COLS - 1) + (NA_COLS - 1)
    cs = jnp.clip(qc - NA_COLS // 2, 0, GRID_W - NA_COLS)
    ok = (kc >= cs) & (kc < cs + NA_COLS)
    code = jnp.where(ok, half * 32 + dc, -1)
    for d in range(2 * NA_ROWS - 2):
        acc = jnp.full(shape, NEG_BIG, F32)
        for hf in range(2):
            for j in range(n_dc):
                val = rpb_ref[base + (d + hf) * n_dc + j] * LOG2E
                acc = jnp.where(code == hf * 32 + j, val, acc)
        o_ref[0, d] = acc


def _na_bias_table(rpb):
    nh = rpb.shape[0]
    return pl.pallas_call(
        _na_bias_kernel,
        out_shape=jax.ShapeDtypeStruct((nh, 2 * NA_ROWS - 2, GRID_W, LANES), F32),
        grid=(nh,),
        in_specs=[pl.BlockSpec(memory_space=pltpu.SMEM)],
        out_specs=pl.BlockSpec((1, 2 * NA_ROWS - 2, GRID_W, LANES), lambda h: (h, 0, 0, 0)),
        compiler_params=_cparams(("parallel",)),
        name="na_bias_table",
    )(rpb.reshape(-1))


def _rope(y, cos, sin_hi, sin_lo, dist):
    return (y * cos + pltpu.roll(y, LANES - dist, 1) * sin_hi
            + pltpu.roll(y, dist, 1) * sin_lo)


def _in_proj_kernel(x_ref, g_ref, sh_ref, sc_ref, w_ref, qn_ref, kn_ref,
                    cg_ref, sga_ref, sgb_ref, cd_ref, sda_ref, sdb_ref,
                    naq_ref, nak_ref, nav_ref, gq_ref, gk_ref, gv_ref,
                    dq_ref, dk_ref, dv_ref, *, n_na, n_gq, n_gkv, n_diff):
    x = x_ref[0]
    h = _rms(x) * g_ref[...]
    h = h * (1.0 + sc_ref[0]) + sh_ref[0]
    hb = h.astype(BF16)

    def proj(c0, nheads):
        return jnp.dot(hb, w_ref[:, c0:c0 + nheads * HEAD_DIM], preferred_element_type=F32)

    def heads(y, nheads):
        return [y[:, i * HEAD_DIM:(i + 1) * HEAD_DIM] for i in range(nheads)]

    na_scale = HEAD_DIM ** -0.5 * LOG2E
    diff_scale = DIFF_QK_DIM ** -0.5 * LOG2E
    c0 = 0
    for i, y in enumerate(heads(proj(c0, n_na), n_na)):
        naq_ref[0, i] = (y * na_scale).astype(BF16)
    c0 += n_na * HEAD_DIM
    for i, y in enumerate(heads(proj(c0, n_na), n_na)):
        nak_ref[0, i] = y.astype(BF16)
    c0 += n_na * HEAD_DIM
    lane = lax.broadcasted_iota(jnp.int32, (x.shape[0], LANES), 1)
    ones_col = jnp.where(lane == 0, 1.0, 0.0).astype(BF16)
    for i, y in enumerate(heads(proj(c0, n_na), n_na)):
        nav_ref[0, i, :, :HEAD_DIM] = y.astype(BF16)
        nav_ref[0, i, :, HEAD_DIM:] = ones_col
    c0 += n_na * HEAD_DIM

    cg, sga, sgb = cg_ref[...], sga_ref[...], sgb_ref[...]
    for i, y in enumerate(heads(proj(c0, n_gq), n_gq)):
        y = _rope(_rms(y) * qn_ref[...], cg, sga, sgb, 32)
        gq_ref[0, i] = (y * na_scale).astype(BF16)
    c0 += n_gq * HEAD_DIM
    for i, y in enumerate(heads(proj(c0, n_gkv), n_gkv)):
        y = _rope(_rms(y) * kn_ref[...], cg, sga, sgb, 32)
        gk_ref[0, i] = y.astype(BF16)
    c0 += n_gkv * HEAD_DIM
    for i, y in enumerate(heads(proj(c0, n_gkv), n_gkv)):
        gv_ref[0, i, :, :HEAD_DIM] = y.astype(BF16)
        gv_ref[0, i, :, HEAD_DIM:] = ones_col
    c0 += n_gkv * HEAD_DIM

    cd, sda, sdb = cd_ref[...], sda_ref[...], sdb_ref[...]
    for i, y in enumerate(heads(proj(c0, n_diff), n_diff)):
        dq_ref[0, i] = (_rope(y, cd, sda, sdb, 16) * diff_scale).astype(BF16)
    c0 += n_diff * HEAD_DIM
    for i, y in enumerate(heads(proj(c0, n_diff), n_diff)):
        dk_ref[0, i] = _rope(y, cd, sda, sdb, 16).astype(BF16)
    c0 += n_diff * HEAD_DIM
    for i, y in enumerate(heads(proj(c0, n_diff), n_diff)):
        dv_ref[0, i, :, :HEAD_DIM] = y.astype(BF16)
        dv_ref[0, i, :, HEAD_DIM:] = ones_col


def _in_proj(x, g, sh, sc, w_in, qn, kn, tabs_g, tabs_d, dims):
    b, l, d = x.shape
    n_na, n_gq, n_gkv, n_diff = dims
    n_in = w_in.shape[1]
    tm = _pick(l, (256, 128))
    vec = pl.BlockSpec((1, d), lambda bi, i: (0, 0))
    mod = pl.BlockSpec((1, 1, d), lambda bi, i: (bi, 0, 0))
    hvec = pl.BlockSpec((1, HEAD_DIM), lambda bi, i: (0, 0))
    tab = pl.BlockSpec((tm, LANES), lambda bi, i: (i, 0))

    def out(nh, width=HEAD_DIM):
        return (jax.ShapeDtypeStruct((b, nh, l, width), BF16),
                pl.BlockSpec((1, nh, tm, width), lambda bi, i: (bi, 0, i, 0)))

    outs = [out(n_na), out(n_na), out(n_na, 2 * HEAD_DIM), out(n_gq), out(n_gkv),
            out(n_gkv, 2 * HEAD_DIM), out(n_diff), out(n_diff), out(n_diff, 2 * HEAD_DIM)]
    kern = functools.partial(_in_proj_kernel, n_na=n_na, n_gq=n_gq, n_gkv=n_gkv, n_diff=n_diff)
    return pl.pallas_call(
        kern,
        out_shape=[o[0] for o in outs],
        grid=(b, l // tm),
        in_specs=[pl.BlockSpec((1, tm, d), lambda bi, i: (bi, i, 0)), vec, mod, mod,
                  pl.BlockSpec((d, n_in), lambda bi, i: (0, 0), pipeline_mode=pl.Buffered(1)),
                  hvec, hvec, tab, tab, tab, tab, tab, tab],
        out_specs=[o[1] for o in outs],
        compiler_params=_cparams(("parallel", "parallel")),
        name="in_proj",
    )(x, g, sh, sc, w_in, qn, kn, *tabs_g, *tabs_d)


def _attn_kernel(*refs, n_seg, group, tq, chunks, diff, lam_init):
    q_ref = refs[0]
    kv_refs = refs[1:1 + 2 * n_seg]
    rest = refs[1 + 2 * n_seg:]
    if diff:
        lq1_ref, lk1_ref, lq2_ref, lk2_ref, sub_ref, o_ref = rest
        q = q_ref[0, 0].astype(F32)
        lane = lax.broadcasted_iota(jnp.int32, q.shape, 1)
        qq = jnp.concatenate([jnp.where(lane < DIFF_QK_DIM, q, 0.0),
                              jnp.where(lane >= DIFF_QK_DIM, q, 0.0)], axis=0).astype(BF16)
    else:
        (o_ref,) = rest
        qq = q_ref[0].reshape(group * tq, HEAD_DIM)
    m_rows = qq.shape[0]

    def step(k, v, carry):
        m, acc = carry
        s = lax.dot_general(qq, k, NT_DIMS, preferred_element_type=F32)
        m_new = jnp.maximum(m, jnp.max(s, axis=-1, keepdims=True))
        p = jnp.exp2((s - m_new).astype(BF16))
        acc = jnp.exp2(m - m_new) * acc + jnp.dot(p, v, preferred_element_type=F32)
        return m_new, acc

    carry = (jnp.full((m_rows, 1), NEG_BIG, F32), jnp.zeros((m_rows, 2 * HEAD_DIM), F32))
    for si in range(n_seg):
        k_ref, v_ref = kv_refs[2 * si], kv_refs[2 * si + 1]
        n_chunks, tk = chunks[si]
        for c in range(n_chunks):
            rows = slice(c * tk, (c + 1) * tk)
            carry = step(k_ref[0, 0, rows, :], v_ref[0, 0, rows, :], carry)
    _, acc = carry
    o = acc[:, :HEAD_DIM] / acc[:, HEAD_DIM:HEAD_DIM + 1]
    if diff:
        lam = (jnp.exp(jnp.sum(lq1_ref[...] * lk1_ref[...], axis=-1, keepdims=True))
               - jnp.exp(jnp.sum(lq2_ref[...] * lk2_ref[...], axis=-1, keepdims=True)) + lam_init)
        od = o[:tq] - lam * o[tq:]
        o_ref[0] = (_rms(od) * sub_ref[...] * (1.0 - lam_init)).astype(BF16)
    else:
        for g in range(group):
            o_ref[0, :, g * HEAD_DIM:(g + 1) * HEAD_DIM] = o[g * tq:(g + 1) * tq].astype(BF16)


def _attention(q, segs, *, group, tq, diff=None):
    b, hq, lq, _ = q.shape
    hkv = hq // group
    kv_args, kv_specs, chunks = [], [], []
    for k, v in segs:
        s = k.shape[2]
        tk = _pick(s, (512, 256, 128))
        chunks.append((s // tk, tk))
        kv_args += [k, v]
        kv_specs += [pl.BlockSpec((1, 1, s, a.shape[3]), lambda bi, h, i: (bi, h, 0, 0))
                     for a in (k, v)]
    extra_args, extra_specs, lam_init = [], [], 0.0
    if diff is not None:
        lq1, lk1, lq2, lk2, subln, lam_init = diff
        extra_args = [lq1, lk1, lq2, lk2, subln]
        extra_specs = [pl.BlockSpec((1, a.shape[1]), lambda bi, h, i: (0, 0)) for a in extra_args]
    kern = functools.partial(_attn_kernel, n_seg=len(segs), group=group, tq=tq,
                             chunks=tuple(chunks), diff=diff is not None, lam_init=lam_init)
    return pl.pallas_call(
        kern,
        out_shape=jax.ShapeDtypeStruct((b, lq, hq * HEAD_DIM), BF16),
        grid=(b, hkv, lq // tq),
        in_specs=[pl.BlockSpec((1, group, tq, HEAD_DIM), lambda bi, h, i: (bi, h, i, 0))]
                 + kv_specs + extra_specs,
        out_specs=pl.BlockSpec((1, tq, group * HEAD_DIM), lambda bi, h, i: (bi, i, h)),
        compiler_params=_cparams(("parallel", "parallel", "parallel")),
        name="diff_attn" if diff is not None else "gqa_attn",
    )(q, *kv_args, *extra_args)


def _na_kernel(q_ref, k_ref, v_ref, kc_ref, vc_ref, bias_ref, o_ref, *, rows):
    band = NA_ROWS * GRID_W
    kc = kc_ref[0, 0]
    vc = vc_ref[0, 0]

    def body(r, _):
        rs = jnp.clip(r - NA_ROWS // 2, 0, rows - NA_ROWS)
        off = rs - r + (NA_ROWS - 1)
        qrows = pl.ds(pl.multiple_of(r * GRID_W, GRID_W), GRID_W)
        krows = pl.ds(pl.multiple_of(rs * GRID_W, GRID_W), band)
        q = q_ref[0, 0, qrows, :]
        bias = jnp.concatenate([bias_ref[0, off + 2 * j] for j in range(NA_ROWS // 2)], axis=1)
        s = lax.dot_general(q, k_ref[0, 0, krows, :], NT_DIMS, preferred_element_type=F32) + bias
        sc = lax.dot_general(q, kc, NT_DIMS, preferred_element_type=F32)
        m = jnp.maximum(jnp.max(s, axis=-1, keepdims=True), jnp.max(sc, axis=-1, keepdims=True))
        p = jnp.exp2((s - m).astype(BF16))
        pc = jnp.exp2((sc - m).astype(BF16))
        o = (jnp.dot(p, v_ref[0, 0, krows, :], preferred_element_type=F32)
             + jnp.dot(pc, vc, preferred_element_type=F32))
        o_ref[0, qrows, :] = (o[:, :HEAD_DIM] / o[:, HEAD_DIM:HEAD_DIM + 1]).astype(BF16)
        return 0

    lax.fori_loop(0, rows, body, 0, unroll=4)


def _na_attention(q, k, v, kc, vc, bias):
    b, nh, l, _ = q.shape
    rows = l // GRID_W

    def full(a):
        return pl.BlockSpec((1, 1) + a.shape[2:], lambda bi, h: (bi, h, 0, 0))

    return pl.pallas_call(
        functools.partial(_na_kernel, rows=rows),
        out_shape=jax.ShapeDtypeStruct((b, l, nh * HEAD_DIM), BF16),
        grid=(b, nh),
        in_specs=[full(q), full(k), full(v), full(kc), full(vc),
                  pl.BlockSpec((1,) + bias.shape[1:], lambda bi, h: (h, 0, 0, 0))],
        out_specs=pl.BlockSpec((1, l, HEAD_DIM), lambda bi, h: (bi, 0, h)),
        compiler_params=_cparams(("parallel", "parallel")),
        name="na_attn",
    )(q, k, v, kc, vc, bias)


def _out_proj_kernel(*refs, n_o, routed):
    o_refs = refs[:n_o]
    (w_ref, x_ref, gpost_ref, gt_ref, gpre_ref, sh_ref, sc_ref) = refs[n_o:n_o + 7]
    rest = refs[n_o + 7:]
    if routed:
        router_ref, x1_ref, h2_ref, route_ref = rest
    else:
        x1_ref, h2_ref = rest
    acc = None
    c0 = 0
    for o_ref in o_refs:
        width = o_ref.shape[-1]
        part = jnp.dot(o_ref[0], w_ref[c0:c0 + width, :], preferred_element_type=F32)
        acc = part if acc is None else acc + part
        c0 += width
    x1 = x_ref[0] + gt_ref[0] * (_rms(acc) * gpost_ref[...])
    x1_ref[0] = x1
    h2 = (_rms(x1) * gpre_ref[...]) * (1.0 + sc_ref[0]) + sh_ref[0]
    h2_ref[0] = h2.astype(h2_ref.dtype)
    if routed:
        h_hi = h2.astype(BF16)
        h_lo = (h2 - h_hi.astype(F32)).astype(BF16)
        r = router_ref[...]
        r_hi = r.astype(BF16)
        r_lo = (r - r_hi.astype(F32)).astype(BF16)
        logits = (jnp.dot(h_hi, r_hi, preferred_element_type=F32)
                  + jnp.dot(h_hi, r_lo, preferred_element_type=F32)
                  + jnp.dot(h_lo, r_hi, preferred_element_type=F32))
        lane = lax.broadcasted_iota(jnp.int32, logits.shape, 1)
        logits = jnp.where(lane < N_EXPERTS, logits, NEG_BIG)
        v1 = jnp.max(logits, axis=-1, keepdims=True)
        i1 = jnp.min(jnp.where(logits == v1, lane, LANES), axis=-1, keepdims=True)
        rest_l = jnp.where(lane == i1, NEG_BIG, logits)
        v2 = jnp.max(rest_l, axis=-1, keepdims=True)
        i2 = jnp.min(jnp.where(rest_l == v2, lane, LANES), axis=-1, keepdims=True)
        e2 = jnp.exp(v2 - v1)
        g1 = 1.0 / (1.0 + e2)
        g2 = e2 / (1.0 + e2)
        route_ref[0] = jnp.where(lane == 0, i1.astype(F32),
                       jnp.where(lane == 1, i2.astype(F32),
                       jnp.where(lane == 2, g1, jnp.where(lane == 3, g2, 0.0))))


def _out_proj(o_list, w_out, x, gpost, gt, gpre, sh, sc, router=None):
    b, l, d = x.shape
    tm = _pick(l, (256, 128))
    routed = router is not None
    vec = pl.BlockSpec((1, d), lambda bi, i: (0, 0))
    mod = pl.BlockSpec((1, 1, d), lambda bi, i: (bi, 0, 0))
    row = pl.BlockSpec((1, tm, d), lambda bi, i: (bi, i, 0))
    in_specs = [pl.BlockSpec((1, tm, o.shape[-1]), lambda bi, i: (bi, i, 0)) for o in o_list]
    in_specs += [pl.BlockSpec(w_out.shape, lambda bi, i: (0, 0), pipeline_mode=pl.Buffered(1)),
                 row, vec, mod, vec, mod, mod]
    args = list(o_list) + [w_out, x, gpost, gt, gpre, sh, sc]
    out_shape = [jax.ShapeDtypeStruct((b, l, d), F32),
                 jax.ShapeDtypeStruct((b, l, d), F32 if routed else BF16)]
    out_specs = [row, row]
    if routed:
        in_specs.append(pl.BlockSpec(router.shape, lambda bi, i: (0, 0)))
        args.append(router)
        out_shape.append(jax.ShapeDtypeStruct((b, l, LANES), F32))
        out_specs.append(pl.BlockSpec((1, tm, LANES), lambda bi, i: (bi, i, 0)))
    return pl.pallas_call(
        functools.partial(_out_proj_kernel, n_o=len(o_list), routed=routed),
        out_shape=out_shape,
        grid=(b, l // tm),
        in_specs=in_specs,
        out_specs=out_specs,
        compiler_params=_cparams(("parallel", "parallel")),
        name="out_proj",
    )(*args)


def _swiglu_step(hb, w1, w3, w2):
    a = jnp.dot(hb, w1, preferred_element_type=F32)
    g = jnp.dot(hb, w3, preferred_element_type=F32)
    act = (a / (1.0 + jnp.exp(-a))) * g
    return jnp.dot(act.astype(BF16), w2, preferred_element_type=F32)


def _ffn_dense_kernel(h_ref, w1_ref, w3_ref, w2_ref, x_ref, gpost_ref, gt_ref, o_ref, acc_ref):
    f = pl.program_id(2)

    @pl.when(f == 0)
    def _():
        acc_ref[...] = jnp.zeros_like(acc_ref)

    acc_ref[...] += _swiglu_step(h_ref[0], w1_ref[...], w3_ref[...], w2_ref[...])

    @pl.when(f == pl.num_programs(2) - 1)
    def _():
        o_ref[0] = x_ref[0] + gt_ref[0] * (_rms(acc_ref[...]) * gpost_ref[...])


def _ffn_dense(h2, w1, w3, w2, x1, gpost, gt):
    b, l, d = x1.shape
    dff = w1.shape[1]
    tm = _pick(l, (512, 256, 128))
    tf = _pick(dff, (512, 256, 128))
    row = pl.BlockSpec((1, tm, d), lambda bi, i, f: (bi, i, 0))
    return pl.pallas_call(
        _ffn_dense_kernel,
        out_shape=jax.ShapeDtypeStruct((b, l, d), F32),
        grid=(b, l // tm, dff // tf),
        in_specs=[row,
                  pl.BlockSpec((d, tf), lambda bi, i, f: (0, f)),
                  pl.BlockSpec((d, tf), lambda bi, i, f: (0, f)),
                  pl.BlockSpec((tf, d), lambda bi, i, f: (f, 0)),
                  row,
                  pl.BlockSpec((1, d), lambda bi, i, f: (0, 0)),
                  pl.BlockSpec((1, 1, d), lambda bi, i, f: (bi, 0, 0))],
        out_specs=row,
        scratch_shapes=[pltpu.VMEM((tm, d), F32)],
        compiler_params=_cparams(("parallel", "parallel", "arbitrary")),
        name="ffn_dense",
    )(h2, w1, w3, w2, x1, gpost, gt)


def _ffn_expert_kernel(tile_e_ref, nact_ref, x_ref, w1_ref, w3_ref, w2_ref, o_ref, hb_ref, acc_ref):
    j = pl.program_id(0)
    f = pl.program_id(1)

    @pl.when(j < nact_ref[0])
    def _():
        @pl.when(f == 0)
        def _():
            hb_ref[...] = x_ref[...].astype(BF16)
            acc_ref[...] = jnp.zeros_like(acc_ref)

        acc_ref[...] += _swiglu_step(hb_ref[...], w1_ref[0], w3_ref[0], w2_ref[0])

        @pl.when(f == pl.num_programs(1) - 1)
        def _():
            o_ref[...] = acc_ref[...]

    @pl.when((j >= nact_ref[0]) & (f == pl.num_programs(1) - 1))
    def _():
        o_ref[...] = jnp.zeros_like(o_ref)


def _ffn_expert(xs, w1, w3, w2, tile_e, nact, tm):
    p, d = xs.shape
    dff = w1.shape[2]
    tf = _pick(dff, (512, 256, 128))
    nf = dff // tf

    def row_map(j, f, te, na):
        return (jnp.minimum(j, na[0] - 1), 0)

    def fsel(j, f, na):
        return jnp.where(j < na[0], f, nf - 1)

    return pl.pallas_call(
        _ffn_expert_kernel,
        out_shape=jax.ShapeDtypeStruct((p, d), F32),
        grid_spec=pltpu.PrefetchScalarGridSpec(
            num_scalar_prefetch=2,
            grid=(p // tm, nf),
            in_specs=[pl.BlockSpec((tm, d), row_map),
                      pl.BlockSpec((1, d, tf), lambda j, f, te, na: (te[j], 0, fsel(j, f, na))),
                      pl.BlockSpec((1, d, tf), lambda j, f, te, na: (te[j], 0, fsel(j, f, na))),
                      pl.BlockSpec((1, tf, d), lambda j, f, te, na: (te[j], fsel(j, f, na), 0))],
            out_specs=pl.BlockSpec((tm, d), lambda j, f, te, na: (j, 0)),
            scratch_shapes=[pltpu.VMEM((tm, d), BF16), pltpu.VMEM((tm, d), F32)]),
        compiler_params=_cparams(("arbitrary", "arbitrary")),
        name="ffn_expert",
    )(tile_e, nact, xs, w1, w3, w2)


def _rank_kernel(route_ref, rank_ref, cnt_ref, carry_ref):
    i = pl.program_id(0)

    @pl.when(i == 0)
    def _():
        carry_ref[...] = jnp.zeros_like(carry_ref)

    r = route_ref[...]
    t = r.shape[0]
    lane = lax.broadcasted_iota(jnp.int32, r.shape, 1)
    e1 = r[:, 0:1].astype(jnp.int32)
    e2 = r[:, 1:2].astype(jnp.int32)
    oh1 = (lane == e1).astype(F32)
    oh2 = (lane == e2).astype(F32)
    c = (oh1 + oh2).astype(BF16)
    row = lax.broadcasted_iota(jnp.int32, (t, t), 0)
    col = lax.broadcasted_iota(jnp.int32, (t, t), 1)
    tri = (col < row).astype(F32).astype(BF16)
    prefix = jnp.dot(tri, c, preferred_element_type=F32) + carry_ref[0:1, :]
    r1 = jnp.sum(prefix * oh1, axis=-1, keepdims=True)
    r2 = jnp.sum(prefix * oh2, axis=-1, keepdims=True)
    rank_ref[...] = jnp.where(lane == 0, r1, jnp.where(lane == 1, r2, 0.0))
    total = carry_ref[0:1, :] + jnp.sum(oh1 + oh2, axis=0, keepdims=True)
    carry_ref[...] = jnp.broadcast_to(total, carry_ref.shape)
    cnt_ref[...] = jnp.broadcast_to(total, cnt_ref.shape)


def _expert_ranks(route):
    n = route.shape[0]
    t = _pick(n, (512, 256, 128))
    return pl.pallas_call(
        _rank_kernel,
        out_shape=[jax.ShapeDtypeStruct((n, LANES), F32), jax.ShapeDtypeStruct((8, LANES), F32)],
        grid=(n // t,),
        in_specs=[pl.BlockSpec((t, LANES), lambda i: (i, 0))],
        out_specs=[pl.BlockSpec((t, LANES), lambda i: (i, 0)),
                   pl.BlockSpec((8, LANES), lambda i: (0, 0))],
        scratch_shapes=[pltpu.VMEM((8, LANES), F32)],
        compiler_params=_cparams(("arbitrary",)),
        name="expert_ranks",
    )(route)


def _dispatch_kernel(pos_ref, h_ref, zeros_hbm, xs_hbm, sem, *, tt):
    del zeros_hbm

    def row_copy(i, p):
        return pltpu.make_async_copy(h_ref.at[pl.ds(i, 1)], xs_hbm.at[pl.ds(p, 1)], sem)

    def issue(i, _):
        for k in range(TOP_K):
            row_copy(i, pos_ref[TOP_K * i + k]).start()
        return 0

    lax.fori_loop(0, tt, issue, 0)

    def drain(i, _):
        row_copy(0, 0).wait()
        return 0

    lax.fori_loop(0, TOP_K * tt, drain, 0)


def _dispatch(h2, pos_flat, p_rows):
    n, d = h2.shape
    tt = _pick(n, (512, 256, 128))
    zeros = jnp.zeros((p_rows, d), h2.dtype)
    return pl.pallas_call(
        functools.partial(_dispatch_kernel, tt=tt),
        out_shape=jax.ShapeDtypeStruct((p_rows, d), h2.dtype),
        grid=(n // tt,),
        in_specs=[pl.BlockSpec((TOP_K * tt,), lambda i: (i,), memory_space=pltpu.SMEM),
                  pl.BlockSpec((tt, d), lambda i: (i, 0)),
                  pl.BlockSpec(memory_space=pl.ANY)],
        out_specs=pl.BlockSpec(memory_space=pl.ANY),
        scratch_shapes=[pltpu.SemaphoreType.DMA],
        input_output_aliases={2: 0},
        compiler_params=_cparams(("arbitrary",)),
        name="moe_dispatch",
    )(pos_flat, h2, zeros)


def _combine_kernel(pos_ref, y_hbm, route_ref, x_ref, gpost_ref, gt_ref, o_ref, buf_ref, sem, *, tt):
    def row_copy(p, k, i):
        return pltpu.make_async_copy(y_hbm.at[pl.ds(p, 1)], buf_ref.at[k, pl.ds(i, 1)], sem)

    def issue(i, _):
        for k in range(TOP_K):
            row_copy(pos_ref[TOP_K * i + k], k, i).start()
        return 0

    lax.fori_loop(0, tt, issue, 0)

    def drain(i, _):
        row_copy(0, 0, 0).wait()
        return 0

    lax.fori_loop(0, TOP_K * tt, drain, 0)
    r = route_ref[0]
    y = r[:, 2:3] * buf_ref[0] + r[:, 3:4] * buf_ref[1]
    o_ref[0] = x_ref[0] + gt_ref[0] * (_rms(y) * gpost_ref[...])


def _combine(y, pos_flat, route, x1, gpost, gt):
    b, l, d = x1.shape
    tt = _pick(l, (256, 128))
    nt = l // tt
    row = pl.BlockSpec((1, tt, d), lambda bi, i: (bi, i, 0))
    return pl.pallas_call(
        functools.partial(_combine_kernel, tt=tt),
        out_shape=jax.ShapeDtypeStruct((b, l, d), F32),
        grid=(b, nt),
        in_specs=[pl.BlockSpec((TOP_K * tt,), lambda bi, i: (bi * nt + i,), memory_space=pltpu.SMEM),
                  pl.BlockSpec(memory_space=pl.ANY),
                  pl.BlockSpec((1, tt, LANES), lambda bi, i: (bi, i, 0)),
                  row,
                  pl.BlockSpec((1, d), lambda bi, i: (0, 0)),
                  pl.BlockSpec((1, 1, d), lambda bi, i: (bi, 0, 0))],
        out_specs=row,
        scratch_shapes=[pltpu.VMEM((TOP_K, tt, d), F32), pltpu.SemaphoreType.DMA],
        compiler_params=_cparams(("arbitrary", "arbitrary")),
        name="moe_combine",
    )(pos_flat, y, route, x1, gpost, gt)


def _moe_ffn(h2, route, w1, w3, w2, x1, gpost, gt):
    b, l, d = x1.shape
    n = b * l
    tm = 512
    route2 = route.reshape(n, LANES)
    rank, counts = _expert_ranks(route2)
    cnt = counts[0, :N_EXPERTS].astype(jnp.int32)
    tiles = (cnt + tm - 1) // tm
    tile_end = jnp.cumsum(tiles)
    row_off = (tile_end - tiles) * tm
    e_idx = route2[:, :TOP_K].astype(jnp.int32)
    pos = (row_off[e_idx] + rank[:, :TOP_K].astype(jnp.int32)).reshape(-1)
    n_tiles = (n * TOP_K) // tm + N_EXPERTS
    nact = tile_end[-1:]
    tile_ids = jnp.minimum(jnp.arange(n_tiles, dtype=jnp.int32), nact[0] - 1)
    tile_e = jnp.sum((tile_ids[:, None] >= tile_end[None, :]).astype(jnp.int32), axis=1)
    xs = _dispatch(h2.reshape(n, d), pos, n_tiles * tm)
    y = _ffn_expert(xs, w1, w3, w2, tile_e, nact.astype(jnp.int32), tm)
    return _combine(y, pos, route, x1, gpost, gt)


def _rope_tables(l, dims_per_map):
    lane = jnp.arange(LANES)
    d = lane % dims_per_map
    part_w = dims_per_map // 2
    half = part_w // 2
    is_col = (d // part_w) == 1
    first = (d % part_w) < half
    inv_freq = ROPE_THETA ** (-jnp.arange(half, dtype=F32) / half)
    freq = inv_freq[d % half]
    pos = jnp.arange(l)
    p = jnp.where(is_col[None, :], (pos % GRID_W)[:, None], (pos // GRID_W)[:, None]).astype(F32)
    ang = p * freq[None, :]
    cos, sin = jnp.cos(ang), jnp.sin(ang)
    zero = jnp.zeros_like(sin)
    return cos, jnp.where(first[None, :], -sin, zero), jnp.where(first[None, :], zero, sin)


def _identity_tables(l):
    return jnp.ones((l, LANES), F32), jnp.zeros((l, LANES), F32), jnp.zeros((l, LANES), F32)


def kernel(x, c, ctx, c_ctx, w_mod, b_mod, g_mix_pre, g_mix_post, g_ffn_pre, g_ffn_post,
           w_in, w_out, na_rpb, gqa_q_norm, gqa_k_norm, diff_lambda_q1, diff_lambda_k1,
           diff_lambda_q2, diff_lambda_k2, diff_subln, ffn_w1, ffn_w3, ffn_w2,
           moe_router, moe_w1, moe_w3, moe_w2):
    b, l, d = x.shape
    cn = ctx.shape[1]
    depth = w_mod.shape[0]
    n_heads = d // HEAD_DIM
    n_na, n_gq, n_diff = n_heads // 4, n_heads // 2, n_heads // 4
    n_gkv = n_gq // 4
    group = n_gq // n_gkv
    dims = (n_na, n_gq, n_gkv, n_diff)

    rows_pad = -(-(b + 1) // 8) * 8
    cvec = jnp.zeros((rows_pad, d), F32).at[:b].set(c).at[b].set(c_ctx)
    mods = _modulation(cvec, w_mod, b_mod)
    bias_tab = _na_bias_table(na_rpb.reshape((depth * n_na,) + na_rpb.shape[2:]))
    bias_tab = bias_tab.reshape((depth, n_na) + bias_tab.shape[1:])

    tabs_g = _rope_tables(l, HEAD_DIM)
    tabs_d = _rope_tables(l, DIFF_QK_DIM)
    tabs_id = _identity_tables(cn)

    x_ctx = ctx
    for li in range(depth):
        last = li == depth - 1
        lam_init = 0.8 - 0.6 * math.exp(-0.3 * li)
        m6 = mods[li].reshape(rows_pad, 6, d)
        sh1, sc1, gt1, sh2, sc2, gt2 = [m6[:b, k][:, None, :] for k in range(6)]
        csh1, csc1, cgt1, csh2, csc2, cgt2 = [
            jnp.broadcast_to(m6[b, k][None, None, :], (b, 1, d)) for k in range(6)]
        vec = lambda a: a[li].reshape(1, -1)
        w_in_b = w_in[li].astype(BF16)
        w_out_b = w_out[li].astype(BF16)
        qn, kn = vec(gqa_q_norm), vec(gqa_k_norm)
        diff_args = (vec(diff_lambda_q1), vec(diff_lambda_k1), vec(diff_lambda_q2),
                     vec(diff_lambda_k2), vec(diff_subln), lam_init)

        (naq, nak, nav, gq, gk, gv, dq, dk, dv) = _in_proj(
            x, vec(g_mix_pre), sh1, sc1, w_in_b, qn, kn, tabs_g, tabs_d, dims)
        (naqc, nakc, navc, gqc, gkc, gvc, dqc, dkc, dvc) = _in_proj(
            x_ctx, vec(g_mix_pre), csh1, csc1, w_in_b, qn, kn, tabs_id, tabs_id, dims)

        oa = _na_attention(naq, nak, nav, nakc, navc, bias_tab[li])
        ob = _attention(gq, [(gk, gv), (gkc, gvc)], group=group, tq=_pick(l, (128,)))
        oc = _attention(dq, [(dk, dv), (dkc, dvc)], group=1, tq=_pick(l, (256, 128)), diff=diff_args)

        routed = li % 2 == 1
        if routed:
            router = jnp.zeros((d, LANES), F32).at[:, :N_EXPERTS].set(moe_router[li // 2])
            x1, h2, route = _out_proj([oa, ob, oc], w_out_b, x, vec(g_mix_post), gt1,
                                      vec(g_ffn_pre), sh2, sc2, router=router)
        else:
            x1, h2 = _out_proj([oa, ob, oc], w_out_b, x, vec(g_mix_post), gt1,
                               vec(g_ffn_pre), sh2, sc2)

        if not last:
            tqc = _pick(cn, (256, 128))
            oac = _attention(naqc, [(nakc, navc)], group=1, tq=tqc)
            obc = _attention(gqc, [(gkc, gvc)], group=group, tq=_pick(cn, (128,)))
            occ = _attention(dqc, [(dkc, dvc)], group=1, tq=tqc, diff=diff_args)
            if routed:
                router = jnp.zeros((d, LANES), F32).at[:, :N_EXPERTS].set(moe_router[li // 2])
                xc1, h2c, route_c = _out_proj([oac, obc, occ], w_out_b, x_ctx, vec(g_mix_post),
                                              cgt1, vec(g_ffn_pre), csh2, csc2, router=router)
            else:
                xc1, h2c = _out_proj([oac, obc, occ], w_out_b, x_ctx, vec(g_mix_post), cgt1,
                                     vec(g_ffn_pre), csh2, csc2)

        if routed:
            w1 = moe_w1[li // 2].astype(BF16)
            w3 = moe_w3[li // 2].astype(BF16)
            w2 = moe_w2[li // 2].astype(BF16)
            x = _moe_ffn(h2, route, w1, w3, w2, x1, vec(g_ffn_post), gt2)
            if not last:
                x_ctx = _moe_ffn(h2c, route_c, w1, w3, w2, xc1, vec(g_ffn_post), cgt2)
        else:
            w1 = ffn_w1[li // 2].astype(BF16)
            w3 = ffn_w3[li // 2].astype(BF16)
            w2 = ffn_w2[li // 2].astype(BF16)
            x = _ffn_dense(h2, w1, w3, w2, x1, vec(g_ffn_post), gt2)
            if not last:
                x_ctx = _ffn_dense(h2c, w1, w3, w2, xc1, vec(g_ffn_post), cgt2)
    return x
```

```python
import functools
import math

import jax
import jax.numpy as jnp
from jax import lax
from jax.experimental import pallas as pl
from jax.experimental.pallas import tpu as pltpu

GRID_W = 64
HEAD_DIM = 128
DIFF_QK_DIM = HEAD_DIM // 2
NA_ROWS = 8
NA_COLS = 16
ROPE_THETA = 10000.0
N_EXPERTS = 8
TOP_K = 2
EPS = 1e-6

LANES = 128
NEG_BIG = -1e30
LOG2E = math.log2(math.e)
VMEM_LIMIT = 56 * 1024 * 1024

BF16 = jnp.bfloat16
F32 = jnp.float32
NT_DIMS = (((1,), (1,)), ((), ()))


def _cparams(sem):
    return pltpu.CompilerParams(dimension_semantics=sem, vmem_limit_bytes=VMEM_LIMIT)


def _pick(n, cands):
    for c in cands:
        if n % c == 0:
            return c
    raise ValueError(f"no tile in {cands} divides {n}")


def _rms(y):
    return y * lax.rsqrt(jnp.mean(y * y, axis=-1, keepdims=True) + EPS)


def _mod_kernel(c_ref, w_ref, b_ref, o_ref):
    c = c_ref[...]
    s = c / (1.0 + jnp.exp(-c))
    o_ref[0] = jnp.dot(s, w_ref[0], preferred_element_type=F32,
                       precision=lax.Precision.HIGHEST) + b_ref[0]


def _modulation(cvec, w_mod, b_mod):
    depth, d, n = w_mod.shape
    r = cvec.shape[0]
    tn = _pick(n, (1024, 512, 256, 128))
    return pl.pallas_call(
        _mod_kernel,
        out_shape=jax.ShapeDtypeStruct((depth, r, n), F32),
        grid=(depth, n // tn),
        in_specs=[pl.BlockSpec((r, d), lambda l, j: (0, 0)),
                  pl.BlockSpec((1, d, tn), lambda l, j: (l, 0, j)),
                  pl.BlockSpec((1, 1, tn), lambda l, j: (l, 0, j))],
        out_specs=pl.BlockSpec((1, r, tn), lambda l, j: (l, 0, j)),
        compiler_params=_cparams(("parallel", "parallel")),
        name="adaln_mod",
    )(cvec, w_mod, b_mod.reshape(depth, 1, n))


def _na_bias_kernel(rpb_ref, o_ref):
    n_dc = 2 * NA_COLS - 1
    base = pl.program_id(0) * ((2 * NA_ROWS - 1) * n_dc)
    shape = (GRID_W, LANES)
    qc = lax.broadcasted_iota(jnp.int32, shape, 0)
    lane = lax.broadcasted_iota(jnp.int32, shape, 1)
    half = lane >> 6
    kc = lane & (GRID_W - 1)
    dc = jnp.clip(kc - qc, -(NA_COLS - 1), NA_COLS - 1) + (NA_COLS - 1)
    cs = jnp.clip(qc - NA_COLS // 2, 0, GRID_W - NA_COLS)
    ok = (kc >= cs) & (kc < cs + NA_COLS)
    code = jnp.where(ok, half * 32 + dc, -1)
    for d in range(2 * NA_ROWS - 2):
        acc = jnp.full(shape, NEG_BIG, F32)
        for hf in range(2):
            for j in range(n_dc):
                val = rpb_ref[base + (d + hf) * n_dc + j] * LOG2E
                acc = jnp.where(code == hf * 32 + j, val, acc)
        o_ref[0, d] = acc


def _na_bias_table(rpb):
    nh = rpb.shape[0]
    return pl.pallas_call(
        _na_bias_kernel,
        out_shape=jax.ShapeDtypeStruct((nh, 2 * NA_ROWS - 2, GRID_W, LANES), F32),
        grid=(nh,),
        in_specs=[pl.BlockSpec(memory_space=pltpu.SMEM)],
        out_specs=pl.BlockSpec((1, 2 * NA_ROWS - 2, GRID_W, LANES), lambda h: (h, 0, 0, 0)),
        compiler_params=_cparams(("parallel",)),
        name="na_bias_table",
    )(rpb.reshape(-1))


def _rope(y, cos, sin_hi, sin_lo, dist):
    return (y * cos + pltpu.roll(y, LANES - dist, 1) * sin_hi
            + pltpu.roll(y, dist, 1) * sin_lo)


def _in_proj_kernel(x_ref, g_ref, sh_ref, sc_ref, w_ref, qn_ref, kn_ref,
                    cg_ref, sga_ref, sgb_ref, cd_ref, sda_ref, sdb_ref,
                    naq_ref, nak_ref, nav_ref, gq_ref, gk_ref, gv_ref,
                    dq_ref, dk_ref, dv_ref, *, n_na, n_gq, n_gkv, n_diff):
    x = x_ref[0]
    h = _rms(x) * g_ref[...]
    h = h * (1.0 + sc_ref[0]) + sh_ref[0]
    hb = h.astype(BF16)

    def proj(c0, nheads):
        return jnp.dot(hb, w_ref[:, c0:c0 + nheads * HEAD_DIM], preferred_element_type=F32)

    def heads(y, nheads):
        return [y[:, i * HEAD_DIM:(i + 1) * HEAD_DIM] for i in range(nheads)]

    na_scale = HEAD_DIM ** -0.5 * LOG2E
    diff_scale = DIFF_QK_DIM ** -0.5 * LOG2E
    c0 = 0
    for i, y in enumerate(heads(proj(c0, n_na), n_na)):
        naq_ref[0, i] = (y * na_scale).astype(BF16)
    c0 += n_na * HEAD_DIM
    for i, y in enumerate(heads(proj(c0, n_na), n_na)):
        nak_ref[0, i] = y.astype(BF16)
    c0 += n_na * HEAD_DIM
    lane = lax.broadcasted_iota(jnp.int32, (x.shape[0], LANES), 1)
    ones_col = jnp.where(lane == 0, 1.0, 0.0).astype(BF16)
    for i, y in enumerate(heads(proj(c0, n_na), n_na)):
        nav_ref[0, i, :, :HEAD_DIM] = y.astype(BF16)
        nav_ref[0, i, :, HEAD_DIM:] = ones_col
    c0 += n_na * HEAD_DIM

    cg, sga, sgb = cg_ref[...], sga_ref[...], sgb_ref[...]
    for i, y in enumerate(heads(proj(c0, n_gq), n_gq)):
        y = _rope(_rms(y) * qn_ref[...], cg, sga, sgb, 32)
        gq_ref[0, i] = (y * na_scale).astype(BF16)
    c0 += n_gq * HEAD_DIM
    for i, y in enumerate(heads(proj(c0, n_gkv), n_gkv)):
        y = _rope(_rms(y) * kn_ref[...], cg, sga, sgb, 32)
        gk_ref[0, i] = y.astype(BF16)
    c0 += n_gkv * HEAD_DIM
    for i, y in enumerate(heads(proj(c0, n_gkv), n_gkv)):
        gv_ref[0, i, :, :HEAD_DIM] = y.astype(BF16)
        gv_ref[0, i, :, HEAD_DIM:] = ones_col
    c0 += n_gkv * HEAD_DIM

    cd, sda, sdb = cd_ref[...], sda_ref[...], sdb_ref[...]
    for i, y in enumerate(heads(proj(c0, n_diff), n_diff)):
        dq_ref[0, i] = (_rope(y, cd, sda, sdb, 16) * diff_scale).astype(BF16)
    c0 += n_diff * HEAD_DIM
    for i, y in enumerate(heads(proj(c0, n_diff), n_diff)):
        dk_ref[0, i] = _rope(y, cd, sda, sdb, 16).astype(BF16)
    c0 += n_diff * HEAD_DIM
    for i, y in enumerate(heads(proj(c0, n_diff), n_diff)):
        dv_ref[0, i, :, :HEAD_DIM] = y.astype(BF16)
        dv_ref[0, i, :, HEAD_DIM:] = ones_col


def _in_proj(x, g, sh, sc, w_in, qn, kn, tabs_g, tabs_d, dims):
    b, l, d = x.shape
    n_na, n_gq, n_gkv, n_diff = dims
    n_in = w_in.shape[1]
    tm = _pick(l, (256, 128))
    vec = pl.BlockSpec((1, d), lambda bi, i: (0, 0))
    mod = pl.BlockSpec((1, 1, d), lambda bi, i: (bi, 0, 0))
    hvec = pl.BlockSpec((1, HEAD_DIM), lambda bi, i: (0, 0))
    tab = pl.BlockSpec((tm, LANES), lambda bi, i: (i, 0))

    def out(nh, width=HEAD_DIM):
        return (jax.ShapeDtypeStruct((b, nh, l, width), BF16),
                pl.BlockSpec((1, nh, tm, width), lambda bi, i: (bi, 0, i, 0)))

    outs = [out(n_na), out(n_na), out(n_na, 2 * HEAD_DIM), out(n_gq), out(n_gkv),
            out(n_gkv, 2 * HEAD_DIM), out(n_diff), out(n_diff), out(n_diff, 2 * HEAD_DIM)]
    kern = functools.partial(_in_proj_kernel, n_na=n_na, n_gq=n_gq, n_gkv=n_gkv, n_diff=n_diff)
    return pl.pallas_call(
        kern,
        out_shape=[o[0] for o in outs],
        grid=(b, l // tm),
        in_specs=[pl.BlockSpec((1, tm, d), lambda bi, i: (bi, i, 0)), vec, mod, mod,
                  pl.BlockSpec((d, n_in), lambda bi, i: (0, 0), pipeline_mode=pl.Buffered(1)),
                  hvec, hvec, tab, tab, tab, tab, tab, tab],
        out_specs=[o[1] for o in outs],
        compiler_params=_cparams(("parallel", "parallel")),
        name="in_proj",
    )(x, g, sh, sc, w_in, qn, kn, *tabs_g, *tabs_d)


def _attn_kernel(*refs, n_seg, group, tq, chunks, diff, lam_init, n_cast):
    q_ref = refs[0]
    kv_refs = refs[1:1 + 2 * n_seg]
    rest = refs[1 + 2 * n_seg:]
    if n_cast:
        for src_ref, dst_ref in zip(rest[-2 * n_cast - 1:-n_cast - 1], rest[-n_cast:]):
            dst_ref[...] = src_ref[...].astype(BF16)
        rest = rest[:-2 * n_cast - 1] + (rest[-n_cast - 1],)
    if diff:
        lq1_ref, lk1_ref, lq2_ref, lk2_ref, sub_ref, o_ref = rest
        q = q_ref[0, 0].astype(F32)
        lane = lax.broadcasted_iota(jnp.int32, q.shape, 1)
        qq = jnp.concatenate([jnp.where(lane < DIFF_QK_DIM, q, 0.0),
                              jnp.where(lane >= DIFF_QK_DIM, q, 0.0)], axis=0).astype(BF16)
    else:
        (o_ref,) = rest
        qq = q_ref[0].reshape(group * tq, HEAD_DIM)
    m_rows = qq.shape[0]

    def step(k, v, carry):
        m, acc = carry
        s = lax.dot_general(qq, k, NT_DIMS, preferred_element_type=F32)
        m_new = jnp.maximum(m, jnp.max(s, axis=-1, keepdims=True))
        p = jnp.exp2((s - m_new).astype(BF16))
        acc = jnp.exp2(m - m_new) * acc + jnp.dot(p, v, preferred_element_type=F32)
        return m_new, acc

    carry = (jnp.full((m_rows, 1), NEG_BIG, F32), jnp.zeros((m_rows, 2 * HEAD_DIM), F32))
    for si in range(n_seg):
        k_ref, v_ref = kv_refs[2 * si], kv_refs[2 * si + 1]
        n_chunks, tk = chunks[si]
        for c in range(n_chunks):
            rows = slice(c * tk, (c + 1) * tk)
            carry = step(k_ref[0, 0, rows, :], v_ref[0, 0, rows, :], carry)
    _, acc = carry
    o = acc[:, :HEAD_DIM] / acc[:, HEAD_DIM:HEAD_DIM + 1]
    if diff:
        lam = (jnp.exp(jnp.sum(lq1_ref[...] * lk1_ref[...], axis=-1, keepdims=True))
               - jnp.exp(jnp.sum(lq2_ref[...] * lk2_ref[...], axis=-1, keepdims=True)) + lam_init)
        od = o[:tq] - lam * o[tq:]
        o_ref[0] = (_rms(od) * sub_ref[...] * (1.0 - lam_init)).astype(BF16)
    else:
        for g in range(group):
            o_ref[0, :, g * HEAD_DIM:(g + 1) * HEAD_DIM] = o[g * tq:(g + 1) * tq].astype(BF16)


def _cast_rows(a, n_steps):
    rows = a.shape[0] // n_steps
    return rows if rows * n_steps == a.shape[0] and rows % 16 == 0 else 0


def _attention(q, segs, *, group, tq, diff=None, casts=()):
    b, hq, lq, _ = q.shape
    hkv = hq // group
    nq = lq // tq
    kv_args, kv_specs, chunks = [], [], []
    for k, v in segs:
        s = k.shape[2]
        tk = _pick(s, (512, 256, 128))
        chunks.append((s // tk, tk))
        kv_args += [k, v]
        kv_specs += [pl.BlockSpec((1, 1, s, a.shape[3]), lambda bi, h, i: (bi, h, 0, 0))
                     for a in (k, v)]
    extra_args, extra_specs, lam_init = [], [], 0.0
    if diff is not None:
        lq1, lk1, lq2, lk2, subln, lam_init = diff
        extra_args = [lq1, lk1, lq2, lk2, subln]
        extra_specs = [pl.BlockSpec((1, a.shape[1]), lambda bi, h, i: (0, 0)) for a in extra_args]
    cast_specs = [pl.BlockSpec((_cast_rows(a, b * hkv * nq), a.shape[1]),
                               lambda bi, h, i: ((bi * hkv + h) * nq + i, 0)) for a in casts]
    kern = functools.partial(_attn_kernel, n_seg=len(segs), group=group, tq=tq,
                             chunks=tuple(chunks), diff=diff is not None, lam_init=lam_init,
                             n_cast=len(casts))
    outs = pl.pallas_call(
        kern,
        out_shape=[jax.ShapeDtypeStruct((b, lq, hq * HEAD_DIM), BF16)]
                  + [jax.ShapeDtypeStruct(a.shape, BF16) for a in casts],
        grid=(b, hkv, nq),
        in_specs=[pl.BlockSpec((1, group, tq, HEAD_DIM), lambda bi, h, i: (bi, h, i, 0))]
                 + kv_specs + extra_specs + cast_specs,
        out_specs=[pl.BlockSpec((1, tq, group * HEAD_DIM), lambda bi, h, i: (bi, i, h))] + cast_specs,
        compiler_params=_cparams(("parallel", "parallel", "parallel")),
        name="diff_attn" if diff is not None else "gqa_attn",
    )(q, *kv_args, *extra_args, *casts)
    return outs[0] if not casts else outs


def _na_kernel(q_ref, k_ref, v_ref, kc_ref, vc_ref, bias_ref, o_ref, *, rows):
    band = NA_ROWS * GRID_W
    kc = kc_ref[0, 0]
    vc = vc_ref[0, 0]

    def body(r, _):
        rs = jnp.clip(r - NA_ROWS // 2, 0, rows - NA_ROWS)
        off = rs - r + (NA_ROWS - 1)
        qrows = pl.ds(pl.multiple_of(r * GRID_W, GRID_W), GRID_W)
        krows = pl.ds(pl.multiple_of(rs * GRID_W, GRID_W), band)
        q = q_ref[0, 0, qrows, :]
        bias = jnp.concatenate([bias_ref[0, off + 2 * j] for j in range(NA_ROWS // 2)], axis=1)
        s = lax.dot_general(q, k_ref[0, 0, krows, :], NT_DIMS, preferred_element_type=F32) + bias
        sc = lax.dot_general(q, kc, NT_DIMS, preferred_element_type=F32)
        m = jnp.maximum(jnp.max(s, axis=-1, keepdims=True), jnp.max(sc, axis=-1, keepdims=True))
        p = jnp.exp2((s - m).astype(BF16))
        pc = jnp.exp2((sc - m).astype(BF16))
        o = (jnp.dot(p, v_ref[0, 0, krows, :], preferred_element_type=F32)
             + jnp.dot(pc, vc, preferred_element_type=F32))
        o_ref[0, qrows, :] = (o[:, :HEAD_DIM] / o[:, HEAD_DIM:HEAD_DIM + 1]).astype(BF16)
        return 0

    lax.fori_loop(0, rows, body, 0, unroll=8)


def _na_attention(q, k, v, kc, vc, bias):
    b, nh, l, _ = q.shape
    rows = l // GRID_W

    def full(a):
        return pl.BlockSpec((1, 1) + a.shape[2:], lambda bi, h: (bi, h, 0, 0))

    return pl.pallas_call(
        functools.partial(_na_kernel, rows=rows),
        out_shape=jax.ShapeDtypeStruct((b, l, nh * HEAD_DIM), BF16),
        grid=(b, nh),
        in_specs=[full(q), full(k), full(v), full(kc), full(vc),
                  pl.BlockSpec((1,) + bias.shape[1:], lambda bi, h: (h, 0, 0, 0))],
        out_specs=pl.BlockSpec((1, l, HEAD_DIM), lambda bi, h: (bi, 0, h)),
        compiler_params=_cparams(("parallel", "parallel")),
        name="na_attn",
    )(q, k, v, kc, vc, bias)


def _out_proj_kernel(*refs, n_o, routed):
    o_refs = refs[:n_o]
    (w_ref, x_ref, gpost_ref, gt_ref, gpre_ref, sh_ref, sc_ref) = refs[n_o:n_o + 7]
    rest = refs[n_o + 7:]
    if routed:
        router_ref, x1_ref, h2_ref, route_ref = rest
    else:
        x1_ref, h2_ref = rest
    acc = None
    c0 = 0
    for o_ref in o_refs:
        width = o_ref.shape[-1]
        part = jnp.dot(o_ref[0], w_ref[c0:c0 + width, :], preferred_element_type=F32)
        acc = part if acc is None else acc + part
        c0 += width
    x1 = x_ref[0] + gt_ref[0] * (_rms(acc) * gpost_ref[...])
    x1_ref[0] = x1
    h2 = (_rms(x1) * gpre_ref[...]) * (1.0 + sc_ref[0]) + sh_ref[0]
    h2_ref[0] = h2.astype(h2_ref.dtype)
    if routed:
        h_hi = h2.astype(BF16)
        h_lo = (h2 - h_hi.astype(F32)).astype(BF16)
        r = router_ref[...]
        r_hi = r.astype(BF16)
        r_lo = (r - r_hi.astype(F32)).astype(BF16)
        logits = (jnp.dot(h_hi, r_hi, preferred_element_type=F32)
                  + jnp.dot(h_hi, r_lo, preferred_element_type=F32)
                  + jnp.dot(h_lo, r_hi, preferred_element_type=F32))
        lane = lax.broadcasted_iota(jnp.int32, logits.shape, 1)
        logits = jnp.where(lane < N_EXPERTS, logits, NEG_BIG)
        v1 = jnp.max(logits, axis=-1, keepdims=True)
        i1 = jnp.min(jnp.where(logits == v1, lane, LANES), axis=-1, keepdims=True)
        rest_l = jnp.where(lane == i1, NEG_BIG, logits)
        v2 = jnp.max(rest_l, axis=-1, keepdims=True)
        i2 = jnp.min(jnp.where(rest_l == v2, lane, LANES), axis=-1, keepdims=True)
        e2 = jnp.exp(v2 - v1)
        g1 = 1.0 / (1.0 + e2)
        g2 = e2 / (1.0 + e2)
        route_ref[0] = jnp.where(lane == 0, i1.astype(F32),
                       jnp.where(lane == 1, i2.astype(F32),
                       jnp.where(lane == 2, g1, jnp.where(lane == 3, g2, 0.0))))


def _out_proj(o_list, w_out, x, gpost, gt, gpre, sh, sc, router=None):
    b, l, d = x.shape
    tm = _pick(l, (256, 128))
    routed = router is not None
    vec = pl.BlockSpec((1, d), lambda bi, i: (0, 0))
    mod = pl.BlockSpec((1, 1, d), lambda bi, i: (bi, 0, 0))
    row = pl.BlockSpec((1, tm, d), lambda bi, i: (bi, i, 0))
    in_specs = [pl.BlockSpec((1, tm, o.shape[-1]), lambda bi, i: (bi, i, 0)) for o in o_list]
    in_specs += [pl.BlockSpec(w_out.shape, lambda bi, i: (0, 0), pipeline_mode=pl.Buffered(1)),
                 row, vec, mod, vec, mod, mod]
    args = list(o_list) + [w_out, x, gpost, gt, gpre, sh, sc]
    out_shape = [jax.ShapeDtypeStruct((b, l, d), F32),
                 jax.ShapeDtypeStruct((b, l, d), F32 if routed else BF16)]
    out_specs = [row, row]
    if routed:
        in_specs.append(pl.BlockSpec(router.shape, lambda bi, i: (0, 0)))
        args.append(router)
        out_shape.append(jax.ShapeDtypeStruct((b, l, LANES), F32))
        out_specs.append(pl.BlockSpec((1, tm, LANES), lambda bi, i: (bi, i, 0)))
    return pl.pallas_call(
        functools.partial(_out_proj_kernel, n_o=len(o_list), routed=routed),
        out_shape=out_shape,
        grid=(b, l // tm),
        in_specs=in_specs,
        out_specs=out_specs,
        compiler_params=_cparams(("parallel", "parallel")),
        name="out_proj",
    )(*args)


def _swiglu_step(hb, w1, w3, w2):
    a = jnp.dot(hb, w1, preferred_element_type=F32)
    g = jnp.dot(hb, w3, preferred_element_type=F32)
    act = (a / (1.0 + jnp.exp(-a))) * g
    return jnp.dot(act.astype(BF16), w2, preferred_element_type=F32)


def _ffn_dense_kernel(h_ref, w1_ref, w3_ref, w2_ref, x_ref, gpost_ref, gt_ref, o_ref, acc_ref):
    f = pl.program_id(2)

    @pl.when(f == 0)
    def _():
        acc_ref[...] = jnp.zeros_like(acc_ref)

    acc_ref[...] += _swiglu_step(h_ref[0], w1_ref[...], w3_ref[...], w2_ref[...])

    @pl.when(f == pl.num_programs(2) - 1)
    def _():
        o_ref[0] = x_ref[0] + gt_ref[0] * (_rms(acc_ref[...]) * gpost_ref[...])


def _ffn_dense(h2, w1, w3, w2, x1, gpost, gt):
    b, l, d = x1.shape
    dff = w1.shape[1]
    tm = _pick(l, (512, 256, 128))
    tf = _pick(dff, (512, 256, 128))
    row = pl.BlockSpec((1, tm, d), lambda bi, i, f: (bi, i, 0))
    return pl.pallas_call(
        _ffn_dense_kernel,
        out_shape=jax.ShapeDtypeStruct((b, l, d), F32),
        grid=(b, l // tm, dff // tf),
        in_specs=[row,
                  pl.BlockSpec((d, tf), lambda bi, i, f: (0, f)),
                  pl.BlockSpec((d, tf), lambda bi, i, f: (0, f)),
                  pl.BlockSpec((tf, d), lambda bi, i, f: (f, 0)),
                  row,
                  pl.BlockSpec((1, d), lambda bi, i, f: (0, 0)),
                  pl.BlockSpec((1, 1, d), lambda bi, i, f: (bi, 0, 0))],
        out_specs=row,
        scratch_shapes=[pltpu.VMEM((tm, d), F32)],
        compiler_params=_cparams(("parallel", "parallel", "arbitrary")),
        name="ffn_dense",
    )(h2, w1, w3, w2, x1, gpost, gt)


def _ffn_expert_kernel(tile_e_ref, nact_ref, src0_ref, srcn_ref, h_hbm, w1_ref, w3_ref, w2_ref,
                       o_ref, xbuf_ref, hb_ref, acc_ref, sem, *, tm):
    j = pl.program_id(0)
    f = pl.program_id(1)
    nact = nact_ref[0]
    slot = j % 2

    def gather(idx_ref, dst_slot):
        def issue(i, _):
            pltpu.make_async_copy(h_hbm.at[pl.ds(idx_ref[i], 1)],
                                  xbuf_ref.at[dst_slot, pl.ds(i, 1)], sem.at[dst_slot]).start()
            return 0
        lax.fori_loop(0, tm, issue, 0, unroll=8)

    @pl.when(j < nact)
    def _():
        @pl.when(f == 0)
        def _():
            @pl.when(j == 0)
            def _():
                gather(src0_ref, 0)

            pltpu.make_async_copy(h_hbm.at[pl.ds(0, tm)], xbuf_ref.at[slot], sem.at[slot]).wait()
            hb_ref[...] = xbuf_ref[slot].astype(BF16)
            acc_ref[...] = jnp.zeros_like(acc_ref)

        @pl.when((f == 1) & (j + 1 < nact))
        def _():
            gather(srcn_ref, 1 - slot)

        acc_ref[...] += _swiglu_step(hb_ref[...], w1_ref[0], w3_ref[0], w2_ref[0])

        @pl.when(f == pl.num_programs(1) - 1)
        def _():
            o_ref[...] = acc_ref[...]

    @pl.when((j >= nact_ref[0]) & (f == pl.num_programs(1) - 1))
    def _():
        o_ref[...] = jnp.zeros_like(o_ref)


def _ffn_expert(h2, src, w1, w3, w2, tile_e, nact, tm):
    d = h2.shape[1]
    p = src.shape[0]
    n_tiles = p // tm
    dff = w1.shape[2]
    tf = _pick(dff, (512, 256, 128))
    nf = dff // tf
    assert nf >= 2

    def fsel(j, f, na):
        return jnp.where(j < na[0], f, nf - 1)

    return pl.pallas_call(
        functools.partial(_ffn_expert_kernel, tm=tm),
        out_shape=jax.ShapeDtypeStruct((p, d), F32),
        grid_spec=pltpu.PrefetchScalarGridSpec(
            num_scalar_prefetch=2,
            grid=(n_tiles, nf),
            in_specs=[pl.BlockSpec((tm,), lambda j, f, te, na: (0,), memory_space=pltpu.SMEM),
                      pl.BlockSpec((tm,), lambda j, f, te, na: (jnp.minimum(j + 1, n_tiles - 1),),
                                   memory_space=pltpu.SMEM),
                      pl.BlockSpec(memory_space=pl.ANY),
                      pl.BlockSpec((1, d, tf), lambda j, f, te, na: (te[j], 0, fsel(j, f, na))),
                      pl.BlockSpec((1, d, tf), lambda j, f, te, na: (te[j], 0, fsel(j, f, na))),
                      pl.BlockSpec((1, tf, d), lambda j, f, te, na: (te[j], fsel(j, f, na), 0))],
            out_specs=pl.BlockSpec((tm, d), lambda j, f, te, na: (j, 0)),
            scratch_shapes=[pltpu.VMEM((2, tm, d), F32), pltpu.VMEM((tm, d), BF16),
                            pltpu.VMEM((tm, d), F32), pltpu.SemaphoreType.DMA((2,))]),
        compiler_params=_cparams(("arbitrary", "arbitrary")),
        name="ffn_expert",
    )(tile_e, nact, src, src, h2, w1, w3, w2)


def _rank_kernel(route_ref, rank_ref, cnt_ref, carry_ref):
    i = pl.program_id(0)

    @pl.when(i == 0)
    def _():
        carry_ref[...] = jnp.zeros_like(carry_ref)

    r = route_ref[...]
    t = r.shape[0]
    lane = lax.broadcasted_iota(jnp.int32, r.shape, 1)
    e1 = r[:, 0:1].astype(jnp.int32)
    e2 = r[:, 1:2].astype(jnp.int32)
    oh1 = (lane == e1).astype(F32)
    oh2 = (lane == e2).astype(F32)
    c = (oh1 + oh2).astype(BF16)
    row = lax.broadcasted_iota(jnp.int32, (t, t), 0)
    col = lax.broadcasted_iota(jnp.int32, (t, t), 1)
    tri = (col < row).astype(F32).astype(BF16)
    prefix = jnp.dot(tri, c, preferred_element_type=F32) + carry_ref[0:1, :]
    r1 = jnp.sum(prefix * oh1, axis=-1, keepdims=True)
    r2 = jnp.sum(prefix * oh2, axis=-1, keepdims=True)
    rank_ref[...] = jnp.where(lane == 0, r1, jnp.where(lane == 1, r2, 0.0))
    total = carry_ref[0:1, :] + jnp.sum(oh1 + oh2, axis=0, keepdims=True)
    carry_ref[...] = jnp.broadcast_to(total, carry_ref.shape)
    cnt_ref[...] = jnp.broadcast_to(total, cnt_ref.shape)


def _expert_ranks(route):
    n = route.shape[0]
    t = _pick(n, (512, 256, 128))
    return pl.pallas_call(
        _rank_kernel,
        out_shape=[jax.ShapeDtypeStruct((n, LANES), F32), jax.ShapeDtypeStruct((8, LANES), F32)],
        grid=(n // t,),
        in_specs=[pl.BlockSpec((t, LANES), lambda i: (i, 0))],
        out_specs=[pl.BlockSpec((t, LANES), lambda i: (i, 0)),
                   pl.BlockSpec((8, LANES), lambda i: (0, 0))],
        scratch_shapes=[pltpu.VMEM((8, LANES), F32)],
        compiler_params=_cparams(("arbitrary",)),
        name="expert_ranks",
    )(route)


def _combine_kernel(pos0_ref, posn_ref, y_hbm, route_ref, x_ref, gpost_ref, gt_ref, o_ref,
                    buf_ref, sem, *, tt):
    n_steps = pl.num_programs(0) * pl.num_programs(1)
    step = pl.program_id(0) * pl.num_programs(1) + pl.program_id(1)
    slot = step % 2

    def gather(p_ref, dst_slot):
        def issue(i, _):
            for k in range(TOP_K):
                pltpu.make_async_copy(y_hbm.at[pl.ds(p_ref[TOP_K * i + k], 1)],
                                      buf_ref.at[dst_slot, pl.ds(k * tt + i, 1)],
                                      sem.at[dst_slot]).start()
            return 0
        lax.fori_loop(0, tt, issue, 0, unroll=4)

    @pl.when(step == 0)
    def _():
        gather(pos0_ref, 0)

    @pl.when(step + 1 < n_steps)
    def _():
        gather(posn_ref, 1 - slot)

    pltpu.make_async_copy(y_hbm.at[pl.ds(0, TOP_K * tt)], buf_ref.at[slot], sem.at[slot]).wait()
    r = route_ref[0]
    y = r[:, 2:3] * buf_ref[slot, 0:tt, :] + r[:, 3:4] * buf_ref[slot, tt:2 * tt, :]
    o_ref[0] = x_ref[0] + gt_ref[0] * (_rms(y) * gpost_ref[...])


def _combine(y, pos_flat, route, x1, gpost, gt):
    b, l, d = x1.shape
    tt = _pick(l, (256, 128))
    nt = l // tt
    n_steps = b * nt
    row = pl.BlockSpec((1, tt, d), lambda bi, i: (bi, i, 0))
    return pl.pallas_call(
        functools.partial(_combine_kernel, tt=tt),
        out_shape=jax.ShapeDtypeStruct((b, l, d), F32),
        grid=(b, nt),
        in_specs=[pl.BlockSpec((TOP_K * tt,), lambda bi, i: (0,), memory_space=pltpu.SMEM),
                  pl.BlockSpec((TOP_K * tt,),
                               lambda bi, i: (jnp.minimum(bi * nt + i + 1, n_steps - 1),),
                               memory_space=pltpu.SMEM),
                  pl.BlockSpec(memory_space=pl.ANY),
                  pl.BlockSpec((1, tt, LANES), lambda bi, i: (bi, i, 0)),
                  row,
                  pl.BlockSpec((1, d), lambda bi, i: (0, 0)),
                  pl.BlockSpec((1, 1, d), lambda bi, i: (bi, 0, 0))],
        out_specs=row,
        scratch_shapes=[pltpu.VMEM((2, TOP_K * tt, d), F32), pltpu.SemaphoreType.DMA((2,))],
        compiler_params=_cparams(("arbitrary", "arbitrary")),
        name="moe_combine",
    )(pos_flat, pos_flat, y, route, x1, gpost, gt)


def _moe_ffn(h2, route, w1, w3, w2, x1, gpost, gt):
    b, l, d = x1.shape
    n = b * l
    tm = 512
    route2 = route.reshape(n, LANES)
    rank, counts = _expert_ranks(route2)
    cnt = counts[0, :N_EXPERTS].astype(jnp.int32)
    tiles = (cnt + tm - 1) // tm
    tile_end = jnp.cumsum(tiles)
    row_off = (tile_end - tiles) * tm
    e_idx = route2[:, :TOP_K].astype(jnp.int32)
    pos = (row_off[e_idx] + rank[:, :TOP_K].astype(jnp.int32)).reshape(-1)
    n_tiles = (n * TOP_K) // tm + N_EXPERTS
    nact = tile_end[-1:]
    tile_ids = jnp.minimum(jnp.arange(n_tiles, dtype=jnp.int32), nact[0] - 1)
    tile_e = jnp.sum((tile_ids[:, None] >= tile_end[None, :]).astype(jnp.int32), axis=1)
    token = jnp.arange(n * TOP_K, dtype=jnp.int32) // TOP_K
    src = jnp.zeros((n_tiles * tm,), jnp.int32).at[pos].set(token)
    y = _ffn_expert(h2.reshape(n, d), src, w1, w3, w2, tile_e, nact.astype(jnp.int32), tm)
    return _combine(y, pos, route, x1, gpost, gt)


def _rope_tables(l, dims_per_map):
    lane = jnp.arange(LANES)
    d = lane % dims_per_map
    part_w = dims_per_map // 2
    half = part_w // 2
    is_col = (d // part_w) == 1
    first = (d % part_w) < half
    inv_freq = ROPE_THETA ** (-jnp.arange(half, dtype=F32) / half)
    freq = inv_freq[d % half]
    pos = jnp.arange(l)
    p = jnp.where(is_col[None, :], (pos % GRID_W)[:, None], (pos // GRID_W)[:, None]).astype(F32)
    ang = p * freq[None, :]
    cos, sin = jnp.cos(ang), jnp.sin(ang)
    zero = jnp.zeros_like(sin)
    return cos, jnp.where(first[None, :], -sin, zero), jnp.where(first[None, :], zero, sin)


def _identity_tables(l):
    return jnp.ones((l, LANES), F32), jnp.zeros((l, LANES), F32), jnp.zeros((l, LANES), F32)


def kernel(x, c, ctx, c_ctx, w_mod, b_mod, g_mix_pre, g_mix_post, g_ffn_pre, g_ffn_post,
           w_in, w_out, na_rpb, gqa_q_norm, gqa_k_norm, diff_lambda_q1, diff_lambda_k1,
           diff_lambda_q2, diff_lambda_k2, diff_subln, ffn_w1, ffn_w3, ffn_w2,
           moe_router, moe_w1, moe_w3, moe_w2):
    b, l, d = x.shape
    cn = ctx.shape[1]
    depth = w_mod.shape[0]
    n_heads = d // HEAD_DIM
    n_na, n_gq, n_diff = n_heads // 4, n_heads // 2, n_heads // 4
    n_gkv = n_gq // 4
    group = n_gq // n_gkv
    dims = (n_na, n_gq, n_gkv, n_diff)

    rows_pad = -(-(b + 1) // 8) * 8
    cvec = jnp.zeros((rows_pad, d), F32).at[:b].set(c).at[b].set(c_ctx)
    mods = _modulation(cvec, w_mod, b_mod)
    bias_tab = _na_bias_table(na_rpb.reshape((depth * n_na,) + na_rpb.shape[2:]))
    bias_tab = bias_tab.reshape((depth, n_na) + bias_tab.shape[1:])

    tabs_g = _rope_tables(l, HEAD_DIM)
    tabs_d = _rope_tables(l, DIFF_QK_DIM)
    tabs_id = _identity_tables(cn)

    x_ctx = ctx
    moe_bf16 = {}
    for li in range(depth):
        last = li == depth - 1
        lam_init = 0.8 - 0.6 * math.exp(-0.3 * li)
        m6 = mods[li].reshape(rows_pad, 6, d)
        sh1, sc1, gt1, sh2, sc2, gt2 = [m6[:b, k][:, None, :] for k in range(6)]
        csh1, csc1, cgt1, csh2, csc2, cgt2 = [
            jnp.broadcast_to(m6[b, k][None, None, :], (b, 1, d)) for k in range(6)]
        vec = lambda a: a[li].reshape(1, -1)
        w_in_b = w_in[li].astype(BF16)
        w_out_b = w_out[li].astype(BF16)
        qn, kn = vec(gqa_q_norm), vec(gqa_k_norm)
        diff_args = (vec(diff_lambda_q1), vec(diff_lambda_k1), vec(diff_lambda_q2),
                     vec(diff_lambda_k2), vec(diff_subln), lam_init)

        (naq, nak, nav, gq, gk, gv, dq, dk, dv) = _in_proj(
            x, vec(g_mix_pre), sh1, sc1, w_in_b, qn, kn, tabs_g, tabs_d, dims)
        (naqc, nakc, navc, gqc, gkc, gvc, dqc, dkc, dvc) = _in_proj(
            x_ctx, vec(g_mix_pre), csh1, csc1, w_in_b, qn, kn, tabs_id, tabs_id, dims)

        oa = _na_attention(naq, nak, nav, nakc, navc, bias_tab[li])
        tq_g, tq_d = _pick(l, (256, 128)), _pick(l, (256, 128))
        casts_g, casts_d = [], []
        if (li + 1) < depth and (li + 1) % 2 == 1:
            e_, d_, f_ = moe_w1.shape[1:]
            flat = [moe_w1[(li + 1) // 2].reshape(e_ * d_, f_), moe_w3[(li + 1) // 2].reshape(e_ * d_, f_),
                    moe_w2[(li + 1) // 2].reshape(e_ * f_, d_)]
            steps_g, steps_d = b * n_gkv * (l // tq_g), b * n_diff * (l // tq_d)
            if _cast_rows(flat[0], steps_g) and _cast_rows(flat[1], steps_d) and _cast_rows(flat[2], steps_d):
                casts_g, casts_d = flat[:1], flat[1:]
        ob = _attention(gq, [(gk, gv), (gkc, gvc)], group=group, tq=tq_g, casts=casts_g)
        oc = _attention(dq, [(dk, dv), (dkc, dvc)], group=1, tq=tq_d, diff=diff_args, casts=casts_d)
        if casts_g:
            ob, w1b = ob
            oc, w3b, w2b = oc
            moe_bf16[li + 1] = (w1b.reshape(e_, d_, f_), w3b.reshape(e_, d_, f_), w2b.reshape(e_, f_, d_))

        routed = li % 2 == 1
        if routed:
            router = jnp.zeros((d, LANES), F32).at[:, :N_EXPERTS].set(moe_router[li // 2])
            x1, h2, route = _out_proj([oa, ob, oc], w_out_b, x, vec(g_mix_post), gt1,
                                      vec(g_ffn_pre), sh2, sc2, router=router)
        else:
            x1, h2 = _out_proj([oa, ob, oc], w_out_b, x, vec(g_mix_post), gt1,
                               vec(g_ffn_pre), sh2, sc2)

        if not last:
            tqc = _pick(cn, (256, 128))
            oac = _attention(naqc, [(nakc, navc)], group=1, tq=tqc)
            obc = _attention(gqc, [(gkc, gvc)], group=group, tq=_pick(cn, (128,)))
            occ = _attention(dqc, [(dkc, dvc)], group=1, tq=tqc, diff=diff_args)
            if routed:
                router = jnp.zeros((d, LANES), F32).at[:, :N_EXPERTS].set(moe_router[li // 2])
                xc1, h2c, route_c = _out_proj([oac, obc, occ], w_out_b, x_ctx, vec(g_mix_post),
                                              cgt1, vec(g_ffn_pre), csh2, csc2, router=router)
            else:
                xc1, h2c = _out_proj([oac, obc, occ], w_out_b, x_ctx, vec(g_mix_post), cgt1,
                                     vec(g_ffn_pre), csh2, csc2)

        if routed:
            if li in moe_bf16:
                w1, w3, w2 = moe_bf16.pop(li)
            else:
                w1, w3, w2 = [w[li // 2].astype(BF16) for w in (moe_w1, moe_w3, moe_w2)]
            x = _moe_ffn(h2, route, w1, w3, w2, x1, vec(g_ffn_post), gt2)
            if not last:
                x_ctx = _moe_ffn(h2c, route_c, w1, w3, w2, xc1, vec(g_ffn_post), cgt2)
        else:
            w1 = ffn_w1[li // 2].astype(BF16)
            w3 = ffn_w3[li // 2].astype(BF16)
            w2 = ffn_w2[li // 2].astype(BF16)
            x = _ffn_dense(h2, w1, w3, w2, x1, vec(g_ffn_post), gt2)
            if not last:
                x_ctx = _ffn_dense(h2c, w1, w3, w2, xc1, vec(g_ffn_post), cgt2)
    return x
```

```python
import functools
import math

import jax
import jax.numpy as jnp
from jax import lax
from jax.experimental import pallas as pl
from jax.experimental.pallas import tpu as pltpu

GRID_W = 64
HEAD_DIM = 128
DIFF_QK_DIM = HEAD_DIM // 2
NA_ROWS = 8
NA_COLS = 16
ROPE_THETA = 10000.0
N_EXPERTS = 8
TOP_K = 2
EPS = 1e-6

LANES = 128
NEG_BIG = -1e30
LOG2E = math.log2(math.e)
VMEM_LIMIT = 56 * 1024 * 1024

BF16 = jnp.bfloat16
F32 = jnp.float32
NT_DIMS = (((1,), (1,)), ((), ()))


def _cparams(sem):
    return pltpu.CompilerParams(dimension_semantics=sem, vmem_limit_bytes=VMEM_LIMIT)


def _pick(n, cands):
    for c in cands:
        if n % c == 0:
            return c
    raise ValueError(f"no tile in {cands} divides {n}")


def _rms(y):
    return y * lax.rsqrt(jnp.mean(y * y, axis=-1, keepdims=True) + EPS)


def _mod_kernel(c_ref, w_ref, b_ref, o_ref):
    c = c_ref[...]
    s = c / (1.0 + jnp.exp(-c))
    w = w_ref[0]
    s_hi = s.astype(BF16)
    s_lo = (s - s_hi.astype(F32)).astype(BF16)
    w_hi = w.astype(BF16)
    w_lo = (w - w_hi.astype(F32)).astype(BF16)
    o_ref[0] = (jnp.dot(s_hi, w_hi, preferred_element_type=F32)
                + jnp.dot(s_hi, w_lo, preferred_element_type=F32)
                + jnp.dot(s_lo, w_hi, preferred_element_type=F32)) + b_ref[0]


def _modulation(cvec, w_mod, b_mod):
    depth, d, n = w_mod.shape
    r = cvec.shape[0]
    tn = _pick(n, (1024, 512, 256, 128))
    return pl.pallas_call(
        _mod_kernel,
        out_shape=jax.ShapeDtypeStruct((depth, r, n), F32),
        grid=(depth, n // tn),
        in_specs=[pl.BlockSpec((r, d), lambda l, j: (0, 0)),
                  pl.BlockSpec((1, d, tn), lambda l, j: (l, 0, j)),
                  pl.BlockSpec((1, 1, tn), lambda l, j: (l, 0, j))],
        out_specs=pl.BlockSpec((1, r, tn), lambda l, j: (l, 0, j)),
        compiler_params=_cparams(("parallel", "parallel")),
        name="adaln_mod",
    )(cvec, w_mod, b_mod.reshape(depth, 1, n))


def _na_bias_kernel(rpb_ref, o_ref):
    n_dc = 2 * NA_COLS - 1
    base = pl.program_id(0) * ((2 * NA_ROWS - 1) * n_dc)
    shape = (GRID_W, LANES)
    qc = lax.broadcasted_iota(jnp.int32, shape, 0)
    lane = lax.broadcasted_iota(jnp.int32, shape, 1)
    half = lane >> 6
    kc = lane & (GRID_W - 1)
    dc = jnp.clip(kc - qc, -(NA_COLS - 1), NA_COLS - 1) + (NA_COLS - 1)
    cs = jnp.clip(qc - NA_COLS // 2, 0, GRID_W - NA_COLS)
    ok = (kc >= cs) & (kc < cs + NA_COLS)
    code = jnp.where(ok, half * 32 + dc, -1)
    for d in range(2 * NA_ROWS - 2):
        acc = jnp.full(shape, NEG_BIG, F32)
        for hf in range(2):
            for j in range(n_dc):
                val = rpb_ref[base + (d + hf) * n_dc + j] * LOG2E
                acc = jnp.where(code == hf * 32 + j, val, acc)
        o_ref[0, d] = acc


def _na_bias_table(rpb):
    nh = rpb.shape[0]
    return pl.pallas_call(
        _na_bias_kernel,
        out_shape=jax.ShapeDtypeStruct((nh, 2 * NA_ROWS - 2, GRID_W, LANES), F32),
        grid=(nh,),
        in_specs=[pl.BlockSpec(memory_space=pltpu.SMEM)],
        out_specs=pl.BlockSpec((1, 2 * NA_ROWS - 2, GRID_W, LANES), lambda h: (h, 0, 0, 0)),
        compiler_params=_cparams(("parallel",)),
        name="na_bias_table",
    )(rpb.reshape(-1))


def _rope(y, cos, sin_hi, sin_lo, dist):
    return (y * cos + pltpu.roll(y, LANES - dist, 1) * sin_hi
            + pltpu.roll(y, dist, 1) * sin_lo)


def _in_proj_kernel(x_ref, g_ref, sh_ref, sc_ref, w_ref, qn_ref, kn_ref,
                    cg_ref, sga_ref, sgb_ref, cd_ref, sda_ref, sdb_ref,
                    naq_ref, nak_ref, nav_ref, gq_ref, gk_ref, gv_ref,
                    dq_ref, dk_ref, dv_ref, *, n_na, n_gq, n_gkv, n_diff):
    x = x_ref[0]
    h = _rms(x) * g_ref[...]
    h = h * (1.0 + sc_ref[0]) + sh_ref[0]
    hb = h.astype(BF16)

    def proj(c0, nheads):
        return jnp.dot(hb, w_ref[:, c0:c0 + nheads * HEAD_DIM], preferred_element_type=F32)

    def heads(y, nheads):
        return [y[:, i * HEAD_DIM:(i + 1) * HEAD_DIM] for i in range(nheads)]

    na_scale = HEAD_DIM ** -0.5 * LOG2E
    diff_scale = DIFF_QK_DIM ** -0.5 * LOG2E
    c0 = 0
    for i, y in enumerate(heads(proj(c0, n_na), n_na)):
        naq_ref[0, i] = (y * na_scale).astype(BF16)
    c0 += n_na * HEAD_DIM
    for i, y in enumerate(heads(proj(c0, n_na), n_na)):
        nak_ref[0, i] = y.astype(BF16)
    c0 += n_na * HEAD_DIM
    lane = lax.broadcasted_iota(jnp.int32, (x.shape[0], LANES), 1)
    ones_col = jnp.where(lane == 0, 1.0, 0.0).astype(BF16)
    for i, y in enumerate(heads(proj(c0, n_na), n_na)):
        nav_ref[0, i, :, :HEAD_DIM] = y.astype(BF16)
        nav_ref[0, i, :, HEAD_DIM:] = ones_col
    c0 += n_na * HEAD_DIM

    cg, sga, sgb = cg_ref[...], sga_ref[...], sgb_ref[...]
    for i, y in enumerate(heads(proj(c0, n_gq), n_gq)):
        y = _rope(_rms(y) * qn_ref[...], cg, sga, sgb, 32)
        gq_ref[0, i] = (y * na_scale).astype(BF16)
    c0 += n_gq * HEAD_DIM
    for i, y in enumerate(heads(proj(c0, n_gkv), n_gkv)):
        y = _rope(_rms(y) * kn_ref[...], cg, sga, sgb, 32)
        gk_ref[0, i] = y.astype(BF16)
    c0 += n_gkv * HEAD_DIM
    for i, y in enumerate(heads(proj(c0, n_gkv), n_gkv)):
        gv_ref[0, i, :, :HEAD_DIM] = y.astype(BF16)
        gv_ref[0, i, :, HEAD_DIM:] = ones_col
    c0 += n_gkv * HEAD_DIM

    cd, sda, sdb = cd_ref[...], sda_ref[...], sdb_ref[...]
    for i, y in enumerate(heads(proj(c0, n_diff), n_diff)):
        dq_ref[0, i] = (_rope(y, cd, sda, sdb, 16) * diff_scale).astype(BF16)
    c0 += n_diff * HEAD_DIM
    for i, y in enumerate(heads(proj(c0, n_diff), n_diff)):
        dk_ref[0, i] = _rope(y, cd, sda, sdb, 16).astype(BF16)
    c0 += n_diff * HEAD_DIM
    for i, y in enumerate(heads(proj(c0, n_diff), n_diff)):
        dv_ref[0, i, :, :HEAD_DIM] = y.astype(BF16)
        dv_ref[0, i, :, HEAD_DIM:] = ones_col


def _in_proj(x, g, sh, sc, w_in, qn, kn, tabs_g, tabs_d, dims):
    b, l, d = x.shape
    n_na, n_gq, n_gkv, n_diff = dims
    n_in = w_in.shape[1]
    tm = _pick(l, (256, 128))
    vec = pl.BlockSpec((1, d), lambda bi, i: (0, 0))
    mod = pl.BlockSpec((1, 1, d), lambda bi, i: (bi, 0, 0))
    hvec = pl.BlockSpec((1, HEAD_DIM), lambda bi, i: (0, 0))
    tab = pl.BlockSpec((tm, LANES), lambda bi, i: (i, 0))

    def out(nh, width=HEAD_DIM):
        return (jax.ShapeDtypeStruct((b, nh, l, width), BF16),
                pl.BlockSpec((1, nh, tm, width), lambda bi, i: (bi, 0, i, 0)))

    outs = [out(n_na), out(n_na), out(n_na, 2 * HEAD_DIM), out(n_gq), out(n_gkv),
            out(n_gkv, 2 * HEAD_DIM), out(n_diff), out(n_diff), out(n_diff, 2 * HEAD_DIM)]
    kern = functools.partial(_in_proj_kernel, n_na=n_na, n_gq=n_gq, n_gkv=n_gkv, n_diff=n_diff)
    return pl.pallas_call(
        kern,
        out_shape=[o[0] for o in outs],
        grid=(b, l // tm),
        in_specs=[pl.BlockSpec((1, tm, d), lambda bi, i: (bi, i, 0)), vec, mod, mod,
                  pl.BlockSpec((d, n_in), lambda bi, i: (0, 0), pipeline_mode=pl.Buffered(1)),
                  hvec, hvec, tab, tab, tab, tab, tab, tab],
        out_specs=[o[1] for o in outs],
        compiler_params=_cparams(("parallel", "parallel")),
        name="in_proj",
    )(x, g, sh, sc, w_in, qn, kn, *tabs_g, *tabs_d)


def _attn_kernel(*refs, n_seg, group, tq, chunks, diff, lam_init, n_cast):
    q_ref = refs[0]
    kv_refs = refs[1:1 + 2 * n_seg]
    rest = refs[1 + 2 * n_seg:]
    if n_cast:
        for src_ref, dst_ref in zip(rest[-2 * n_cast - 1:-n_cast - 1], rest[-n_cast:]):
            if len(dst_ref.shape) == 2:
                dst_ref[...] = src_ref[...].astype(BF16)
            else:
                tf = dst_ref.shape[3]
                for c in range(dst_ref.shape[1]):
                    dst_ref[0, c] = src_ref[:, c * tf:(c + 1) * tf].astype(BF16)
        rest = rest[:-2 * n_cast - 1] + (rest[-n_cast - 1],)
    if diff:
        lq1_ref, lk1_ref, lq2_ref, lk2_ref, sub_ref, o_ref = rest
        q = q_ref[0, 0].astype(F32)
        lane = lax.broadcasted_iota(jnp.int32, q.shape, 1)
        qq = jnp.concatenate([jnp.where(lane < DIFF_QK_DIM, q, 0.0),
                              jnp.where(lane >= DIFF_QK_DIM, q, 0.0)], axis=0).astype(BF16)
    else:
        (o_ref,) = rest
        qq = q_ref[0].reshape(group * tq, HEAD_DIM)
    m_rows = qq.shape[0]

    def step(k, v, carry):
        m, acc = carry
        s = lax.dot_general(qq, k, NT_DIMS, preferred_element_type=F32)
        m_new = jnp.maximum(m, jnp.max(s, axis=-1, keepdims=True))
        p = jnp.exp2((s - m_new).astype(BF16))
        acc = jnp.exp2(m - m_new) * acc + jnp.dot(p, v, preferred_element_type=F32)
        return m_new, acc

    carry = (jnp.full((m_rows, 1), NEG_BIG, F32), jnp.zeros((m_rows, 2 * HEAD_DIM), F32))
    for si in range(n_seg):
        k_ref, v_ref = kv_refs[2 * si], kv_refs[2 * si + 1]
        n_chunks, tk = chunks[si]
        for c in range(n_chunks):
            rows = slice(c * tk, (c + 1) * tk)
            carry = step(k_ref[0, 0, rows, :], v_ref[0, 0, rows, :], carry)
    _, acc = carry
    o = acc[:, :HEAD_DIM] / acc[:, HEAD_DIM:HEAD_DIM + 1]
    if diff:
        lam = (jnp.exp(jnp.sum(lq1_ref[...] * lk1_ref[...], axis=-1, keepdims=True))
               - jnp.exp(jnp.sum(lq2_ref[...] * lk2_ref[...], axis=-1, keepdims=True)) + lam_init)
        od = o[:tq] - lam * o[tq:]
        o_ref[0] = (_rms(od) * sub_ref[...] * (1.0 - lam_init)).astype(BF16)
    else:
        for g in range(group):
            o_ref[0, :, g * HEAD_DIM:(g + 1) * HEAD_DIM] = o[g * tq:(g + 1) * tq].astype(BF16)


def _cast_rows(a, n_steps):
    rows = a.shape[0] // n_steps
    return rows if rows * n_steps == a.shape[0] and rows % 16 == 0 else 0


def _attention(q, segs, *, group, tq, diff=None, casts=()):
    b, hq, lq, _ = q.shape
    hkv = hq // group
    nq = lq // tq
    kv_args, kv_specs, chunks = [], [], []
    for k, v in segs:
        s = k.shape[2]
        tk = _pick(s, (512, 256, 128))
        chunks.append((s // tk, tk))
        kv_args += [k, v]
        kv_specs += [pl.BlockSpec((1, 1, s, a.shape[3]), lambda bi, h, i: (bi, h, 0, 0))
                     for a in (k, v)]
    extra_args, extra_specs, lam_init = [], [], 0.0
    if diff is not None:
        lq1, lk1, lq2, lk2, subln, lam_init = diff
        extra_args = [lq1, lk1, lq2, lk2, subln]
        extra_specs = [pl.BlockSpec((1, a.shape[1]), lambda bi, h, i: (0, 0)) for a in extra_args]
    n_steps = b * hkv * nq
    cast_args, cast_in_specs, cast_out_specs, cast_out_shapes = [], [], [], []
    for a in casts:
        flat = a.reshape(-1, a.shape[-1])
        rows = _cast_rows(flat, n_steps)
        spec2d = pl.BlockSpec((rows, flat.shape[1]), lambda bi, h, i: ((bi * hkv + h) * nq + i, 0))
        cast_args.append(flat)
        cast_in_specs.append(spec2d)
        if a.ndim == 2:
            cast_out_specs.append(spec2d)
            cast_out_shapes.append(jax.ShapeDtypeStruct(a.shape, BF16))
        else:
            e, d, dff = a.shape
            tf = _ff_tile(dff)
            per_e = d // rows
            assert per_e * rows == d
            cast_out_specs.append(pl.BlockSpec(
                (1, dff // tf, rows, tf),
                lambda bi, h, i: (((bi * hkv + h) * nq + i) // per_e, 0, ((bi * hkv + h) * nq + i) % per_e, 0)))
            cast_out_shapes.append(jax.ShapeDtypeStruct((e, dff // tf, d, tf), BF16))
    kern = functools.partial(_attn_kernel, n_seg=len(segs), group=group, tq=tq,
                             chunks=tuple(chunks), diff=diff is not None, lam_init=lam_init,
                             n_cast=len(casts))
    outs = pl.pallas_call(
        kern,
        out_shape=[jax.ShapeDtypeStruct((b, lq, hq * HEAD_DIM), BF16)] + cast_out_shapes,
        grid=(b, hkv, nq),
        in_specs=[pl.BlockSpec((1, group, tq, HEAD_DIM), lambda bi, h, i: (bi, h, i, 0))]
                 + kv_specs + extra_specs + cast_in_specs,
        out_specs=[pl.BlockSpec((1, tq, group * HEAD_DIM), lambda bi, h, i: (bi, i, h))] + cast_out_specs,
        compiler_params=_cparams(("parallel", "parallel", "parallel")),
        name="diff_attn" if diff is not None else "gqa_attn",
    )(q, *kv_args, *extra_args, *cast_args)
    return outs[0] if not casts else outs


def _na_kernel(q_ref, k_ref, v_ref, kc_ref, vc_ref, bias_ref, o_ref, *, rows):
    band = NA_ROWS * GRID_W
    kc = kc_ref[0, 0]
    vc = vc_ref[0, 0]

    def body(r, _):
        rs = jnp.clip(r - NA_ROWS // 2, 0, rows - NA_ROWS)
        off = rs - r + (NA_ROWS - 1)
        qrows = pl.ds(pl.multiple_of(r * GRID_W, GRID_W), GRID_W)
        krows = pl.ds(pl.multiple_of(rs * GRID_W, GRID_W), band)
        q = q_ref[0, 0, qrows, :]
        bias = jnp.concatenate([bias_ref[0, off + 2 * j] for j in range(NA_ROWS // 2)], axis=1)
        s = lax.dot_general(q, k_ref[0, 0, krows, :], NT_DIMS, preferred_element_type=F32) + bias
        sc = lax.dot_general(q, kc, NT_DIMS, preferred_element_type=F32)
        m = jnp.maximum(jnp.max(s, axis=-1, keepdims=True), jnp.max(sc, axis=-1, keepdims=True))
        p = jnp.exp2((s - m).astype(BF16))
        pc = jnp.exp2((sc - m).astype(BF16))
        o = (jnp.dot(p, v_ref[0, 0, krows, :], preferred_element_type=F32)
             + jnp.dot(pc, vc, preferred_element_type=F32))
        o_ref[0, qrows, :] = (o[:, :HEAD_DIM] / o[:, HEAD_DIM:HEAD_DIM + 1]).astype(BF16)
        return 0

    lax.fori_loop(0, rows, body, 0, unroll=8)


def _na_attention(q, k, v, kc, vc, bias):
    b, nh, l, _ = q.shape
    rows = l // GRID_W

    def full(a):
        return pl.BlockSpec((1, 1) + a.shape[2:], lambda bi, h: (bi, h, 0, 0))

    return pl.pallas_call(
        functools.partial(_na_kernel, rows=rows),
        out_shape=jax.ShapeDtypeStruct((b, l, nh * HEAD_DIM), BF16),
        grid=(b, nh),
        in_specs=[full(q), full(k), full(v), full(kc), full(vc),
                  pl.BlockSpec((1,) + bias.shape[1:], lambda bi, h: (h, 0, 0, 0))],
        out_specs=pl.BlockSpec((1, l, HEAD_DIM), lambda bi, h: (bi, 0, h)),
        compiler_params=_cparams(("parallel", "parallel")),
        name="na_attn",
    )(q, k, v, kc, vc, bias)


def _out_proj_kernel(*refs, n_o, routed):
    o_refs = refs[:n_o]
    (w_ref, x_ref, gpost_ref, gt_ref, gpre_ref, sh_ref, sc_ref) = refs[n_o:n_o + 7]
    rest = refs[n_o + 7:]
    if routed:
        router_ref, x1_ref, h2_ref, route_ref = rest
    else:
        x1_ref, h2_ref = rest
    tm = x_ref.shape[1]
    sub = min(tm, 256)
    for r0 in range(0, tm, sub):
        rows = slice(r0, r0 + sub)
        acc = None
        c0 = 0
        for o_ref in o_refs:
            width = o_ref.shape[-1]
            part = jnp.dot(o_ref[0, rows, :], w_ref[c0:c0 + width, :], preferred_element_type=F32)
            acc = part if acc is None else acc + part
            c0 += width
        x1 = x_ref[0, rows, :] + gt_ref[0] * (_rms(acc) * gpost_ref[...])
        x1_ref[0, rows, :] = x1
        h2 = (_rms(x1) * gpre_ref[...]) * (1.0 + sc_ref[0]) + sh_ref[0]
        h2_ref[0, rows, :] = h2.astype(h2_ref.dtype)
        if routed:
            h_hi = h2.astype(BF16)
            h_lo = (h2 - h_hi.astype(F32)).astype(BF16)
            r = router_ref[...]
            r_hi = r.astype(BF16)
            r_lo = (r - r_hi.astype(F32)).astype(BF16)
            logits = (jnp.dot(h_hi, r_hi, preferred_element_type=F32)
                      + jnp.dot(h_hi, r_lo, preferred_element_type=F32)
                      + jnp.dot(h_lo, r_hi, preferred_element_type=F32))
            lane = lax.broadcasted_iota(jnp.int32, logits.shape, 1)
            logits = jnp.where(lane < N_EXPERTS, logits, NEG_BIG)
            v1 = jnp.max(logits, axis=-1, keepdims=True)
            i1 = jnp.min(jnp.where(logits == v1, lane, LANES), axis=-1, keepdims=True)
            rest_l = jnp.where(lane == i1, NEG_BIG, logits)
            v2 = jnp.max(rest_l, axis=-1, keepdims=True)
            i2 = jnp.min(jnp.where(rest_l == v2, lane, LANES), axis=-1, keepdims=True)
            e2 = jnp.exp(v2 - v1)
            g1 = 1.0 / (1.0 + e2)
            g2 = e2 / (1.0 + e2)
            route_ref[0, rows, :] = jnp.where(lane == 0, i1.astype(F32),
                                    jnp.where(lane == 1, i2.astype(F32),
                                    jnp.where(lane == 2, g1, jnp.where(lane == 3, g2, 0.0))))


def _out_proj(o_list, w_out, x, gpost, gt, gpre, sh, sc, router=None):
    b, l, d = x.shape
    tm = _pick(l, (512, 256, 128))
    routed = router is not None
    vec = pl.BlockSpec((1, d), lambda bi, i: (0, 0))
    mod = pl.BlockSpec((1, 1, d), lambda bi, i: (bi, 0, 0))
    row = pl.BlockSpec((1, tm, d), lambda bi, i: (bi, i, 0))
    in_specs = [pl.BlockSpec((1, tm, o.shape[-1]), lambda bi, i: (bi, i, 0)) for o in o_list]
    in_specs += [pl.BlockSpec(w_out.shape, lambda bi, i: (0, 0), pipeline_mode=pl.Buffered(1)),
                 row, vec, mod, vec, mod, mod]
    args = list(o_list) + [w_out, x, gpost, gt, gpre, sh, sc]
    out_shape = [jax.ShapeDtypeStruct((b, l, d), F32),
                 jax.ShapeDtypeStruct((b, l, d), F32 if routed else BF16)]
    out_specs = [row, row]
    if routed:
        in_specs.append(pl.BlockSpec(router.shape, lambda bi, i: (0, 0)))
        args.append(router)
        out_shape.append(jax.ShapeDtypeStruct((b, l, LANES), F32))
        out_specs.append(pl.BlockSpec((1, tm, LANES), lambda bi, i: (bi, i, 0)))
    return pl.pallas_call(
        functools.partial(_out_proj_kernel, n_o=len(o_list), routed=routed),
        out_shape=out_shape,
        grid=(b, l // tm),
        in_specs=in_specs,
        out_specs=out_specs,
        compiler_params=_cparams(("parallel", "parallel")),
        name="out_proj",
    )(*args)


def _swiglu_step(hb, w1, w3, w2):
    a = jnp.dot(hb, w1, preferred_element_type=F32)
    g = jnp.dot(hb, w3, preferred_element_type=F32)
    act = (a / (1.0 + jnp.exp(-a))) * g
    return jnp.dot(act.astype(BF16), w2, preferred_element_type=F32)


def _ffn_dense_kernel(h_ref, w1_ref, w3_ref, w2_ref, x_ref, gpost_ref, gt_ref, o_ref, acc_ref):
    f = pl.program_id(2)

    @pl.when(f == 0)
    def _():
        acc_ref[...] = jnp.zeros_like(acc_ref)

    acc_ref[...] += _swiglu_step(h_ref[0], w1_ref[0], w3_ref[0], w2_ref[...])

    @pl.when(f == pl.num_programs(2) - 1)
    def _():
        o_ref[0] = x_ref[0] + gt_ref[0] * (_rms(acc_ref[...]) * gpost_ref[...])


def _ff_tile(dff):
    return _pick(dff, (512, 256, 128))


def _chunk_major(w, tf):
    *lead, d, dff = w.shape
    return jnp.moveaxis(w.reshape(*lead, d, dff // tf, tf), -2, -3)


def _ffn_dense(h2, w1, w3, w2, x1, gpost, gt):
    b, l, d = x1.shape
    dff = w2.shape[0]
    tm = _pick(l, (512, 256, 128))
    tf = w1.shape[2]
    row = pl.BlockSpec((1, tm, d), lambda bi, i, f: (bi, i, 0))
    return pl.pallas_call(
        _ffn_dense_kernel,
        out_shape=jax.ShapeDtypeStruct((b, l, d), F32),
        grid=(b, l // tm, dff // tf),
        in_specs=[row,
                  pl.BlockSpec((1, d, tf), lambda bi, i, f: (f, 0, 0)),
                  pl.BlockSpec((1, d, tf), lambda bi, i, f: (f, 0, 0)),
                  pl.BlockSpec((tf, d), lambda bi, i, f: (f, 0)),
                  row,
                  pl.BlockSpec((1, d), lambda bi, i, f: (0, 0)),
                  pl.BlockSpec((1, 1, d), lambda bi, i, f: (bi, 0, 0))],
        out_specs=row,
        scratch_shapes=[pltpu.VMEM((tm, d), F32)],
        compiler_params=_cparams(("parallel", "parallel", "arbitrary")),
        name="ffn_dense",
    )(h2, w1, w3, w2, x1, gpost, gt)


def _ffn_expert_kernel(tile_e_ref, nact_ref, src0_ref, srcn_ref, h_hbm, w1_ref, w3_ref, w2_ref,
                       o_ref, xbuf_ref, hb_ref, acc_ref, sem, *, tm):
    j = pl.program_id(0)
    f = pl.program_id(1)
    nact = nact_ref[0]
    slot = j % 2

    def gather(idx_ref, dst_slot):
        def issue(i, _):
            pltpu.make_async_copy(h_hbm.at[pl.ds(idx_ref[i], 1)],
                                  xbuf_ref.at[dst_slot, pl.ds(i, 1)], sem.at[dst_slot]).start()
            return 0
        lax.fori_loop(0, tm, issue, 0, unroll=8)

    @pl.when(j < nact)
    def _():
        @pl.when(f == 0)
        def _():
            @pl.when(j == 0)
            def _():
                gather(src0_ref, 0)

            pltpu.make_async_copy(h_hbm.at[pl.ds(0, tm)], xbuf_ref.at[slot], sem.at[slot]).wait()
            hb_ref[...] = xbuf_ref[slot].astype(BF16)
            acc_ref[...] = jnp.zeros_like(acc_ref)

        @pl.when((f == 1) & (j + 1 < nact))
        def _():
            gather(srcn_ref, 1 - slot)

        acc_ref[...] += _swiglu_step(hb_ref[...], w1_ref[0, 0], w3_ref[0, 0], w2_ref[0])

        @pl.when(f == pl.num_programs(1) - 1)
        def _():
            o_ref[...] = acc_ref[...]

    @pl.when((j >= nact_ref[0]) & (f == pl.num_programs(1) - 1))
    def _():
        o_ref[...] = jnp.zeros_like(o_ref)


def _ffn_expert(h2, src, w1, w3, w2, tile_e, nact, tm):
    d = h2.shape[1]
    p = src.shape[0]
    n_tiles = p // tm
    nf, tf = w1.shape[1], w1.shape[3]
    assert nf >= 2

    def fsel(j, f, na):
        return jnp.where(j < na[0], f, nf - 1)

    return pl.pallas_call(
        functools.partial(_ffn_expert_kernel, tm=tm),
        out_shape=jax.ShapeDtypeStruct((p, d), F32),
        grid_spec=pltpu.PrefetchScalarGridSpec(
            num_scalar_prefetch=2,
            grid=(n_tiles, nf),
            in_specs=[pl.BlockSpec((tm,), lambda j, f, te, na: (0,), memory_space=pltpu.SMEM),
                      pl.BlockSpec((tm,), lambda j, f, te, na: (jnp.minimum(j + 1, n_tiles - 1),),
                                   memory_space=pltpu.SMEM),
                      pl.BlockSpec(memory_space=pl.ANY),
                      pl.BlockSpec((1, 1, d, tf), lambda j, f, te, na: (te[j], fsel(j, f, na), 0, 0)),
                      pl.BlockSpec((1, 1, d, tf), lambda j, f, te, na: (te[j], fsel(j, f, na), 0, 0)),
                      pl.BlockSpec((1, tf, d), lambda j, f, te, na: (te[j], fsel(j, f, na), 0))],
            out_specs=pl.BlockSpec((tm, d), lambda j, f, te, na: (j, 0)),
            scratch_shapes=[pltpu.VMEM((2, tm, d), F32), pltpu.VMEM((tm, d), BF16),
                            pltpu.VMEM((tm, d), F32), pltpu.SemaphoreType.DMA((2,))]),
        compiler_params=_cparams(("arbitrary", "arbitrary")),
        name="ffn_expert",
    )(tile_e, nact, src, src, h2, w1, w3, w2)


def _rank_kernel(route_ref, rank_ref, cnt_ref, carry_ref):
    i = pl.program_id(0)

    @pl.when(i == 0)
    def _():
        carry_ref[...] = jnp.zeros_like(carry_ref)

    r = route_ref[...]
    t = r.shape[0]
    lane = lax.broadcasted_iota(jnp.int32, r.shape, 1)
    e1 = r[:, 0:1].astype(jnp.int32)
    e2 = r[:, 1:2].astype(jnp.int32)
    oh1 = (lane == e1).astype(F32)
    oh2 = (lane == e2).astype(F32)
    c = (oh1 + oh2).astype(BF16)
    row = lax.broadcasted_iota(jnp.int32, (t, t), 0)
    col = lax.broadcasted_iota(jnp.int32, (t, t), 1)
    tri = (col < row).astype(F32).astype(BF16)
    prefix = jnp.dot(tri, c, preferred_element_type=F32) + carry_ref[0:1, :]
    r1 = jnp.sum(prefix * oh1, axis=-1, keepdims=True)
    r2 = jnp.sum(prefix * oh2, axis=-1, keepdims=True)
    rank_ref[...] = jnp.where(lane == 0, r1, jnp.where(lane == 1, r2, 0.0))
    total = carry_ref[0:1, :] + jnp.sum(oh1 + oh2, axis=0, keepdims=True)
    carry_ref[...] = jnp.broadcast_to(total, carry_ref.shape)
    cnt_ref[...] = jnp.broadcast_to(total, cnt_ref.shape)


def _expert_ranks(route):
    n = route.shape[0]
    t = _pick(n, (512, 256, 128))
    return pl.pallas_call(
        _rank_kernel,
        out_shape=[jax.ShapeDtypeStruct((n, LANES), F32), jax.ShapeDtypeStruct((8, LANES), F32)],
        grid=(n // t,),
        in_specs=[pl.BlockSpec((t, LANES), lambda i: (i, 0))],
        out_specs=[pl.BlockSpec((t, LANES), lambda i: (i, 0)),
                   pl.BlockSpec((8, LANES), lambda i: (0, 0))],
        scratch_shapes=[pltpu.VMEM((8, LANES), F32)],
        compiler_params=_cparams(("arbitrary",)),
        name="expert_ranks",
    )(route)


def _combine_kernel(pos0_ref, posn_ref, y_hbm, route_ref, x_ref, gpost_ref, gt_ref, o_ref,
                    buf_ref, sem, *, tt):
    n_steps = pl.num_programs(0) * pl.num_programs(1)
    step = pl.program_id(0) * pl.num_programs(1) + pl.program_id(1)
    slot = step % 2

    def gather(p_ref, dst_slot):
        def issue(i, _):
            for k in range(TOP_K):
                pltpu.make_async_copy(y_hbm.at[pl.ds(p_ref[TOP_K * i + k], 1)],
                                      buf_ref.at[dst_slot, pl.ds(k * tt + i, 1)],
                                      sem.at[dst_slot]).start()
            return 0
        lax.fori_loop(0, tt, issue, 0, unroll=4)

    @pl.when(step == 0)
    def _():
        gather(pos0_ref, 0)

    @pl.when(step + 1 < n_steps)
    def _():
        gather(posn_ref, 1 - slot)

    pltpu.make_async_copy(y_hbm.at[pl.ds(0, TOP_K * tt)], buf_ref.at[slot], sem.at[slot]).wait()
    r = route_ref[0]
    y = r[:, 2:3] * buf_ref[slot, 0:tt, :] + r[:, 3:4] * buf_ref[slot, tt:2 * tt, :]
    o_ref[0] = x_ref[0] + gt_ref[0] * (_rms(y) * gpost_ref[...])


def _combine(y, pos_flat, route, x1, gpost, gt):
    b, l, d = x1.shape
    tt = _pick(l, (256, 128))
    nt = l // tt
    n_steps = b * nt
    row = pl.BlockSpec((1, tt, d), lambda bi, i: (bi, i, 0))
    return pl.pallas_call(
        functools.partial(_combine_kernel, tt=tt),
        out_shape=jax.ShapeDtypeStruct((b, l, d), F32),
        grid=(b, nt),
        in_specs=[pl.BlockSpec((TOP_K * tt,), lambda bi, i: (0,), memory_space=pltpu.SMEM),
                  pl.BlockSpec((TOP_K * tt,),
                               lambda bi, i: (jnp.minimum(bi * nt + i + 1, n_steps - 1),),
                               memory_space=pltpu.SMEM),
                  pl.BlockSpec(memory_space=pl.ANY),
                  pl.BlockSpec((1, tt, LANES), lambda bi, i: (bi, i, 0)),
                  row,
                  pl.BlockSpec((1, d), lambda bi, i: (0, 0)),
                  pl.BlockSpec((1, 1, d), lambda bi, i: (bi, 0, 0))],
        out_specs=row,
        scratch_shapes=[pltpu.VMEM((2, TOP_K * tt, d), F32), pltpu.SemaphoreType.DMA((2,))],
        compiler_params=_cparams(("arbitrary", "arbitrary")),
        name="moe_combine",
    )(pos_flat, pos_flat, y, route, x1, gpost, gt)


def _moe_ffn(h2, route, w1, w3, w2, x1, gpost, gt):
    b, l, d = x1.shape
    n = b * l
    tm = 512
    route2 = route.reshape(n, LANES)
    rank, counts = _expert_ranks(route2)
    cnt = counts[0, :N_EXPERTS].astype(jnp.int32)
    tiles = (cnt + tm - 1) // tm
    tile_end = jnp.cumsum(tiles)
    row_off = (tile_end - tiles) * tm
    e_idx = route2[:, :TOP_K].astype(jnp.int32)
    pos = (row_off[e_idx] + rank[:, :TOP_K].astype(jnp.int32)).reshape(-1)
    n_tiles = (n * TOP_K) // tm + N_EXPERTS
    nact = tile_end[-1:]
    tile_ids = jnp.minimum(jnp.arange(n_tiles, dtype=jnp.int32), nact[0] - 1)
    tile_e = jnp.sum((tile_ids[:, None] >= tile_end[None, :]).astype(jnp.int32), axis=1)
    token = jnp.arange(n * TOP_K, dtype=jnp.int32) // TOP_K
    src = jnp.zeros((n_tiles * tm,), jnp.int32).at[pos].set(token)
    y = _ffn_expert(h2.reshape(n, d), src, w1, w3, w2, tile_e, nact.astype(jnp.int32), tm)
    return _combine(y, pos, route, x1, gpost, gt)


def _rope_tables(l, dims_per_map):
    lane = jnp.arange(LANES)
    d = lane % dims_per_map
    part_w = dims_per_map // 2
    half = part_w // 2
    is_col = (d // part_w) == 1
    first = (d % part_w) < half
    inv_freq = ROPE_THETA ** (-jnp.arange(half, dtype=F32) / half)
    freq = inv_freq[d % half]
    pos = jnp.arange(l)
    p = jnp.where(is_col[None, :], (pos % GRID_W)[:, None], (pos // GRID_W)[:, None]).astype(F32)
    ang = p * freq[None, :]
    cos, sin = jnp.cos(ang), jnp.sin(ang)
    zero = jnp.zeros_like(sin)
    return cos, jnp.where(first[None, :], -sin, zero), jnp.where(first[None, :], zero, sin)


def _identity_tables(l):
    return jnp.ones((l, LANES), F32), jnp.zeros((l, LANES), F32), jnp.zeros((l, LANES), F32)


def kernel(x, c, ctx, c_ctx, w_mod, b_mod, g_mix_pre, g_mix_post, g_ffn_pre, g_ffn_post,
           w_in, w_out, na_rpb, gqa_q_norm, gqa_k_norm, diff_lambda_q1, diff_lambda_k1,
           diff_lambda_q2, diff_lambda_k2, diff_subln, ffn_w1, ffn_w3, ffn_w2,
           moe_router, moe_w1, moe_w3, moe_w2):
    b, l, d = x.shape
    cn = ctx.shape[1]
    depth = w_mod.shape[0]
    n_heads = d // HEAD_DIM
    n_na, n_gq, n_diff = n_heads // 4, n_heads // 2, n_heads // 4
    n_gkv = n_gq // 4
    group = n_gq // n_gkv
    dims = (n_na, n_gq, n_gkv, n_diff)

    rows_pad = -(-(b + 1) // 8) * 8
    cvec = jnp.zeros((rows_pad, d), F32).at[:b].set(c).at[b].set(c_ctx)
    mods = _modulation(cvec, w_mod, b_mod)
    bias_tab = _na_bias_table(na_rpb.reshape((depth * n_na,) + na_rpb.shape[2:]))
    bias_tab = bias_tab.reshape((depth, n_na) + bias_tab.shape[1:])

    tabs_g = _rope_tables(l, HEAD_DIM)
    tabs_d = _rope_tables(l, DIFF_QK_DIM)
    tabs_id = _identity_tables(cn)

    x_ctx = ctx
    moe_bf16 = {}
    for li in range(depth):
        last = li == depth - 1
        lam_init = 0.8 - 0.6 * math.exp(-0.3 * li)
        m6 = mods[li].reshape(rows_pad, 6, d)
        sh1, sc1, gt1, sh2, sc2, gt2 = [m6[:b, k][:, None, :] for k in range(6)]
        csh1, csc1, cgt1, csh2, csc2, cgt2 = [
            jnp.broadcast_to(m6[b, k][None, None, :], (b, 1, d)) for k in range(6)]
        vec = lambda a: a[li].reshape(1, -1)
        w_in_b = w_in[li].astype(BF16)
        w_out_b = w_out[li].astype(BF16)
        qn, kn = vec(gqa_q_norm), vec(gqa_k_norm)
        diff_args = (vec(diff_lambda_q1), vec(diff_lambda_k1), vec(diff_lambda_q2),
                     vec(diff_lambda_k2), vec(diff_subln), lam_init)

        (naq, nak, nav, gq, gk, gv, dq, dk, dv) = _in_proj(
            x, vec(g_mix_pre), sh1, sc1, w_in_b, qn, kn, tabs_g, tabs_d, dims)
        (naqc, nakc, navc, gqc, gkc, gvc, dqc, dkc, dvc) = _in_proj(
            x_ctx, vec(g_mix_pre), csh1, csc1, w_in_b, qn, kn, tabs_id, tabs_id, dims)

        oa = _na_attention(naq, nak, nav, nakc, navc, bias_tab[li])
        tq_g, tq_d = _pick(l, (256, 128)), _pick(l, (256, 128))
        casts_g, casts_d = [], []
        if (li + 1) < depth and (li + 1) % 2 == 1:
            mi = (li + 1) // 2
            e_, d_, f_ = moe_w1.shape[1:]
            steps_g, steps_d = b * n_gkv * (l // tq_g), b * n_diff * (l // tq_d)
            rows_g = _cast_rows(moe_w1[mi].reshape(e_ * d_, f_), steps_g)
            rows_d = _cast_rows(moe_w3[mi].reshape(e_ * d_, f_), steps_d)
            if (rows_g and rows_d and d_ % rows_g == 0 and d_ % rows_d == 0
                    and _cast_rows(moe_w2[mi].reshape(e_ * f_, d_), steps_d)):
                casts_g, casts_d = [moe_w1[mi]], [moe_w3[mi], moe_w2[mi].reshape(e_ * f_, d_)]
        ob = _attention(gq, [(gk, gv), (gkc, gvc)], group=group, tq=tq_g, casts=casts_g)
        oc = _attention(dq, [(dk, dv), (dkc, dvc)], group=1, tq=tq_d, diff=diff_args, casts=casts_d)
        if casts_g:
            ob, w1b = ob
            oc, w3b, w2b = oc
            moe_bf16[li + 1] = (w1b, w3b, w2b.reshape(e_, f_, d_))

        routed = li % 2 == 1
        if routed:
            router = jnp.zeros((d, LANES), F32).at[:, :N_EXPERTS].set(moe_router[li // 2])
            x1, h2, route = _out_proj([oa, ob, oc], w_out_b, x, vec(g_mix_post), gt1,
                                      vec(g_ffn_pre), sh2, sc2, router=router)
        else:
            x1, h2 = _out_proj([oa, ob, oc], w_out_b, x, vec(g_mix_post), gt1,
                               vec(g_ffn_pre), sh2, sc2)

        if not last:
            tqc = _pick(cn, (256, 128))
            oac = _attention(naqc, [(nakc, navc)], group=1, tq=tqc)
            obc = _attention(gqc, [(gkc, gvc)], group=group, tq=_pick(cn, (128,)))
            occ = _attention(dqc, [(dkc, dvc)], group=1, tq=tqc, diff=diff_args)
            if routed:
                router = jnp.zeros((d, LANES), F32).at[:, :N_EXPERTS].set(moe_router[li // 2])
                xc1, h2c, route_c = _out_proj([oac, obc, occ], w_out_b, x_ctx, vec(g_mix_post),
                                              cgt1, vec(g_ffn_pre), csh2, csc2, router=router)
            else:
                xc1, h2c = _out_proj([oac, obc, occ], w_out_b, x_ctx, vec(g_mix_post), cgt1,
                                     vec(g_ffn_pre), csh2, csc2)

        if routed:
            if li in moe_bf16:
                w1, w3, w2 = moe_bf16.pop(li)
            else:
                tf = _ff_tile(moe_w1.shape[3])
                w1 = _chunk_major(moe_w1[li // 2], tf).astype(BF16)
                w3 = _chunk_major(moe_w3[li // 2], tf).astype(BF16)
                w2 = moe_w2[li // 2].astype(BF16)
            x = _moe_ffn(h2, route, w1, w3, w2, x1, vec(g_ffn_post), gt2)
            if not last:
                x_ctx = _moe_ffn(h2c, route_c, w1, w3, w2, xc1, vec(g_ffn_post), cgt2)
        else:
            tf = _ff_tile(ffn_w1.shape[2])
            w1 = _chunk_major(ffn_w1[li // 2], tf).astype(BF16)
            w3 = _chunk_major(ffn_w3[li // 2], tf).astype(BF16)
            w2 = ffn_w2[li // 2].astype(BF16)
            x = _ffn_dense(h2, w1, w3, w2, x1, vec(g_ffn_post), gt2)
            if not last:
                x_ctx = _ffn_dense(h2c, w1, w3, w2, xc1, vec(g_ffn_post), cgt2)
    return x
```

```python
import functools
import math

import jax
import jax.numpy as jnp
from jax import lax
from jax.experimental import pallas as pl
from jax.experimental.pallas import tpu as pltpu

GRID_W = 64
HEAD_DIM = 128
DIFF_QK_DIM = HEAD_DIM // 2
NA_ROWS = 8
NA_COLS = 16
ROPE_THETA = 10000.0
N_EXPERTS = 8
TOP_K = 2
EPS = 1e-6

LANES = 128
MXU_WIDTH = 256
NEG_BIG = -1e30
LOG2E = math.log2(math.e)
VMEM_LIMIT = 56 * 1024 * 1024

BF16 = jnp.bfloat16
F32 = jnp.float32
NT_DIMS = (((1,), (1,)), ((), ()))


def _cparams(sem):
    return pltpu.CompilerParams(dimension_semantics=sem, vmem_limit_bytes=VMEM_LIMIT)


def _pick(n, cands):
    for c in cands:
        if n % c == 0:
            return c
    raise ValueError(f"no tile in {cands} divides {n}")


def _rms(y):
    return y * lax.rsqrt(jnp.mean(y * y, axis=-1, keepdims=True) + EPS)


def _mod_kernel(c_ref, w_ref, b_ref, o_ref):
    c = c_ref[...]
    s = c / (1.0 + jnp.exp(-c))
    w = w_ref[0]
    s_hi = s.astype(BF16)
    s_lo = (s - s_hi.astype(F32)).astype(BF16)
    w_hi = w.astype(BF16)
    w_lo = (w - w_hi.astype(F32)).astype(BF16)
    o_ref[0] = (jnp.dot(s_hi, w_hi, preferred_element_type=F32)
                + jnp.dot(s_hi, w_lo, preferred_element_type=F32)
                + jnp.dot(s_lo, w_hi, preferred_element_type=F32)) + b_ref[0]


def _modulation(cvec, w_mod, b_mod):
    depth, d, n = w_mod.shape
    r = cvec.shape[0]
    tn = _pick(n, (1024, 512, 256, 128))
    return pl.pallas_call(
        _mod_kernel,
        out_shape=jax.ShapeDtypeStruct((depth, r, n), F32),
        grid=(depth, n // tn),
        in_specs=[pl.BlockSpec((r, d), lambda l, j: (0, 0)),
                  pl.BlockSpec((1, d, tn), lambda l, j: (l, 0, j)),
                  pl.BlockSpec((1, 1, tn), lambda l, j: (l, 0, j))],
        out_specs=pl.BlockSpec((1, r, tn), lambda l, j: (l, 0, j)),
        compiler_params=_cparams(("parallel", "parallel")),
        name="adaln_mod",
    )(cvec, w_mod, b_mod.reshape(depth, 1, n))


def _na_bias_kernel(rpb_ref, o_ref):
    n_dc = 2 * NA_COLS - 1
    base = pl.program_id(0) * ((2 * NA_ROWS - 1) * n_dc)
    shape = (GRID_W, LANES)
    qc = lax.broadcasted_iota(jnp.int32, shape, 0)
    lane = lax.broadcasted_iota(jnp.int32, shape, 1)
    half = lane >> 6
    kc = lane & (GRID_W - 1)
    dc = jnp.clip(kc - qc, -(NA_COLS - 1), NA_COLS - 1) + (NA_COLS - 1)
    cs = jnp.clip(qc - NA_COLS // 2, 0, GRID_W - NA_COLS)
    ok = (kc >= cs) & (kc < cs + NA_COLS)
    code = jnp.where(ok, half * 32 + dc, -1)
    for d in range(2 * NA_ROWS - 2):
        acc = jnp.full(shape, NEG_BIG, F32)
        for hf in range(2):
            for j in range(n_dc):
                val = rpb_ref[base + (d + hf) * n_dc + j] * LOG2E
                acc = jnp.where(code == hf * 32 + j, val, acc)
        o_ref[0, d] = acc


def _na_bias_table(rpb):
    nh = rpb.shape[0]
    return pl.pallas_call(
        _na_bias_kernel,
        out_shape=jax.ShapeDtypeStruct((nh, 2 * NA_ROWS - 2, GRID_W, LANES), F32),
        grid=(nh,),
        in_specs=[pl.BlockSpec(memory_space=pltpu.SMEM)],
        out_specs=pl.BlockSpec((1, 2 * NA_ROWS - 2, GRID_W, LANES), lambda h: (h, 0, 0, 0)),
        compiler_params=_cparams(("parallel",)),
        name="na_bias_table",
    )(rpb.reshape(-1))


def _rope(y, cos, sin_hi, sin_lo, dist):
    return (y * cos + pltpu.roll(y, LANES - dist, 1) * sin_hi
            + pltpu.roll(y, dist, 1) * sin_lo)


def _in_proj_kernel(x_ref, g_ref, sh_ref, sc_ref, w_ref, qn_ref, kn_ref,
                    cg_ref, sga_ref, sgb_ref, cd_ref, sda_ref, sdb_ref,
                    naq_ref, nak_ref, nav_ref, gq_ref, gk_ref, gv_ref,
                    dq_ref, dk_ref, dv_ref, *, n_na, n_gq, n_gkv, n_diff):
    x = x_ref[0]
    h = _rms(x) * g_ref[...]
    h = h * (1.0 + sc_ref[0]) + sh_ref[0]
    hb = h.astype(BF16)

    def proj(c0, nheads):
        return jnp.dot(hb, w_ref[:, c0:c0 + nheads * HEAD_DIM], preferred_element_type=F32)

    def heads(y, nheads):
        return [y[:, i * HEAD_DIM:(i + 1) * HEAD_DIM] for i in range(nheads)]

    na_scale = HEAD_DIM ** -0.5 * LOG2E
    diff_scale = DIFF_QK_DIM ** -0.5 * LOG2E
    c0 = 0
    for i, y in enumerate(heads(proj(c0, n_na), n_na)):
        naq_ref[0, i] = (y * na_scale).astype(BF16)
    c0 += n_na * HEAD_DIM
    for i, y in enumerate(heads(proj(c0, n_na), n_na)):
        nak_ref[0, i] = y.astype(BF16)
    c0 += n_na * HEAD_DIM
    lane = lax.broadcasted_iota(jnp.int32, (x.shape[0], LANES), 1)
    ones_col = jnp.where(lane == 0, 1.0, 0.0).astype(BF16)
    for i, y in enumerate(heads(proj(c0, n_na), n_na)):
        nav_ref[0, i, :, :HEAD_DIM] = y.astype(BF16)
        nav_ref[0, i, :, HEAD_DIM:] = ones_col
    c0 += n_na * HEAD_DIM

    cg, sga, sgb = cg_ref[...], sga_ref[...], sgb_ref[...]
    for i, y in enumerate(heads(proj(c0, n_gq), n_gq)):
        y = _rope(_rms(y) * qn_ref[...], cg, sga, sgb, 32)
        gq_ref[0, i] = (y * na_scale).astype(BF16)
    c0 += n_gq * HEAD_DIM
    for i, y in enumerate(heads(proj(c0, n_gkv), n_gkv)):
        y = _rope(_rms(y) * kn_ref[...], cg, sga, sgb, 32)
        gk_ref[0, i] = y.astype(BF16)
    c0 += n_gkv * HEAD_DIM
    for i, y in enumerate(heads(proj(c0, n_gkv), n_gkv)):
        gv_ref[0, i, :, :HEAD_DIM] = y.astype(BF16)
        gv_ref[0, i, :, HEAD_DIM:] = ones_col
    c0 += n_gkv * HEAD_DIM

    cd, sda, sdb = cd_ref[...], sda_ref[...], sdb_ref[...]
    for i, y in enumerate(heads(proj(c0, n_diff), n_diff)):
        y = _rope(y, cd, sda, sdb, 16) * diff_scale
        dq_ref[0, 2 * i] = jnp.where(lane < DIFF_QK_DIM, y, 0.0).astype(BF16)
        dq_ref[0, 2 * i + 1] = jnp.where(lane >= DIFF_QK_DIM, y, 0.0).astype(BF16)
    c0 += n_diff * HEAD_DIM
    for i, y in enumerate(heads(proj(c0, n_diff), n_diff)):
        dk_ref[0, i] = _rope(y, cd, sda, sdb, 16).astype(BF16)
    c0 += n_diff * HEAD_DIM
    for i, y in enumerate(heads(proj(c0, n_diff), n_diff)):
        dv_ref[0, i, :, :HEAD_DIM] = y.astype(BF16)
        dv_ref[0, i, :, HEAD_DIM:] = ones_col


def _in_proj(x, g, sh, sc, w_in, qn, kn, tabs_g, tabs_d, dims):
    b, l, d = x.shape
    n_na, n_gq, n_gkv, n_diff = dims
    n_in = w_in.shape[1]
    tm = _pick(l, (256, 128))
    vec = pl.BlockSpec((1, d), lambda bi, i: (0, 0))
    mod = pl.BlockSpec((1, 1, d), lambda bi, i: (bi, 0, 0))
    hvec = pl.BlockSpec((1, HEAD_DIM), lambda bi, i: (0, 0))
    tab = pl.BlockSpec((tm, LANES), lambda bi, i: (i, 0))

    def out(nh, width=HEAD_DIM):
        return (jax.ShapeDtypeStruct((b, nh, l, width), BF16),
                pl.BlockSpec((1, nh, tm, width), lambda bi, i: (bi, 0, i, 0)))

    outs = [out(n_na), out(n_na), out(n_na, 2 * HEAD_DIM), out(n_gq), out(n_gkv),
            out(n_gkv, 2 * HEAD_DIM), out(2 * n_diff), out(n_diff), out(n_diff, 2 * HEAD_DIM)]
    kern = functools.partial(_in_proj_kernel, n_na=n_na, n_gq=n_gq, n_gkv=n_gkv, n_diff=n_diff)
    return pl.pallas_call(
        kern,
        out_shape=[o[0] for o in outs],
        grid=(b, l // tm),
        in_specs=[pl.BlockSpec((1, tm, d), lambda bi, i: (bi, i, 0)), vec, mod, mod,
                  pl.BlockSpec((d, n_in), lambda bi, i: (0, 0), pipeline_mode=pl.Buffered(1)),
                  hvec, hvec, tab, tab, tab, tab, tab, tab],
        out_specs=[o[1] for o in outs],
        compiler_params=_cparams(("parallel", "parallel")),
        name="in_proj",
    )(x, g, sh, sc, w_in, qn, kn, *tabs_g, *tabs_d)


def _attn_kernel(*refs, n_seg, group, tq, chunks, diff, lam_init, n_cast):
    q_ref = refs[0]
    kv_refs = refs[1:1 + 2 * n_seg]
    rest = list(refs[1 + 2 * n_seg:])
    cast_out = [rest.pop() for _ in range(n_cast)][::-1]
    o_ref = rest.pop()
    cast_in = [rest.pop() for _ in range(n_cast)][::-1]
    for src_ref, dst_ref in zip(cast_in, cast_out):
        dst_ref[...] = src_ref[...].astype(BF16)
    if diff:
        lq1_ref, lk1_ref, lq2_ref, lk2_ref, sub_ref = rest
    m_rows = group * tq

    def step(k, v, carry):
        m, acc = carry
        q = q_ref[0].reshape(m_rows, HEAD_DIM)
        s = lax.dot_general(q, k, NT_DIMS, preferred_element_type=F32)
        m_new = jnp.maximum(m, jnp.max(s, axis=-1, keepdims=True))
        p = jnp.exp2((s - m_new).astype(BF16))
        acc = jnp.exp2(m - m_new) * acc + jnp.dot(p, v, preferred_element_type=F32)
        return m_new, acc

    carry = (jnp.full((m_rows, 1), NEG_BIG, F32), jnp.zeros((m_rows, 2 * HEAD_DIM), F32))
    for si in range(n_seg):
        k_ref, v_ref = kv_refs[2 * si], kv_refs[2 * si + 1]
        n_chunks, tk = chunks[si]
        for c in range(n_chunks):
            rows = slice(c * tk, (c + 1) * tk)
            carry = step(k_ref[0, 0, rows, :], v_ref[0, 0, rows, :], carry)
    _, acc = carry
    o = acc[:, :HEAD_DIM] / acc[:, HEAD_DIM:HEAD_DIM + 1]
    if diff:
        lam = (jnp.exp(jnp.sum(lq1_ref[...] * lk1_ref[...], axis=-1, keepdims=True))
               - jnp.exp(jnp.sum(lq2_ref[...] * lk2_ref[...], axis=-1, keepdims=True)) + lam_init)
        od = o[:tq] - lam * o[tq:]
        o_ref[0] = (_rms(od) * sub_ref[...] * (1.0 - lam_init)).astype(BF16)
    else:
        for g in range(group):
            o_ref[0, :, g * HEAD_DIM:(g + 1) * HEAD_DIM] = o[g * tq:(g + 1) * tq].astype(BF16)


def _cast_rows(a, n_steps):
    rows = a.shape[0] // n_steps
    return rows if rows * n_steps == a.shape[0] and rows % 16 == 0 else 0


def _attention(q, segs, *, group, tq, diff=None, casts=()):
    b, hq, lq, _ = q.shape
    hkv = hq // group
    nq = lq // tq
    out_heads = 1 if diff is not None else group
    kv_args, kv_specs, chunks = [], [], []
    for k, v in segs:
        s = k.shape[2]
        tk = _pick(s, (256, 128))
        chunks.append((s // tk, tk))
        kv_args += [k, v]
        kv_specs += [pl.BlockSpec((1, 1, s, a.shape[3]), lambda bi, h, i: (bi, h, 0, 0))
                     for a in (k, v)]
    extra_args, extra_specs, lam_init = [], [], 0.0
    if diff is not None:
        lq1, lk1, lq2, lk2, subln, lam_init = diff
        extra_args = [lq1, lk1, lq2, lk2, subln]
        extra_specs = [pl.BlockSpec((1, a.shape[1]), lambda bi, h, i: (0, 0)) for a in extra_args]
    cast_specs = [pl.BlockSpec((_cast_rows(a, b * hkv * nq), a.shape[1]),
                               lambda bi, h, i: ((bi * hkv + h) * nq + i, 0)) for a in casts]
    kern = functools.partial(_attn_kernel, n_seg=len(segs), group=group, tq=tq,
                             chunks=tuple(chunks), diff=diff is not None, lam_init=lam_init,
                             n_cast=len(casts))
    outs = pl.pallas_call(
        kern,
        out_shape=[jax.ShapeDtypeStruct((b, lq, hkv * out_heads * HEAD_DIM), BF16)]
                  + [jax.ShapeDtypeStruct(a.shape, BF16) for a in casts],
        grid=(b, hkv, nq),
        in_specs=[pl.BlockSpec((1, group, tq, HEAD_DIM), lambda bi, h, i: (bi, h, i, 0))]
                 + kv_specs + extra_specs + cast_specs,
        out_specs=[pl.BlockSpec((1, tq, out_heads * HEAD_DIM), lambda bi, h, i: (bi, i, h))] + cast_specs,
        compiler_params=_cparams(("parallel", "parallel", "parallel")),
        name="diff_attn" if diff is not None else "gqa_attn",
    )(q, *kv_args, *extra_args, *casts)
    return outs[0] if not casts else outs


def _na_kernel(q_ref, k_ref, v_ref, kc_ref, vc_ref, bias_ref, o_ref, *, rows):
    band = NA_ROWS * GRID_W
    kc = kc_ref[0, 0]
    vc = vc_ref[0, 0]

    def body(r, _):
        rs = jnp.clip(r - NA_ROWS // 2, 0, rows - NA_ROWS)
        off = rs - r + (NA_ROWS - 1)
        qrows = pl.ds(pl.multiple_of(r * GRID_W, GRID_W), GRID_W)
        krows = pl.ds(pl.multiple_of(rs * GRID_W, GRID_W), band)
        q = q_ref[0, 0, qrows, :]
        bias = jnp.concatenate([bias_ref[0, off + 2 * j] for j in range(NA_ROWS // 2)], axis=1)
        s = lax.dot_general(q, k_ref[0, 0, krows, :], NT_DIMS, preferred_element_type=F32) + bias
        sc = lax.dot_general(q, kc, NT_DIMS, preferred_element_type=F32)
        m = jnp.maximum(jnp.max(s, axis=-1, keepdims=True), jnp.max(sc, axis=-1, keepdims=True))
        p = jnp.exp2((s - m).astype(BF16))
        pc = jnp.exp2((sc - m).astype(BF16))
        o = (jnp.dot(p, v_ref[0, 0, krows, :], preferred_element_type=F32)
             + jnp.dot(pc, vc, preferred_element_type=F32))
        o_ref[0, qrows, :] = (o[:, :HEAD_DIM] / o[:, HEAD_DIM:HEAD_DIM + 1]).astype(BF16)
        return 0

    lax.fori_loop(0, rows, body, 0, unroll=8)


def _na_attention(q, k, v, kc, vc, bias):
    b, nh, l, _ = q.shape
    rows = l // GRID_W

    def full(a):
        return pl.BlockSpec((1, 1) + a.shape[2:], lambda bi, h: (bi, h, 0, 0))

    return pl.pallas_call(
        functools.partial(_na_kernel, rows=rows),
        out_shape=jax.ShapeDtypeStruct((b, l, nh * HEAD_DIM), BF16),
        grid=(b, nh),
        in_specs=[full(q), full(k), full(v), full(kc), full(vc),
                  pl.BlockSpec((1,) + bias.shape[1:], lambda bi, h: (h, 0, 0, 0))],
        out_specs=pl.BlockSpec((1, l, HEAD_DIM), lambda bi, h: (bi, 0, h)),
        compiler_params=_cparams(("parallel", "parallel")),
        name="na_attn",
    )(q, k, v, kc, vc, bias)


def _out_proj_kernel(*refs, n_o, routed):
    o_refs = refs[:n_o]
    (w_ref, x_ref, gpost_ref, gt_ref, gpre_ref, sh_ref, sc_ref) = refs[n_o:n_o + 7]
    rest = refs[n_o + 7:]
    if routed:
        router_ref, x1_ref, h2_ref, route_ref = rest
    else:
        x1_ref, h2_ref = rest
    tm = x_ref.shape[1]
    sub = min(tm, 256)
    for r0 in range(0, tm, sub):
        rows = slice(r0, r0 + sub)
        acc = None
        c0 = 0
        for o_ref in o_refs:
            width = o_ref.shape[-1]
            part = jnp.dot(o_ref[0, rows, :], w_ref[c0:c0 + width, :], preferred_element_type=F32)
            acc = part if acc is None else acc + part
            c0 += width
        x1 = x_ref[0, rows, :] + gt_ref[0] * (_rms(acc) * gpost_ref[...])
        x1_ref[0, rows, :] = x1
        h2 = (_rms(x1) * gpre_ref[...]) * (1.0 + sc_ref[0]) + sh_ref[0]
        h2_ref[0, rows, :] = h2.astype(h2_ref.dtype)
        if routed:
            h_hi = h2.astype(BF16)
            h_lo = (h2 - h_hi.astype(F32)).astype(BF16)
            r = router_ref[...]
            r_hi = r.astype(BF16)
            r_lo = (r - r_hi.astype(F32)).astype(BF16)
            logits = (jnp.dot(h_hi, r_hi, preferred_element_type=F32)
                      + jnp.dot(h_hi, r_lo, preferred_element_type=F32)
                      + jnp.dot(h_lo, r_hi, preferred_element_type=F32))
            lane = lax.broadcasted_iota(jnp.int32, logits.shape, 1)
            logits = jnp.where(lane < N_EXPERTS, logits, NEG_BIG)
            v1 = jnp.max(logits, axis=-1, keepdims=True)
            i1 = jnp.min(jnp.where(logits == v1, lane, LANES), axis=-1, keepdims=True)
            rest_l = jnp.where(lane == i1, NEG_BIG, logits)
            v2 = jnp.max(rest_l, axis=-1, keepdims=True)
            i2 = jnp.min(jnp.where(rest_l == v2, lane, LANES), axis=-1, keepdims=True)
            e2 = jnp.exp(v2 - v1)
            g1 = 1.0 / (1.0 + e2)
            g2 = e2 / (1.0 + e2)
            route_ref[0, rows, :] = jnp.where(lane == 0, i1.astype(F32),
                                    jnp.where(lane == 1, i2.astype(F32),
                                    jnp.where(lane == 2, g1, jnp.where(lane == 3, g2, 0.0))))


def _out_proj(o_list, w_out, x, gpost, gt, gpre, sh, sc, router=None):
    b, l, d = x.shape
    tm = _pick(l, (512, 256, 128))
    routed = router is not None
    vec = pl.BlockSpec((1, d), lambda bi, i: (0, 0))
    mod = pl.BlockSpec((1, 1, d), lambda bi, i: (bi, 0, 0))
    row = pl.BlockSpec((1, tm, d), lambda bi, i: (bi, i, 0))
    in_specs = [pl.BlockSpec((1, tm, o.shape[-1]), lambda bi, i: (bi, i, 0)) for o in o_list]
    in_specs += [pl.BlockSpec(w_out.shape, lambda bi, i: (0, 0), pipeline_mode=pl.Buffered(1)),
                 row, vec, mod, vec, mod, mod]
    args = list(o_list) + [w_out, x, gpost, gt, gpre, sh, sc]
    out_shape = [jax.ShapeDtypeStruct((b, l, d), F32),
                 jax.ShapeDtypeStruct((b, l, d), F32 if routed else BF16)]
    out_specs = [row, row]
    if routed:
        in_specs.append(pl.BlockSpec(router.shape, lambda bi, i: (0, 0)))
        args.append(router)
        out_shape.append(jax.ShapeDtypeStruct((b, l, LANES), F32))
        out_specs.append(pl.BlockSpec((1, tm, LANES), lambda bi, i: (bi, i, 0)))
    return pl.pallas_call(
        functools.partial(_out_proj_kernel, n_o=len(o_list), routed=routed),
        out_shape=out_shape,
        grid=(b, l // tm),
        in_specs=in_specs,
        out_specs=out_specs,
        compiler_params=_cparams(("parallel", "parallel")),
        name="out_proj",
    )(*args)


def _swiglu_step(hb, w1_ref, w3_ref, w2_ref):
    out = None
    for c0 in range(0, w1_ref.shape[-1], MXU_WIDTH):
        cols = slice(c0, c0 + MXU_WIDTH)
        a = jnp.dot(hb, w1_ref[:, cols], preferred_element_type=F32)
        g = jnp.dot(hb, w3_ref[:, cols], preferred_element_type=F32)
        act = (a / (1.0 + jnp.exp(-a))) * g
        part = jnp.dot(act.astype(BF16), w2_ref[cols, :], preferred_element_type=F32)
        out = part if out is None else out + part
    return out


def _ffn_dense_kernel(h_ref, w1_ref, w3_ref, w2_ref, x_ref, gpost_ref, gt_ref, o_ref, acc_ref):
    f = pl.program_id(2)

    @pl.when(f == 0)
    def _():
        acc_ref[...] = jnp.zeros_like(acc_ref)

    acc_ref[...] += _swiglu_step(h_ref[0], w1_ref, w3_ref, w2_ref)

    @pl.when(f == pl.num_programs(2) - 1)
    def _():
        o_ref[0] = x_ref[0] + gt_ref[0] * (_rms(acc_ref[...]) * gpost_ref[...])


def _ff_tile(dff):
    return _pick(dff, (512, 256))


def _ffn_dense(h2, w1, w3, w2, x1, gpost, gt):
    b, l, d = x1.shape
    dff = w1.shape[1]
    tm = _pick(l, (512, 256, 128))
    tf = _ff_tile(dff)
    row = pl.BlockSpec((1, tm, d), lambda bi, i, f: (bi, i, 0))
    return pl.pallas_call(
        _ffn_dense_kernel,
        out_shape=jax.ShapeDtypeStruct((b, l, d), F32),
        grid=(b, l // tm, dff // tf),
        in_specs=[row,
                  pl.BlockSpec((d, tf), lambda bi, i, f: (0, f)),
                  pl.BlockSpec((d, tf), lambda bi, i, f: (0, f)),
                  pl.BlockSpec((tf, d), lambda bi, i, f: (f, 0)),
                  row,
                  pl.BlockSpec((1, d), lambda bi, i, f: (0, 0)),
                  pl.BlockSpec((1, 1, d), lambda bi, i, f: (bi, 0, 0))],
        out_specs=row,
        scratch_shapes=[pltpu.VMEM((tm, d), F32)],
        compiler_params=_cparams(("parallel", "parallel", "arbitrary")),
        name="ffn_dense",
    )(h2, w1, w3, w2, x1, gpost, gt)


def _ffn_expert_kernel(tile_e_ref, nact_ref, src0_ref, srcn_ref, h_hbm, w1_ref, w3_ref, w2_ref,
                       o_ref, xbuf_ref, hb_ref, acc_ref, sem, *, tm):
    j = pl.program_id(0)
    f = pl.program_id(1)
    nact = nact_ref[0]
    slot = j % 2

    def gather(idx_ref, dst_slot):
        def issue(i, _):
            pltpu.make_async_copy(h_hbm.at[pl.ds(idx_ref[i], 1)],
                                  xbuf_ref.at[dst_slot, pl.ds(i, 1)], sem.at[dst_slot]).start()
            return 0
        lax.fori_loop(0, tm, issue, 0, unroll=8)

    @pl.when(j < nact)
    def _():
        @pl.when(f == 0)
        def _():
            @pl.when(j == 0)
            def _():
                gather(src0_ref, 0)

            pltpu.make_async_copy(h_hbm.at[pl.ds(0, tm)], xbuf_ref.at[slot], sem.at[slot]).wait()
            hb_ref[...] = xbuf_ref[slot].astype(BF16)
            acc_ref[...] = jnp.zeros_like(acc_ref)

        @pl.when((f == 1) & (j + 1 < nact))
        def _():
            gather(srcn_ref, 1 - slot)

        acc_ref[...] += _swiglu_step(hb_ref[...], w1_ref.at[0], w3_ref.at[0], w2_ref.at[0])

        @pl.when(f == pl.num_programs(1) - 1)
        def _():
            o_ref[...] = acc_ref[...]

    @pl.when((j >= nact_ref[0]) & (f == pl.num_programs(1) - 1))
    def _():
        o_ref[...] = jnp.zeros_like(o_ref)


def _ffn_expert(h2, src, w1, w3, w2, tile_e, nact, tm):
    d = h2.shape[1]
    p = src.shape[0]
    n_tiles = p // tm
    tf = _ff_tile(w1.shape[2])
    nf = w1.shape[2] // tf
    assert nf >= 2

    def fsel(j, f, na):
        return jnp.where(j < na[0], f, nf - 1)

    return pl.pallas_call(
        functools.partial(_ffn_expert_kernel, tm=tm),
        out_shape=jax.ShapeDtypeStruct((p, d), F32),
        grid_spec=pltpu.PrefetchScalarGridSpec(
            num_scalar_prefetch=2,
            grid=(n_tiles, nf),
            in_specs=[pl.BlockSpec((tm,), lambda j, f, te, na: (0,), memory_space=pltpu.SMEM),
                      pl.BlockSpec((tm,), lambda j, f, te, na: (jnp.minimum(j + 1, n_tiles - 1),),
                                   memory_space=pltpu.SMEM),
                      pl.BlockSpec(memory_space=pl.ANY),
                      pl.BlockSpec((1, d, tf), lambda j, f, te, na: (te[j], 0, fsel(j, f, na))),
                      pl.BlockSpec((1, d, tf), lambda j, f, te, na: (te[j], 0, fsel(j, f, na))),
                      pl.BlockSpec((1, tf, d), lambda j, f, te, na: (te[j], fsel(j, f, na), 0))],
            out_specs=pl.BlockSpec((tm, d), lambda j, f, te, na: (j, 0)),
            scratch_shapes=[pltpu.VMEM((2, tm, d), F32), pltpu.VMEM((tm, d), BF16),
                            pltpu.VMEM((tm, d), F32), pltpu.SemaphoreType.DMA((2,))]),
        compiler_params=_cparams(("arbitrary", "arbitrary")),
        name="ffn_expert",
    )(tile_e, nact, src, src, h2, w1, w3, w2)


def _rank_kernel(route_ref, rank_ref, cnt_ref, carry_ref):
    i = pl.program_id(0)

    @pl.when(i == 0)
    def _():
        carry_ref[...] = jnp.zeros_like(carry_ref)

    r = route_ref[...]
    t = r.shape[0]
    lane = lax.broadcasted_iota(jnp.int32, r.shape, 1)
    e1 = r[:, 0:1].astype(jnp.int32)
    e2 = r[:, 1:2].astype(jnp.int32)
    oh1 = (lane == e1).astype(F32)
    oh2 = (lane == e2).astype(F32)
    c = (oh1 + oh2).astype(BF16)
    row = lax.broadcasted_iota(jnp.int32, (t, t), 0)
    col = lax.broadcasted_iota(jnp.int32, (t, t), 1)
    tri = (col < row).astype(F32).astype(BF16)
    prefix = jnp.dot(tri, c, preferred_element_type=F32) + carry_ref[0:1, :]
    r1 = jnp.sum(prefix * oh1, axis=-1, keepdims=True)
    r2 = jnp.sum(prefix * oh2, axis=-1, keepdims=True)
    rank_ref[...] = jnp.where(lane == 0, r1, jnp.where(lane == 1, r2, 0.0))
    total = carry_ref[0:1, :] + jnp.sum(oh1 + oh2, axis=0, keepdims=True)
    carry_ref[...] = jnp.broadcast_to(total, carry_ref.shape)
    cnt_ref[...] = jnp.broadcast_to(total, cnt_ref.shape)


def _expert_ranks(route):
    n = route.shape[0]
    t = _pick(n, (512, 256, 128))
    return pl.pallas_call(
        _rank_kernel,
        out_shape=[jax.ShapeDtypeStruct((n, LANES), F32), jax.ShapeDtypeStruct((8, LANES), F32)],
        grid=(n // t,),
        in_specs=[pl.BlockSpec((t, LANES), lambda i: (i, 0))],
        out_specs=[pl.BlockSpec((t, LANES), lambda i: (i, 0)),
                   pl.BlockSpec((8, LANES), lambda i: (0, 0))],
        scratch_shapes=[pltpu.VMEM((8, LANES), F32)],
        compiler_params=_cparams(("arbitrary",)),
        name="expert_ranks",
    )(route)


def _combine_kernel(pos0_ref, posn_ref, y_hbm, route_ref, x_ref, gpost_ref, gt_ref, o_ref,
                    buf_ref, sem, *, tt):
    n_steps = pl.num_programs(0) * pl.num_programs(1)
    step = pl.program_id(0) * pl.num_programs(1) + pl.program_id(1)
    slot = step % 2

    def gather(p_ref, dst_slot):
        def issue(i, _):
            for k in range(TOP_K):
                pltpu.make_async_copy(y_hbm.at[pl.ds(p_ref[TOP_K * i + k], 1)],
                                      buf_ref.at[dst_slot, pl.ds(k * tt + i, 1)],
                                      sem.at[dst_slot]).start()
            return 0
        lax.fori_loop(0, tt, issue, 0, unroll=4)

    @pl.when(step == 0)
    def _():
        gather(pos0_ref, 0)

    @pl.when(step + 1 < n_steps)
    def _():
        gather(posn_ref, 1 - slot)

    pltpu.make_async_copy(y_hbm.at[pl.ds(0, TOP_K * tt)], buf_ref.at[slot], sem.at[slot]).wait()
    r = route_ref[0]
    y = r[:, 2:3] * buf_ref[slot, 0:tt, :] + r[:, 3:4] * buf_ref[slot, tt:2 * tt, :]
    o_ref[0] = x_ref[0] + gt_ref[0] * (_rms(y) * gpost_ref[...])


def _combine(y, pos_flat, route, x1, gpost, gt):
    b, l, d = x1.shape
    tt = _pick(l, (256, 128))
    nt = l // tt
    n_steps = b * nt
    row = pl.BlockSpec((1, tt, d), lambda bi, i: (bi, i, 0))
    return pl.pallas_call(
        functools.partial(_combine_kernel, tt=tt),
        out_shape=jax.ShapeDtypeStruct((b, l, d), F32),
        grid=(b, nt),
        in_specs=[pl.BlockSpec((TOP_K * tt,), lambda bi, i: (0,), memory_space=pltpu.SMEM),
                  pl.BlockSpec((TOP_K * tt,),
                               lambda bi, i: (jnp.minimum(bi * nt + i + 1, n_steps - 1),),
                               memory_space=pltpu.SMEM),
                  pl.BlockSpec(memory_space=pl.ANY),
                  pl.BlockSpec((1, tt, LANES), lambda bi, i: (bi, i, 0)),
                  row,
                  pl.BlockSpec((1, d), lambda bi, i: (0, 0)),
                  pl.BlockSpec((1, 1, d), lambda bi, i: (bi, 0, 0))],
        out_specs=row,
        scratch_shapes=[pltpu.VMEM((2, TOP_K * tt, d), F32), pltpu.SemaphoreType.DMA((2,))],
        compiler_params=_cparams(("arbitrary", "arbitrary")),
        name="moe_combine",
    )(pos_flat, pos_flat, y, route, x1, gpost, gt)


def _moe_ffn(h2, route, w1, w3, w2, x1, gpost, gt):
    b, l, d = x1.shape
    n = b * l
    tm = 512
    route2 = route.reshape(n, LANES)
    rank, counts = _expert_ranks(route2)
    cnt = counts[0, :N_EXPERTS].astype(jnp.int32)
    tiles = (cnt + tm - 1) // tm
    tile_end = jnp.cumsum(tiles)
    row_off = (tile_end - tiles) * tm
    e_idx = route2[:, :TOP_K].astype(jnp.int32)
    pos = (row_off[e_idx] + rank[:, :TOP_K].astype(jnp.int32)).reshape(-1)
    n_tiles = (n * TOP_K) // tm + N_EXPERTS
    nact = tile_end[-1:]
    tile_ids = jnp.minimum(jnp.arange(n_tiles, dtype=jnp.int32), nact[0] - 1)
    tile_e = jnp.sum((tile_ids[:, None] >= tile_end[None, :]).astype(jnp.int32), axis=1)
    token = jnp.arange(n * TOP_K, dtype=jnp.int32) // TOP_K
    src = jnp.zeros((n_tiles * tm,), jnp.int32).at[pos].set(
        token, unique_indices=True, mode="promise_in_bounds")
    y = _ffn_expert(h2.reshape(n, d), src, w1, w3, w2, tile_e, nact.astype(jnp.int32), tm)
    return _combine(y, pos, route, x1, gpost, gt)


def _rope_tables(l, dims_per_map):
    lane = jnp.arange(LANES)
    d = lane % dims_per_map
    part_w = dims_per_map // 2
    half = part_w // 2
    is_col = (d // part_w) == 1
    first = (d % part_w) < half
    inv_freq = ROPE_THETA ** (-jnp.arange(half, dtype=F32) / half)
    freq = inv_freq[d % half]
    pos = jnp.arange(l)
    p = jnp.where(is_col[None, :], (pos % GRID_W)[:, None], (pos // GRID_W)[:, None]).astype(F32)
    ang = p * freq[None, :]
    cos, sin = jnp.cos(ang), jnp.sin(ang)
    zero = jnp.zeros_like(sin)
    return cos, jnp.where(first[None, :], -sin, zero), jnp.where(first[None, :], zero, sin)


def _identity_tables(l):
    return jnp.ones((l, LANES), F32), jnp.zeros((l, LANES), F32), jnp.zeros((l, LANES), F32)


def kernel(x, c, ctx, c_ctx, w_mod, b_mod, g_mix_pre, g_mix_post, g_ffn_pre, g_ffn_post,
           w_in, w_out, na_rpb, gqa_q_norm, gqa_k_norm, diff_lambda_q1, diff_lambda_k1,
           diff_lambda_q2, diff_lambda_k2, diff_subln, ffn_w1, ffn_w3, ffn_w2,
           moe_router, moe_w1, moe_w3, moe_w2):
    b, l, d = x.shape
    cn = ctx.shape[1]
    depth = w_mod.shape[0]
    n_heads = d // HEAD_DIM
    n_na, n_gq, n_diff = n_heads // 4, n_heads // 2, n_heads // 4
    n_gkv = n_gq // 4
    group = n_gq // n_gkv
    dims = (n_na, n_gq, n_gkv, n_diff)

    rows_pad = -(-(b + 1) // 8) * 8
    cvec = jnp.zeros((rows_pad, d), F32).at[:b].set(c).at[b].set(c_ctx)
    mods = _modulation(cvec, w_mod, b_mod)
    bias_tab = _na_bias_table(na_rpb.reshape((depth * n_na,) + na_rpb.shape[2:]))
    bias_tab = bias_tab.reshape((depth, n_na) + bias_tab.shape[1:])

    tabs_g = _rope_tables(l, HEAD_DIM)
    tabs_d = _rope_tables(l, DIFF_QK_DIM)
    tabs_id = _identity_tables(cn)

    x_ctx = ctx
    moe_bf16 = {}
    for li in range(depth):
        last = li == depth - 1
        lam_init = 0.8 - 0.6 * math.exp(-0.3 * li)
        m6 = mods[li].reshape(rows_pad, 6, d)
        sh1, sc1, gt1, sh2, sc2, gt2 = [m6[:b, k][:, None, :] for k in range(6)]
        csh1, csc1, cgt1, csh2, csc2, cgt2 = [
            jnp.broadcast_to(m6[b, k][None, None, :], (b, 1, d)) for k in range(6)]
        vec = lambda a: a[li].reshape(1, -1)
        w_in_b = w_in[li].astype(BF16)
        w_out_b = w_out[li].astype(BF16)
        qn, kn = vec(gqa_q_norm), vec(gqa_k_norm)
        diff_args = (vec(diff_lambda_q1), vec(diff_lambda_k1), vec(diff_lambda_q2),
                     vec(diff_lambda_k2), vec(diff_subln), lam_init)

        (naq, nak, nav, gq, gk, gv, dq, dk, dv) = _in_proj(
            x, vec(g_mix_pre), sh1, sc1, w_in_b, qn, kn, tabs_g, tabs_d, dims)
        (naqc, nakc, navc, gqc, gkc, gvc, dqc, dkc, dvc) = _in_proj(
            x_ctx, vec(g_mix_pre), csh1, csc1, w_in_b, qn, kn, tabs_id, tabs_id, dims)

        oa = _na_attention(naq, nak, nav, nakc, navc, bias_tab[li])
        tq_g, tq_d = _pick(l, (256, 128)), _pick(l, (256, 128))
        casts_g, casts_d = [], []
        if (li + 1) < depth and (li + 1) % 2 == 1:
            mi = (li + 1) // 2
            e_, d_, f_ = moe_w1.shape[1:]
            flat = [moe_w1[mi].reshape(e_ * d_, f_), moe_w3[mi].reshape(e_ * d_, f_),
                    moe_w2[mi].reshape(e_ * f_, d_)]
            steps_g, steps_d = b * n_gkv * (l // tq_g), b * n_diff * (l // tq_d)
            if _cast_rows(flat[0], steps_g) and _cast_rows(flat[1], steps_d) and _cast_rows(flat[2], steps_d):
                casts_g, casts_d = flat[:1], flat[1:]
        ob = _attention(gq, [(gk, gv), (gkc, gvc)], group=group, tq=tq_g, casts=casts_g)
        oc = _attention(dq, [(dk, dv), (dkc, dvc)], group=2, tq=tq_d, diff=diff_args, casts=casts_d)
        if casts_g:
            ob, w1b = ob
            oc, w3b, w2b = oc
            moe_bf16[li + 1] = (w1b.reshape(e_, d_, f_), w3b.reshape(e_, d_, f_), w2b.reshape(e_, f_, d_))

        routed = li % 2 == 1
        if routed:
            router = jnp.zeros((d, LANES), F32).at[:, :N_EXPERTS].set(moe_router[li // 2])
            x1, h2, route = _out_proj([oa, ob, oc], w_out_b, x, vec(g_mix_post), gt1,
                                      vec(g_ffn_pre), sh2, sc2, router=router)
        else:
            x1, h2 = _out_proj([oa, ob, oc], w_out_b, x, vec(g_mix_post), gt1,
                               vec(g_ffn_pre), sh2, sc2)

        if not last:
            tqc = _pick(cn, (256, 128))
            oac = _attention(naqc, [(nakc, navc)], group=1, tq=tqc)
            obc = _attention(gqc, [(gkc, gvc)], group=group, tq=_pick(cn, (128,)))
            occ = _attention(dqc, [(dkc, dvc)], group=2, tq=tqc, diff=diff_args)
            if routed:
                router = jnp.zeros((d, LANES), F32).at[:, :N_EXPERTS].set(moe_router[li // 2])
                xc1, h2c, route_c = _out_proj([oac, obc, occ], w_out_b, x_ctx, vec(g_mix_post),
                                              cgt1, vec(g_ffn_pre), csh2, csc2, router=router)
            else:
                xc1, h2c = _out_proj([oac, obc, occ], w_out_b, x_ctx, vec(g_mix_post), cgt1,
                                     vec(g_ffn_pre), csh2, csc2)

        if routed:
            if li in moe_bf16:
                w1, w3, w2 = moe_bf16.pop(li)
            else:
                w1, w3, w2 = [w[li // 2].astype(BF16) for w in (moe_w1, moe_w3, moe_w2)]
            x = _moe_ffn(h2, route, w1, w3, w2, x1, vec(g_ffn_post), gt2)
            if not last:
                x_ctx = _moe_ffn(h2c, route_c, w1, w3, w2, xc1, vec(g_ffn_post), cgt2)
        else:
            w1 = ffn_w1[li // 2].astype(BF16)
            w3 = ffn_w3[li // 2].astype(BF16)
            w2 = ffn_w2[li // 2].astype(BF16)
            x = _ffn_dense(h2, w1, w3, w2, x1, vec(g_ffn_post), gt2)
            if not last:
                x_ctx = _ffn_dense(h2c, w1, w3, w2, xc1, vec(g_ffn_post), cgt2)
    return x
```

```python
import functools
import math

import jax
import jax.numpy as jnp
from jax import lax
from jax.experimental import pallas as pl
from jax.experimental.pallas import tpu as pltpu

GRID_W = 64
HEAD_DIM = 128
DIFF_QK_DIM = HEAD_DIM // 2
NA_ROWS = 8
NA_COLS = 16
ROPE_THETA = 10000.0
N_EXPERTS = 8
TOP_K = 2
EPS = 1e-6

LANES = 128
MXU_WIDTH = 256
NEG_BIG = -1e30
LOG2E = math.log2(math.e)
VMEM_LIMIT = 56 * 1024 * 1024

BF16 = jnp.bfloat16
F32 = jnp.float32
NT_DIMS = (((1,), (1,)), ((), ()))


def _cparams(sem):
    return pltpu.CompilerParams(dimension_semantics=sem, vmem_limit_bytes=VMEM_LIMIT)


def _pick(n, cands):
    for c in cands:
        if n % c == 0:
            return c
    raise ValueError(f"no tile in {cands} divides {n}")


def _rms(y):
    return y * lax.rsqrt(jnp.mean(y * y, axis=-1, keepdims=True) + EPS)


def _mod_kernel(c_ref, w_ref, b_ref, o_ref):
    c = c_ref[...]
    s = c / (1.0 + jnp.exp(-c))
    w = w_ref[0]
    s_hi = s.astype(BF16)
    s_lo = (s - s_hi.astype(F32)).astype(BF16)
    w_hi = w.astype(BF16)
    w_lo = (w - w_hi.astype(F32)).astype(BF16)
    o_ref[0] = (jnp.dot(s_hi, w_hi, preferred_element_type=F32)
                + jnp.dot(s_hi, w_lo, preferred_element_type=F32)
                + jnp.dot(s_lo, w_hi, preferred_element_type=F32)) + b_ref[0]


def _modulation(cvec, w_mod, b_mod):
    depth, d, n = w_mod.shape
    r = cvec.shape[0]
    tn = _pick(n, (1024, 512, 256, 128))
    return pl.pallas_call(
        _mod_kernel,
        out_shape=jax.ShapeDtypeStruct((depth, r, n), F32),
        grid=(depth, n // tn),
        in_specs=[pl.BlockSpec((r, d), lambda l, j: (0, 0)),
                  pl.BlockSpec((1, d, tn), lambda l, j: (l, 0, j)),
                  pl.BlockSpec((1, 1, tn), lambda l, j: (l, 0, j))],
        out_specs=pl.BlockSpec((1, r, tn), lambda l, j: (l, 0, j)),
        compiler_params=_cparams(("parallel", "parallel")),
        name="adaln_mod",
    )(cvec, w_mod, b_mod.reshape(depth, 1, n))


def _na_bias_kernel(rpb_ref, o_ref):
    n_dc = 2 * NA_COLS - 1
    base = pl.program_id(0) * ((2 * NA_ROWS - 1) * n_dc)
    shape = (GRID_W, LANES)
    qc = lax.broadcasted_iota(jnp.int32, shape, 0)
    lane = lax.broadcasted_iota(jnp.int32, shape, 1)
    half = lane >> 6
    kc = lane & (GRID_W - 1)
    dc = jnp.clip(kc - qc, -(NA_COLS - 1), NA_COLS - 1) + (NA_COLS - 1)
    cs = jnp.clip(qc - NA_COLS // 2, 0, GRID_W - NA_COLS)
    ok = (kc >= cs) & (kc < cs + NA_COLS)
    code = jnp.where(ok, half * 32 + dc, -1)
    for d in range(2 * NA_ROWS - 2):
        acc = jnp.full(shape, NEG_BIG, F32)
        for hf in range(2):
            for j in range(n_dc):
                val = rpb_ref[base + (d + hf) * n_dc + j] * LOG2E
                acc = jnp.where(code == hf * 32 + j, val, acc)
        o_ref[0, d] = acc


def _na_bias_table(rpb):
    nh = rpb.shape[0]
    return pl.pallas_call(
        _na_bias_kernel,
        out_shape=jax.ShapeDtypeStruct((nh, 2 * NA_ROWS - 2, GRID_W, LANES), F32),
        grid=(nh,),
        in_specs=[pl.BlockSpec(memory_space=pltpu.SMEM)],
        out_specs=pl.BlockSpec((1, 2 * NA_ROWS - 2, GRID_W, LANES), lambda h: (h, 0, 0, 0)),
        compiler_params=_cparams(("parallel",)),
        name="na_bias_table",
    )(rpb.reshape(-1))


def _rope(y, cos, sin_hi, sin_lo, dist):
    return (y * cos + pltpu.roll(y, LANES - dist, 1) * sin_hi
            + pltpu.roll(y, dist, 1) * sin_lo)


def _in_proj_kernel(x_ref, g_ref, sh_ref, sc_ref, w_ref, qn_ref, kn_ref,
                    cg_ref, sga_ref, sgb_ref, cd_ref, sda_ref, sdb_ref,
                    naq_ref, nak_ref, nav_ref, gq_ref, gk_ref, gv_ref,
                    dq_ref, dk_ref, dv_ref, *, n_na, n_gq, n_gkv, n_diff):
    x = x_ref[0]
    h = _rms(x) * g_ref[...]
    h = h * (1.0 + sc_ref[0]) + sh_ref[0]
    hb = h.astype(BF16)

    def proj(c0, nheads):
        return jnp.dot(hb, w_ref[:, c0:c0 + nheads * HEAD_DIM], preferred_element_type=F32)

    def heads(y, nheads):
        return [y[:, i * HEAD_DIM:(i + 1) * HEAD_DIM] for i in range(nheads)]

    na_scale = HEAD_DIM ** -0.5 * LOG2E
    diff_scale = DIFF_QK_DIM ** -0.5 * LOG2E
    c0 = 0
    for i, y in enumerate(heads(proj(c0, n_na), n_na)):
        naq_ref[0, i] = (y * na_scale).astype(BF16)
    c0 += n_na * HEAD_DIM
    for i, y in enumerate(heads(proj(c0, n_na), n_na)):
        nak_ref[0, i] = y.astype(BF16)
    c0 += n_na * HEAD_DIM
    lane = lax.broadcasted_iota(jnp.int32, (x.shape[0], LANES), 1)
    ones_col = jnp.where(lane == 0, 1.0, 0.0).astype(BF16)
    for i, y in enumerate(heads(proj(c0, n_na), n_na)):
        nav_ref[0, i, :, :HEAD_DIM] = y.astype(BF16)
        nav_ref[0, i, :, HEAD_DIM:] = ones_col
    c0 += n_na * HEAD_DIM

    cg, sga, sgb = cg_ref[...], sga_ref[...], sgb_ref[...]
    for i, y in enumerate(heads(proj(c0, n_gq), n_gq)):
        y = _rope(_rms(y) * qn_ref[...], cg, sga, sgb, 32)
        gq_ref[0, i] = (y * na_scale).astype(BF16)
    c0 += n_gq * HEAD_DIM
    for i, y in enumerate(heads(proj(c0, n_gkv), n_gkv)):
        y = _rope(_rms(y) * kn_ref[...], cg, sga, sgb, 32)
        gk_ref[0, i] = y.astype(BF16)
    c0 += n_gkv * HEAD_DIM
    for i, y in enumerate(heads(proj(c0, n_gkv), n_gkv)):
        gv_ref[0, i, :, :HEAD_DIM] = y.astype(BF16)
        gv_ref[0, i, :, HEAD_DIM:] = ones_col
    c0 += n_gkv * HEAD_DIM

    cd, sda, sdb = cd_ref[...], sda_ref[...], sdb_ref[...]
    for i, y in enumerate(heads(proj(c0, n_diff), n_diff)):
        y = _rope(y, cd, sda, sdb, 16) * diff_scale
        dq_ref[0, 2 * i] = jnp.where(lane < DIFF_QK_DIM, y, 0.0).astype(BF16)
        dq_ref[0, 2 * i + 1] = jnp.where(lane >= DIFF_QK_DIM, y, 0.0).astype(BF16)
    c0 += n_diff * HEAD_DIM
    for i, y in enumerate(heads(proj(c0, n_diff), n_diff)):
        dk_ref[0, i] = _rope(y, cd, sda, sdb, 16).astype(BF16)
    c0 += n_diff * HEAD_DIM
    for i, y in enumerate(heads(proj(c0, n_diff), n_diff)):
        dv_ref[0, i, :, :HEAD_DIM] = y.astype(BF16)
        dv_ref[0, i, :, HEAD_DIM:] = ones_col


def _in_proj(x, g, sh, sc, w_in, qn, kn, tabs_g, tabs_d, dims):
    b, l, d = x.shape
    n_na, n_gq, n_gkv, n_diff = dims
    n_in = w_in.shape[1]
    tm = _pick(l, (256, 128))
    vec = pl.BlockSpec((1, d), lambda bi, i: (0, 0))
    mod = pl.BlockSpec((1, 1, d), lambda bi, i: (bi, 0, 0))
    hvec = pl.BlockSpec((1, HEAD_DIM), lambda bi, i: (0, 0))
    tab = pl.BlockSpec((tm, LANES), lambda bi, i: (i, 0))

    def out(nh, width=HEAD_DIM):
        return (jax.ShapeDtypeStruct((b, nh, l, width), BF16),
                pl.BlockSpec((1, nh, tm, width), lambda bi, i: (bi, 0, i, 0)))

    outs = [out(n_na), out(n_na), out(n_na, 2 * HEAD_DIM), out(n_gq), out(n_gkv),
            out(n_gkv, 2 * HEAD_DIM), out(2 * n_diff), out(n_diff), out(n_diff, 2 * HEAD_DIM)]
    kern = functools.partial(_in_proj_kernel, n_na=n_na, n_gq=n_gq, n_gkv=n_gkv, n_diff=n_diff)
    return pl.pallas_call(
        kern,
        out_shape=[o[0] for o in outs],
        grid=(b, l // tm),
        in_specs=[pl.BlockSpec((1, tm, d), lambda bi, i: (bi, i, 0)), vec, mod, mod,
                  pl.BlockSpec((d, n_in), lambda bi, i: (0, 0), pipeline_mode=pl.Buffered(1)),
                  hvec, hvec, tab, tab, tab, tab, tab, tab],
        out_specs=[o[1] for o in outs],
        compiler_params=_cparams(("parallel", "parallel")),
        name="in_proj",
    )(x, g, sh, sc, w_in, qn, kn, *tabs_g, *tabs_d)


def _attn_kernel(*refs, n_seg, group, tq, chunks, diff, lam_init, n_cast):
    q_ref = refs[0]
    kv_refs = refs[1:1 + 2 * n_seg]
    rest = list(refs[1 + 2 * n_seg:])
    cast_out = [rest.pop() for _ in range(n_cast)][::-1]
    o_ref = rest.pop()
    cast_in = [rest.pop() for _ in range(n_cast)][::-1]
    for src_ref, dst_ref in zip(cast_in, cast_out):
        dst_ref[...] = src_ref[...].astype(BF16)
    if diff:
        lq1_ref, lk1_ref, lq2_ref, lk2_ref, sub_ref = rest
    m_rows = group * tq

    def step(k, v, carry):
        m, acc = carry
        q = q_ref[0].reshape(m_rows, HEAD_DIM)
        s = lax.dot_general(q, k, NT_DIMS, preferred_element_type=F32)
        m_new = jnp.maximum(m, jnp.max(s, axis=-1, keepdims=True))
        p = jnp.exp2((s - m_new).astype(BF16))
        acc = jnp.exp2(m - m_new) * acc + jnp.dot(p, v, preferred_element_type=F32)
        return m_new, acc

    carry = (jnp.full((m_rows, 1), NEG_BIG, F32), jnp.zeros((m_rows, 2 * HEAD_DIM), F32))
    for si in range(n_seg):
        k_ref, v_ref = kv_refs[2 * si], kv_refs[2 * si + 1]
        n_chunks, tk = chunks[si]
        for c in range(n_chunks):
            rows = slice(c * tk, (c + 1) * tk)
            carry = step(k_ref[0, 0, rows, :], v_ref[0, 0, rows, :], carry)
    _, acc = carry
    o = acc[:, :HEAD_DIM] / acc[:, HEAD_DIM:HEAD_DIM + 1]
    if diff:
        lam = (jnp.exp(jnp.sum(lq1_ref[...] * lk1_ref[...], axis=-1, keepdims=True))
               - jnp.exp(jnp.sum(lq2_ref[...] * lk2_ref[...], axis=-1, keepdims=True)) + lam_init)
        od = o[:tq] - lam * o[tq:]
        o_ref[0] = (_rms(od) * sub_ref[...] * (1.0 - lam_init)).astype(BF16)
    else:
        for g in range(group):
            o_ref[0, :, g * HEAD_DIM:(g + 1) * HEAD_DIM] = o[g * tq:(g + 1) * tq].astype(BF16)


def _cast_rows(a, n_steps):
    rows = a.shape[0] // n_steps
    return rows if rows * n_steps == a.shape[0] and rows % 16 == 0 else 0


def _attention(q, segs, *, group, tq, diff=None, casts=()):
    b, hq, lq, _ = q.shape
    hkv = hq // group
    nq = lq // tq
    out_heads = 1 if diff is not None else group
    kv_args, kv_specs, chunks = [], [], []
    for k, v in segs:
        s = k.shape[2]
        tk = _pick(s, (512, 256, 128) if diff is not None else (256, 128))
        chunks.append((s // tk, tk))
        kv_args += [k, v]
        kv_specs += [pl.BlockSpec((1, 1, s, a.shape[3]), lambda bi, h, i: (bi, h, 0, 0))
                     for a in (k, v)]
    extra_args, extra_specs, lam_init = [], [], 0.0
    if diff is not None:
        lq1, lk1, lq2, lk2, subln, lam_init = diff
        extra_args = [lq1, lk1, lq2, lk2, subln]
        extra_specs = [pl.BlockSpec((1, a.shape[1]), lambda bi, h, i: (0, 0)) for a in extra_args]
    cast_specs = [pl.BlockSpec((_cast_rows(a, b * hkv * nq), a.shape[1]),
                               lambda bi, h, i: ((bi * hkv + h) * nq + i, 0)) for a in casts]
    kern = functools.partial(_attn_kernel, n_seg=len(segs), group=group, tq=tq,
                             chunks=tuple(chunks), diff=diff is not None, lam_init=lam_init,
                             n_cast=len(casts))
    outs = pl.pallas_call(
        kern,
        out_shape=[jax.ShapeDtypeStruct((b, lq, hkv * out_heads * HEAD_DIM), BF16)]
                  + [jax.ShapeDtypeStruct(a.shape, BF16) for a in casts],
        grid=(b, hkv, nq),
        in_specs=[pl.BlockSpec((1, group, tq, HEAD_DIM), lambda bi, h, i: (bi, h, i, 0))]
                 + kv_specs + extra_specs + cast_specs,
        out_specs=[pl.BlockSpec((1, tq, out_heads * HEAD_DIM), lambda bi, h, i: (bi, i, h))] + cast_specs,
        compiler_params=_cparams(("parallel", "parallel", "parallel")),
        name="diff_attn" if diff is not None else "gqa_attn",
    )(q, *kv_args, *extra_args, *casts)
    return outs[0] if not casts else outs


def _na_kernel(q_ref, k_ref, v_ref, kc_ref, vc_ref, bias_ref, o_ref, *, rows):
    band = NA_ROWS * GRID_W
    kc = kc_ref[0, 0]
    vc = vc_ref[0, 0]

    def body(r, _):
        rs = jnp.clip(r - NA_ROWS // 2, 0, rows - NA_ROWS)
        off = rs - r + (NA_ROWS - 1)
        qrows = pl.ds(pl.multiple_of(r * GRID_W, GRID_W), GRID_W)
        krows = pl.ds(pl.multiple_of(rs * GRID_W, GRID_W), band)
        q = q_ref[0, 0, qrows, :]
        bias = jnp.concatenate([bias_ref[0, off + 2 * j] for j in range(NA_ROWS // 2)], axis=1)
        s = lax.dot_general(q, k_ref[0, 0, krows, :], NT_DIMS, preferred_element_type=F32) + bias
        sc = lax.dot_general(q, kc, NT_DIMS, preferred_element_type=F32)
        m = jnp.maximum(jnp.max(s, axis=-1, keepdims=True), jnp.max(sc, axis=-1, keepdims=True))
        p = jnp.exp2((s - m).astype(BF16))
        pc = jnp.exp2((sc - m).astype(BF16))
        o = (jnp.dot(p, v_ref[0, 0, krows, :], preferred_element_type=F32)
             + jnp.dot(pc, vc, preferred_element_type=F32))
        o_ref[0, qrows, :] = (o[:, :HEAD_DIM] / o[:, HEAD_DIM:HEAD_DIM + 1]).astype(BF16)
        return 0

    lax.fori_loop(0, rows, body, 0, unroll=32)


def _na_attention(q, k, v, kc, vc, bias):
    b, nh, l, _ = q.shape
    rows = l // GRID_W

    def full(a):
        return pl.BlockSpec((1, 1) + a.shape[2:], lambda bi, h: (bi, h, 0, 0))

    return pl.pallas_call(
        functools.partial(_na_kernel, rows=rows),
        out_shape=jax.ShapeDtypeStruct((b, l, nh * HEAD_DIM), BF16),
        grid=(b, nh),
        in_specs=[full(q), full(k), full(v), full(kc), full(vc),
                  pl.BlockSpec((1,) + bias.shape[1:], lambda bi, h: (h, 0, 0, 0))],
        out_specs=pl.BlockSpec((1, l, HEAD_DIM), lambda bi, h: (bi, 0, h)),
        compiler_params=_cparams(("parallel", "parallel")),
        name="na_attn",
    )(q, k, v, kc, vc, bias)


def _out_proj_kernel(*refs, n_o, routed):
    o_refs = refs[:n_o]
    (w_ref, x_ref, gpost_ref, gt_ref, gpre_ref, sh_ref, sc_ref) = refs[n_o:n_o + 7]
    rest = refs[n_o + 7:]
    if routed:
        router_ref, x1_ref, h2_ref, route_ref = rest
    else:
        x1_ref, h2_ref = rest
    tm = x_ref.shape[1]
    sub = min(tm, 256)
    for r0 in range(0, tm, sub):
        rows = slice(r0, r0 + sub)
        acc = None
        c0 = 0
        for o_ref in o_refs:
            width = o_ref.shape[-1]
            part = jnp.dot(o_ref[0, rows, :], w_ref[c0:c0 + width, :], preferred_element_type=F32)
            acc = part if acc is None else acc + part
            c0 += width
        x1 = x_ref[0, rows, :] + gt_ref[0] * (_rms(acc) * gpost_ref[...])
        x1_ref[0, rows, :] = x1
        h2 = (_rms(x1) * gpre_ref[...]) * (1.0 + sc_ref[0]) + sh_ref[0]
        h2_ref[0, rows, :] = h2.astype(h2_ref.dtype)
        if routed:
            h_hi = h2.astype(BF16)
            h_lo = (h2 - h_hi.astype(F32)).astype(BF16)
            r = router_ref[...]
            r_hi = r.astype(BF16)
            r_lo = (r - r_hi.astype(F32)).astype(BF16)
            logits = (jnp.dot(h_hi, r_hi, preferred_element_type=F32)
                      + jnp.dot(h_hi, r_lo, preferred_element_type=F32)
                      + jnp.dot(h_lo, r_hi, preferred_element_type=F32))
            lane = lax.broadcasted_iota(jnp.int32, logits.shape, 1)
            logits = jnp.where(lane < N_EXPERTS, logits, NEG_BIG)
            v1 = jnp.max(logits, axis=-1, keepdims=True)
            i1 = jnp.min(jnp.where(logits == v1, lane, LANES), axis=-1, keepdims=True)
            rest_l = jnp.where(lane == i1, NEG_BIG, logits)
            v2 = jnp.max(rest_l, axis=-1, keepdims=True)
            i2 = jnp.min(jnp.where(rest_l == v2, lane, LANES), axis=-1, keepdims=True)
            e2 = jnp.exp(v2 - v1)
            g1 = 1.0 / (1.0 + e2)
            g2 = e2 / (1.0 + e2)
            route_ref[0, rows, :] = jnp.where(lane == 0, i1.astype(F32),
                                    jnp.where(lane == 1, i2.astype(F32),
                                    jnp.where(lane == 2, g1, jnp.where(lane == 3, g2, 0.0))))


def _out_proj(o_list, w_out, x, gpost, gt, gpre, sh, sc, router=None):
    b, l, d = x.shape
    tm = _pick(l, (512, 256, 128))
    routed = router is not None
    vec = pl.BlockSpec((1, d), lambda bi, i: (0, 0))
    mod = pl.BlockSpec((1, 1, d), lambda bi, i: (bi, 0, 0))
    row = pl.BlockSpec((1, tm, d), lambda bi, i: (bi, i, 0))
    in_specs = [pl.BlockSpec((1, tm, o.shape[-1]), lambda bi, i: (bi, i, 0)) for o in o_list]
    in_specs += [pl.BlockSpec(w_out.shape, lambda bi, i: (0, 0), pipeline_mode=pl.Buffered(1)),
                 row, vec, mod, vec, mod, mod]
    args = list(o_list) + [w_out, x, gpost, gt, gpre, sh, sc]
    out_shape = [jax.ShapeDtypeStruct((b, l, d), F32),
                 jax.ShapeDtypeStruct((b, l, d), F32 if routed else BF16)]
    out_specs = [row, row]
    if routed:
        in_specs.append(pl.BlockSpec(router.shape, lambda bi, i: (0, 0)))
        args.append(router)
        out_shape.append(jax.ShapeDtypeStruct((b, l, LANES), F32))
        out_specs.append(pl.BlockSpec((1, tm, LANES), lambda bi, i: (bi, i, 0)))
    return pl.pallas_call(
        functools.partial(_out_proj_kernel, n_o=len(o_list), routed=routed),
        out_shape=out_shape,
        grid=(b, l // tm),
        in_specs=in_specs,
        out_specs=out_specs,
        compiler_params=_cparams(("parallel", "parallel")),
        name="out_proj",
    )(*args)


def _swiglu_step(hb, w1_ref, w3_ref, w2_ref):
    out = None
    for c0 in range(0, w1_ref.shape[-1], MXU_WIDTH):
        cols = slice(c0, c0 + MXU_WIDTH)
        a = jnp.dot(hb, w1_ref[:, cols], preferred_element_type=F32)
        g = jnp.dot(hb, w3_ref[:, cols], preferred_element_type=F32)
        act = (a / (1.0 + jnp.exp(-a))) * g
        part = jnp.dot(act.astype(BF16), w2_ref[cols, :], preferred_element_type=F32)
        out = part if out is None else out + part
    return out


def _ffn_dense_kernel(h_ref, w1_ref, w3_ref, w2_ref, x_ref, gpost_ref, gt_ref, o_ref, acc_ref):
    f = pl.program_id(2)

    @pl.when(f == 0)
    def _():
        acc_ref[...] = jnp.zeros_like(acc_ref)

    acc_ref[...] += _swiglu_step(h_ref[0], w1_ref, w3_ref, w2_ref)

    @pl.when(f == pl.num_programs(2) - 1)
    def _():
        o_ref[0] = x_ref[0] + gt_ref[0] * (_rms(acc_ref[...]) * gpost_ref[...])


def _ff_tile(dff):
    return _pick(dff, (512, 256))


def _ffn_dense(h2, w1, w3, w2, x1, gpost, gt):
    b, l, d = x1.shape
    dff = w1.shape[1]
    tm = _pick(l, (512, 256, 128))
    tf = _ff_tile(dff)
    row = pl.BlockSpec((1, tm, d), lambda bi, i, f: (bi, i, 0))
    return pl.pallas_call(
        _ffn_dense_kernel,
        out_shape=jax.ShapeDtypeStruct((b, l, d), F32),
        grid=(b, l // tm, dff // tf),
        in_specs=[row,
                  pl.BlockSpec((d, tf), lambda bi, i, f: (0, f)),
                  pl.BlockSpec((d, tf), lambda bi, i, f: (0, f)),
                  pl.BlockSpec((tf, d), lambda bi, i, f: (f, 0)),
                  row,
                  pl.BlockSpec((1, d), lambda bi, i, f: (0, 0)),
                  pl.BlockSpec((1, 1, d), lambda bi, i, f: (bi, 0, 0))],
        out_specs=row,
        scratch_shapes=[pltpu.VMEM((tm, d), F32)],
        compiler_params=_cparams(("parallel", "parallel", "arbitrary")),
        name="ffn_dense",
    )(h2, w1, w3, w2, x1, gpost, gt)


def _ffn_expert_kernel(tile_e_ref, nact_ref, src0_ref, srcn_ref, h_hbm, w1_ref, w3_ref, w2_ref,
                       o_ref, xbuf_ref, hb_ref, acc_ref, sem, *, tm):
    j = pl.program_id(0)
    f = pl.program_id(1)
    nact = nact_ref[0]
    slot = j % 2

    def gather(idx_ref, dst_slot):
        def issue(i, _):
            pltpu.make_async_copy(h_hbm.at[pl.ds(idx_ref[i], 1)],
                                  xbuf_ref.at[dst_slot, pl.ds(i, 1)], sem.at[dst_slot]).start()
            return 0
        lax.fori_loop(0, tm, issue, 0, unroll=8)

    @pl.when(j < nact)
    def _():
        @pl.when(f == 0)
        def _():
            @pl.when(j == 0)
            def _():
                gather(src0_ref, 0)

            pltpu.make_async_copy(h_hbm.at[pl.ds(0, tm)], xbuf_ref.at[slot], sem.at[slot]).wait()
            hb_ref[...] = xbuf_ref[slot].astype(BF16)
            acc_ref[...] = jnp.zeros_like(acc_ref)

        @pl.when((f == 1) & (j + 1 < nact))
        def _():
            gather(srcn_ref, 1 - slot)

        acc_ref[...] += _swiglu_step(hb_ref[...], w1_ref.at[0], w3_ref.at[0], w2_ref.at[0])

        @pl.when(f == pl.num_programs(1) - 1)
        def _():
            o_ref[...] = acc_ref[...]

    @pl.when((j >= nact_ref[0]) & (f == pl.num_programs(1) - 1))
    def _():
        o_ref[...] = jnp.zeros_like(o_ref)


def _ffn_expert(h2, src, w1, w3, w2, tile_e, nact, tm):
    d = h2.shape[1]
    p = src.shape[0]
    n_tiles = p // tm
    tf = _ff_tile(w1.shape[2])
    nf = w1.shape[2] // tf
    assert nf >= 2

    def fsel(j, f, na):
        return jnp.where(j < na[0], f, nf - 1)

    return pl.pallas_call(
        functools.partial(_ffn_expert_kernel, tm=tm),
        out_shape=jax.ShapeDtypeStruct((p, d), F32),
        grid_spec=pltpu.PrefetchScalarGridSpec(
            num_scalar_prefetch=2,
            grid=(n_tiles, nf),
            in_specs=[pl.BlockSpec((tm,), lambda j, f, te, na: (0,), memory_space=pltpu.SMEM),
                      pl.BlockSpec((tm,), lambda j, f, te, na: (jnp.minimum(j + 1, n_tiles - 1),),
                                   memory_space=pltpu.SMEM),
                      pl.BlockSpec(memory_space=pl.ANY),
                      pl.BlockSpec((1, d, tf), lambda j, f, te, na: (te[j], 0, fsel(j, f, na))),
                      pl.BlockSpec((1, d, tf), lambda j, f, te, na: (te[j], 0, fsel(j, f, na))),
                      pl.BlockSpec((1, tf, d), lambda j, f, te, na: (te[j], fsel(j, f, na), 0))],
            out_specs=pl.BlockSpec((tm, d), lambda j, f, te, na: (j, 0)),
            scratch_shapes=[pltpu.VMEM((2, tm, d), F32), pltpu.VMEM((tm, d), BF16),
                            pltpu.VMEM((tm, d), F32), pltpu.SemaphoreType.DMA((2,))]),
        compiler_params=_cparams(("arbitrary", "arbitrary")),
        name="ffn_expert",
    )(tile_e, nact, src, src, h2, w1, w3, w2)


def _rank_kernel(route_ref, rank_ref, cnt_ref, carry_ref):
    i = pl.program_id(0)

    @pl.when(i == 0)
    def _():
        carry_ref[...] = jnp.zeros_like(carry_ref)

    r = route_ref[...]
    t = r.shape[0]
    lane = lax.broadcasted_iota(jnp.int32, r.shape, 1)
    e1 = r[:, 0:1].astype(jnp.int32)
    e2 = r[:, 1:2].astype(jnp.int32)
    oh1 = (lane == e1).astype(F32)
    oh2 = (lane == e2).astype(F32)
    c = (oh1 + oh2).astype(BF16)
    row = lax.broadcasted_iota(jnp.int32, (t, t), 0)
    col = lax.broadcasted_iota(jnp.int32, (t, t), 1)
    tri = (col < row).astype(F32).astype(BF16)
    prefix = jnp.dot(tri, c, preferred_element_type=F32) + carry_ref[0:1, :]
    r1 = jnp.sum(prefix * oh1, axis=-1, keepdims=True)
    r2 = jnp.sum(prefix * oh2, axis=-1, keepdims=True)
    rank_ref[...] = jnp.where(lane == 0, r1, jnp.where(lane == 1, r2, 0.0))
    total = carry_ref[0:1, :] + jnp.sum(oh1 + oh2, axis=0, keepdims=True)
    carry_ref[...] = jnp.broadcast_to(total, carry_ref.shape)
    cnt_ref[...] = jnp.broadcast_to(total, cnt_ref.shape)


def _expert_ranks(route):
    n = route.shape[0]
    t = _pick(n, (512, 256, 128))
    return pl.pallas_call(
        _rank_kernel,
        out_shape=[jax.ShapeDtypeStruct((n, LANES), F32), jax.ShapeDtypeStruct((8, LANES), F32)],
        grid=(n // t,),
        in_specs=[pl.BlockSpec((t, LANES), lambda i: (i, 0))],
        out_specs=[pl.BlockSpec((t, LANES), lambda i: (i, 0)),
                   pl.BlockSpec((8, LANES), lambda i: (0, 0))],
        scratch_shapes=[pltpu.VMEM((8, LANES), F32)],
        compiler_params=_cparams(("arbitrary",)),
        name="expert_ranks",
    )(route)


def _combine_kernel(pos0_ref, posn_ref, y_hbm, route_ref, x_ref, gpost_ref, gt_ref, o_ref,
                    buf_ref, sem, *, tt):
    n_steps = pl.num_programs(0) * pl.num_programs(1)
    step = pl.program_id(0) * pl.num_programs(1) + pl.program_id(1)
    slot = step % 2

    def gather(p_ref, dst_slot):
        def issue(i, _):
            for k in range(TOP_K):
                pltpu.make_async_copy(y_hbm.at[pl.ds(p_ref[TOP_K * i + k], 1)],
                                      buf_ref.at[dst_slot, pl.ds(k * tt + i, 1)],
                                      sem.at[dst_slot]).start()
            return 0
        lax.fori_loop(0, tt, issue, 0, unroll=4)

    @pl.when(step == 0)
    def _():
        gather(pos0_ref, 0)

    @pl.when(step + 1 < n_steps)
    def _():
        gather(posn_ref, 1 - slot)

    pltpu.make_async_copy(y_hbm.at[pl.ds(0, TOP_K * tt)], buf_ref.at[slot], sem.at[slot]).wait()
    r = route_ref[0]
    y = r[:, 2:3] * buf_ref[slot, 0:tt, :] + r[:, 3:4] * buf_ref[slot, tt:2 * tt, :]
    o_ref[0] = x_ref[0] + gt_ref[0] * (_rms(y) * gpost_ref[...])


def _combine(y, pos_flat, route, x1, gpost, gt):
    b, l, d = x1.shape
    tt = _pick(l, (256, 128))
    nt = l // tt
    n_steps = b * nt
    row = pl.BlockSpec((1, tt, d), lambda bi, i: (bi, i, 0))
    return pl.pallas_call(
        functools.partial(_combine_kernel, tt=tt),
        out_shape=jax.ShapeDtypeStruct((b, l, d), F32),
        grid=(b, nt),
        in_specs=[pl.BlockSpec((TOP_K * tt,), lambda bi, i: (0,), memory_space=pltpu.SMEM),
                  pl.BlockSpec((TOP_K * tt,),
                               lambda bi, i: (jnp.minimum(bi * nt + i + 1, n_steps - 1),),
                               memory_space=pltpu.SMEM),
                  pl.BlockSpec(memory_space=pl.ANY),
                  pl.BlockSpec((1, tt, LANES), lambda bi, i: (bi, i, 0)),
                  row,
                  pl.BlockSpec((1, d), lambda bi, i: (0, 0)),
                  pl.BlockSpec((1, 1, d), lambda bi, i: (bi, 0, 0))],
        out_specs=row,
        scratch_shapes=[pltpu.VMEM((2, TOP_K * tt, d), F32), pltpu.SemaphoreType.DMA((2,))],
        compiler_params=_cparams(("arbitrary", "arbitrary")),
        name="moe_combine",
    )(pos_flat, pos_flat, y, route, x1, gpost, gt)


def _moe_ffn(h2, route, w1, w3, w2, x1, gpost, gt):
    b, l, d = x1.shape
    n = b * l
    tm = 512
    route2 = route.reshape(n, LANES)
    rank, counts = _expert_ranks(route2)
    cnt = counts[0, :N_EXPERTS].astype(jnp.int32)
    tiles = (cnt + tm - 1) // tm
    tile_end = jnp.cumsum(tiles)
    row_off = (tile_end - tiles) * tm
    e_idx = route2[:, :TOP_K].astype(jnp.int32)
    pos = (row_off[e_idx] + rank[:, :TOP_K].astype(jnp.int32)).reshape(-1)
    n_tiles = (n * TOP_K) // tm + N_EXPERTS
    nact = tile_end[-1:]
    tile_ids = jnp.minimum(jnp.arange(n_tiles, dtype=jnp.int32), nact[0] - 1)
    tile_e = jnp.sum((tile_ids[:, None] >= tile_end[None, :]).astype(jnp.int32), axis=1)
    token = jnp.arange(n * TOP_K, dtype=jnp.int32) // TOP_K
    src = jnp.zeros((n_tiles * tm,), jnp.int32).at[pos].set(
        token, unique_indices=True, mode="promise_in_bounds")
    y = _ffn_expert(h2.reshape(n, d), src, w1, w3, w2, tile_e, nact.astype(jnp.int32), tm)
    return _combine(y, pos, route, x1, gpost, gt)


def _rope_tables(l, dims_per_map):
    lane = jnp.arange(LANES)
    d = lane % dims_per_map
    part_w = dims_per_map // 2
    half = part_w // 2
    is_col = (d // part_w) == 1
    first = (d % part_w) < half
    inv_freq = ROPE_THETA ** (-jnp.arange(half, dtype=F32) / half)
    freq = inv_freq[d % half]
    pos = jnp.arange(l)
    p = jnp.where(is_col[None, :], (pos % GRID_W)[:, None], (pos // GRID_W)[:, None]).astype(F32)
    ang = p * freq[None, :]
    cos, sin = jnp.cos(ang), jnp.sin(ang)
    zero = jnp.zeros_like(sin)
    return cos, jnp.where(first[None, :], -sin, zero), jnp.where(first[None, :], zero, sin)


def _identity_tables(l):
    return jnp.ones((l, LANES), F32), jnp.zeros((l, LANES), F32), jnp.zeros((l, LANES), F32)


def kernel(x, c, ctx, c_ctx, w_mod, b_mod, g_mix_pre, g_mix_post, g_ffn_pre, g_ffn_post,
           w_in, w_out, na_rpb, gqa_q_norm, gqa_k_norm, diff_lambda_q1, diff_lambda_k1,
           diff_lambda_q2, diff_lambda_k2, diff_subln, ffn_w1, ffn_w3, ffn_w2,
           moe_router, moe_w1, moe_w3, moe_w2):
    b, l, d = x.shape
    cn = ctx.shape[1]
    depth = w_mod.shape[0]
    n_heads = d // HEAD_DIM
    n_na, n_gq, n_diff = n_heads // 4, n_heads // 2, n_heads // 4
    n_gkv = n_gq // 4
    group = n_gq // n_gkv
    dims = (n_na, n_gq, n_gkv, n_diff)

    rows_pad = -(-(b + 1) // 8) * 8
    cvec = jnp.zeros((rows_pad, d), F32).at[:b].set(c).at[b].set(c_ctx)
    mods = _modulation(cvec, w_mod, b_mod)
    bias_tab = _na_bias_table(na_rpb.reshape((depth * n_na,) + na_rpb.shape[2:]))
    bias_tab = bias_tab.reshape((depth, n_na) + bias_tab.shape[1:])

    tabs_g = _rope_tables(l, HEAD_DIM)
    tabs_d = _rope_tables(l, DIFF_QK_DIM)
    tabs_id = _identity_tables(cn)

    x_ctx = ctx
    moe_bf16 = {}
    for li in range(depth):
        last = li == depth - 1
        lam_init = 0.8 - 0.6 * math.exp(-0.3 * li)
        m6 = mods[li].reshape(rows_pad, 6, d)
        sh1, sc1, gt1, sh2, sc2, gt2 = [m6[:b, k][:, None, :] for k in range(6)]
        csh1, csc1, cgt1, csh2, csc2, cgt2 = [
            jnp.broadcast_to(m6[b, k][None, None, :], (b, 1, d)) for k in range(6)]
        vec = lambda a: a[li].reshape(1, -1)
        w_in_b = w_in[li].astype(BF16)
        w_out_b = w_out[li].astype(BF16)
        qn, kn = vec(gqa_q_norm), vec(gqa_k_norm)
        diff_args = (vec(diff_lambda_q1), vec(diff_lambda_k1), vec(diff_lambda_q2),
                     vec(diff_lambda_k2), vec(diff_subln), lam_init)

        (naq, nak, nav, gq, gk, gv, dq, dk, dv) = _in_proj(
            x, vec(g_mix_pre), sh1, sc1, w_in_b, qn, kn, tabs_g, tabs_d, dims)
        (naqc, nakc, navc, gqc, gkc, gvc, dqc, dkc, dvc) = _in_proj(
            x_ctx, vec(g_mix_pre), csh1, csc1, w_in_b, qn, kn, tabs_id, tabs_id, dims)

        oa = _na_attention(naq, nak, nav, nakc, navc, bias_tab[li])
        tq_g, tq_d = _pick(l, (256, 128)), _pick(l, (256, 128))
        casts_g, casts_d = [], []
        if (li + 1) < depth and (li + 1) % 2 == 1:
            mi = (li + 1) // 2
            e_, d_, f_ = moe_w1.shape[1:]
            flat = [moe_w1[mi].reshape(e_ * d_, f_), moe_w3[mi].reshape(e_ * d_, f_),
                    moe_w2[mi].reshape(e_ * f_, d_)]
            steps_g, steps_d = b * n_gkv * (l // tq_g), b * n_diff * (l // tq_d)
            if _cast_rows(flat[0], steps_g) and _cast_rows(flat[1], steps_d) and _cast_rows(flat[2], steps_d):
                casts_g, casts_d = flat[:1], flat[1:]
        ob = _attention(gq, [(gk, gv), (gkc, gvc)], group=group, tq=tq_g, casts=casts_g)
        oc = _attention(dq, [(dk, dv), (dkc, dvc)], group=2, tq=tq_d, diff=diff_args, casts=casts_d)
        if casts_g:
            ob, w1b = ob
            oc, w3b, w2b = oc
            moe_bf16[li + 1] = (w1b.reshape(e_, d_, f_), w3b.reshape(e_, d_, f_), w2b.reshape(e_, f_, d_))

        routed = li % 2 == 1
        if routed:
            router = jnp.zeros((d, LANES), F32).at[:, :N_EXPERTS].set(moe_router[li // 2])
            x1, h2, route = _out_proj([oa, ob, oc], w_out_b, x, vec(g_mix_post), gt1,
                                      vec(g_ffn_pre), sh2, sc2, router=router)
        else:
            x1, h2 = _out_proj([oa, ob, oc], w_out_b, x, vec(g_mix_post), gt1,
                               vec(g_ffn_pre), sh2, sc2)

        if not last:
            tqc = _pick(cn, (256, 128))
            oac = _attention(naqc, [(nakc, navc)], group=1, tq=tqc)
            obc = _attention(gqc, [(gkc, gvc)], group=group, tq=_pick(cn, (128,)))
            occ = _attention(dqc, [(dkc, dvc)], group=2, tq=tqc, diff=diff_args)
            if routed:
                router = jnp.zeros((d, LANES), F32).at[:, :N_EXPERTS].set(moe_router[li // 2])
                xc1, h2c, route_c = _out_proj([oac, obc, occ], w_out_b, x_ctx, vec(g_mix_post),
                                              cgt1, vec(g_ffn_pre), csh2, csc2, router=router)
            else:
                xc1, h2c = _out_proj([oac, obc, occ], w_out_b, x_ctx, vec(g_mix_post), cgt1,
                                     vec(g_ffn_pre), csh2, csc2)

        if routed:
            if li in moe_bf16:
                w1, w3, w2 = moe_bf16.pop(li)
            else:
                w1, w3, w2 = [w[li // 2].astype(BF16) for w in (moe_w1, moe_w3, moe_w2)]
            x = _moe_ffn(h2, route, w1, w3, w2, x1, vec(g_ffn_post), gt2)
            if not last:
                x_ctx = _moe_ffn(h2c, route_c, w1, w3, w2, xc1, vec(g_ffn_post), cgt2)
        else:
            w1 = ffn_w1[li // 2].astype(BF16)
            w3 = ffn_w3[li // 2].astype(BF16)
            w2 = ffn_w2[li // 2].astype(BF16)
            x = _ffn_dense(h2, w1, w3, w2, x1, vec(g_ffn_post), gt2)
            if not last:
                x_ctx = _ffn_dense(h2c.reshape(1, b * cn, d), w1, w3, w2, xc1.reshape(1, b * cn, d),
                                   vec(g_ffn_post), cgt2[:1]).reshape(b, cn, d)
    return x
```

```python
import functools
import math

import jax
import jax.numpy as jnp
from jax import lax
from jax.experimental import pallas as pl
from jax.experimental.pallas import tpu as pltpu

GRID_W = 64
HEAD_DIM = 128
DIFF_QK_DIM = HEAD_DIM // 2
NA_ROWS = 8
NA_COLS = 16
ROPE_THETA = 10000.0
N_EXPERTS = 8
TOP_K = 2
EPS = 1e-6

LANES = 128
MXU_WIDTH = 256
NEG_BIG = -1e30
LOG2E = math.log2(math.e)
VMEM_LIMIT = 56 * 1024 * 1024

BF16 = jnp.bfloat16
F32 = jnp.float32
NT_DIMS = (((1,), (1,)), ((), ()))


def _cparams(sem):
    return pltpu.CompilerParams(dimension_semantics=sem, vmem_limit_bytes=VMEM_LIMIT)


def _pick(n, cands):
    for c in cands:
        if n % c == 0:
            return c
    raise ValueError(f"no tile in {cands} divides {n}")


def _rms(y):
    return y * lax.rsqrt(jnp.mean(y * y, axis=-1, keepdims=True) + EPS)


def _mod_kernel(c_ref, w_ref, b_ref, o_ref):
    c = c_ref[...]
    s = c / (1.0 + jnp.exp(-c))
    w = w_ref[0]
    s_hi = s.astype(BF16)
    s_lo = (s - s_hi.astype(F32)).astype(BF16)
    w_hi = w.astype(BF16)
    w_lo = (w - w_hi.astype(F32)).astype(BF16)
    o_ref[0] = (jnp.dot(s_hi, w_hi, preferred_element_type=F32)
                + jnp.dot(s_hi, w_lo, preferred_element_type=F32)
                + jnp.dot(s_lo, w_hi, preferred_element_type=F32)) + b_ref[0]


def _modulation(cvec, w_mod, b_mod):
    depth, d, n = w_mod.shape
    r = cvec.shape[0]
    tn = _pick(n, (1024, 512, 256, 128))
    return pl.pallas_call(
        _mod_kernel,
        out_shape=jax.ShapeDtypeStruct((depth, r, n), F32),
        grid=(depth, n // tn),
        in_specs=[pl.BlockSpec((r, d), lambda l, j: (0, 0)),
                  pl.BlockSpec((1, d, tn), lambda l, j: (l, 0, j)),
                  pl.BlockSpec((1, 1, tn), lambda l, j: (l, 0, j))],
        out_specs=pl.BlockSpec((1, r, tn), lambda l, j: (l, 0, j)),
        compiler_params=_cparams(("parallel", "parallel")),
        name="adaln_mod",
    )(cvec, w_mod, b_mod.reshape(depth, 1, n))


def _na_bias_kernel(rpb_ref, o_ref):
    n_dc = 2 * NA_COLS - 1
    base = pl.program_id(0) * ((2 * NA_ROWS - 1) * n_dc)
    shape = (GRID_W, LANES)
    qc = lax.broadcasted_iota(jnp.int32, shape, 0)
    lane = lax.broadcasted_iota(jnp.int32, shape, 1)
    half = lane >> 6
    kc = lane & (GRID_W - 1)
    dc = jnp.clip(kc - qc, -(NA_COLS - 1), NA_COLS - 1) + (NA_COLS - 1)
    cs = jnp.clip(qc - NA_COLS // 2, 0, GRID_W - NA_COLS)
    ok = (kc >= cs) & (kc < cs + NA_COLS)
    code = jnp.where(ok, half * 32 + dc, -1)
    for d in range(2 * NA_ROWS - 2):
        acc = jnp.full(shape, NEG_BIG, F32)
        for hf in range(2):
            for j in range(n_dc):
                val = rpb_ref[base + (d + hf) * n_dc + j] * LOG2E
                acc = jnp.where(code == hf * 32 + j, val, acc)
        o_ref[0, d] = acc


def _na_bias_table(rpb):
    nh = rpb.shape[0]
    return pl.pallas_call(
        _na_bias_kernel,
        out_shape=jax.ShapeDtypeStruct((nh, 2 * NA_ROWS - 2, GRID_W, LANES), F32),
        grid=(nh,),
        in_specs=[pl.BlockSpec(memory_space=pltpu.SMEM)],
        out_specs=pl.BlockSpec((1, 2 * NA_ROWS - 2, GRID_W, LANES), lambda h: (h, 0, 0, 0)),
        compiler_params=_cparams(("parallel",)),
        name="na_bias_table",
    )(rpb.reshape(-1))


def _rope(y, cos, sin_hi, sin_lo, dist):
    return (y * cos + pltpu.roll(y, LANES - dist, 1) * sin_hi
            + pltpu.roll(y, dist, 1) * sin_lo)


def _in_proj_kernel(x_ref, g_ref, sh_ref, sc_ref, w_ref, qn_ref, kn_ref,
                    cg_ref, sga_ref, sgb_ref, cd_ref, sda_ref, sdb_ref,
                    naq_ref, nak_ref, nav_ref, gq_ref, gk_ref, gv_ref,
                    dq_ref, dk_ref, dv_ref, *, n_na, n_gq, n_gkv, n_diff):
    x = x_ref[0]
    h = _rms(x) * g_ref[...]
    h = h * (1.0 + sc_ref[0]) + sh_ref[0]
    hb = h.astype(BF16)

    def proj(c0, nheads):
        return jnp.dot(hb, w_ref[:, c0:c0 + nheads * HEAD_DIM], preferred_element_type=F32)

    def heads(y, nheads):
        return [y[:, i * HEAD_DIM:(i + 1) * HEAD_DIM] for i in range(nheads)]

    na_scale = HEAD_DIM ** -0.5 * LOG2E
    diff_scale = DIFF_QK_DIM ** -0.5 * LOG2E
    c0 = 0
    for i, y in enumerate(heads(proj(c0, n_na), n_na)):
        naq_ref[0, i] = (y * na_scale).astype(BF16)
    c0 += n_na * HEAD_DIM
    for i, y in enumerate(heads(proj(c0, n_na), n_na)):
        nak_ref[0, i] = y.astype(BF16)
    c0 += n_na * HEAD_DIM
    lane = lax.broadcasted_iota(jnp.int32, (x.shape[0], LANES), 1)
    ones_col = jnp.where(lane == 0, 1.0, 0.0).astype(BF16)
    for i, y in enumerate(heads(proj(c0, n_na), n_na)):
        nav_ref[0, i, :, :HEAD_DIM] = y.astype(BF16)
        nav_ref[0, i, :, HEAD_DIM:] = ones_col
    c0 += n_na * HEAD_DIM

    cg, sga, sgb = cg_ref[...], sga_ref[...], sgb_ref[...]
    for i, y in enumerate(heads(proj(c0, n_gq), n_gq)):
        y = _rope(_rms(y) * qn_ref[...], cg, sga, sgb, 32)
        gq_ref[0, i] = (y * na_scale).astype(BF16)
    c0 += n_gq * HEAD_DIM
    for i, y in enumerate(heads(proj(c0, n_gkv), n_gkv)):
        y = _rope(_rms(y) * kn_ref[...], cg, sga, sgb, 32)
        gk_ref[0, i] = y.astype(BF16)
    c0 += n_gkv * HEAD_DIM
    for i, y in enumerate(heads(proj(c0, n_gkv), n_gkv)):
        gv_ref[0, i, :, :HEAD_DIM] = y.astype(BF16)
        gv_ref[0, i, :, HEAD_DIM:] = ones_col
    c0 += n_gkv * HEAD_DIM

    cd, sda, sdb = cd_ref[...], sda_ref[...], sdb_ref[...]
    for i, y in enumerate(heads(proj(c0, n_diff), n_diff)):
        y = _rope(y, cd, sda, sdb, 16) * diff_scale
        dq_ref[0, 2 * i] = jnp.where(lane < DIFF_QK_DIM, y, 0.0).astype(BF16)
        dq_ref[0, 2 * i + 1] = jnp.where(lane >= DIFF_QK_DIM, y, 0.0).astype(BF16)
    c0 += n_diff * HEAD_DIM
    for i, y in enumerate(heads(proj(c0, n_diff), n_diff)):
        dk_ref[0, i] = _rope(y, cd, sda, sdb, 16).astype(BF16)
    c0 += n_diff * HEAD_DIM
    for i, y in enumerate(heads(proj(c0, n_diff), n_diff)):
        dv_ref[0, i, :, :HEAD_DIM] = y.astype(BF16)
        dv_ref[0, i, :, HEAD_DIM:] = ones_col


def _in_proj(x, g, sh, sc, w_in, qn, kn, tabs_g, tabs_d, dims):
    b, l, d = x.shape
    n_na, n_gq, n_gkv, n_diff = dims
    n_in = w_in.shape[1]
    tm = _pick(l, (256, 128))
    vec = pl.BlockSpec((1, d), lambda bi, i: (0, 0))
    mod = pl.BlockSpec((1, 1, d), lambda bi, i: (bi, 0, 0))
    hvec = pl.BlockSpec((1, HEAD_DIM), lambda bi, i: (0, 0))
    tab = pl.BlockSpec((tm, LANES), lambda bi, i: (i, 0))

    def out(nh, width=HEAD_DIM):
        return (jax.ShapeDtypeStruct((b, nh, l, width), BF16),
                pl.BlockSpec((1, nh, tm, width), lambda bi, i: (bi, 0, i, 0)))

    outs = [out(n_na), out(n_na), out(n_na, 2 * HEAD_DIM), out(n_gq), out(n_gkv),
            out(n_gkv, 2 * HEAD_DIM), out(2 * n_diff), out(n_diff), out(n_diff, 2 * HEAD_DIM)]
    kern = functools.partial(_in_proj_kernel, n_na=n_na, n_gq=n_gq, n_gkv=n_gkv, n_diff=n_diff)
    return pl.pallas_call(
        kern,
        out_shape=[o[0] for o in outs],
        grid=(b, l // tm),
        in_specs=[pl.BlockSpec((1, tm, d), lambda bi, i: (bi, i, 0)), vec, mod, mod,
                  pl.BlockSpec((d, n_in), lambda bi, i: (0, 0), pipeline_mode=pl.Buffered(1)),
                  hvec, hvec, tab, tab, tab, tab, tab, tab],
        out_specs=[o[1] for o in outs],
        compiler_params=_cparams(("parallel", "parallel")),
        name="in_proj",
    )(x, g, sh, sc, w_in, qn, kn, *tabs_g, *tabs_d)


def _attn_kernel(*refs, n_seg, group, tq, chunks, diff, lam_init, n_cast):
    q_ref = refs[0]
    kv_refs = refs[1:1 + 2 * n_seg]
    rest = list(refs[1 + 2 * n_seg:])
    cast_out = [rest.pop() for _ in range(n_cast)][::-1]
    o_ref = rest.pop()
    cast_in = [rest.pop() for _ in range(n_cast)][::-1]
    for src_ref, dst_ref in zip(cast_in, cast_out):
        dst_ref[...] = src_ref[...].astype(BF16)
    if diff:
        lq1_ref, lk1_ref, lq2_ref, lk2_ref, sub_ref = rest
    m_rows = group * tq

    def step(k, v, carry):
        m, acc = carry
        q = q_ref[0].reshape(m_rows, HEAD_DIM)
        s = lax.dot_general(q, k, NT_DIMS, preferred_element_type=F32)
        m_new = jnp.maximum(m, jnp.max(s, axis=-1, keepdims=True))
        p = jnp.exp2((s - m_new).astype(BF16))
        acc = jnp.exp2(m - m_new) * acc + jnp.dot(p, v, preferred_element_type=F32)
        return m_new, acc

    carry = (jnp.full((m_rows, 1), NEG_BIG, F32), jnp.zeros((m_rows, 2 * HEAD_DIM), F32))
    for si in range(n_seg):
        k_ref, v_ref = kv_refs[2 * si], kv_refs[2 * si + 1]
        n_chunks, tk = chunks[si]
        for c in range(n_chunks):
            rows = slice(c * tk, (c + 1) * tk)
            carry = step(k_ref[0, 0, rows, :], v_ref[0, 0, rows, :], carry)
    _, acc = carry
    o = acc[:, :HEAD_DIM] / acc[:, HEAD_DIM:HEAD_DIM + 1]
    if diff:
        lam = (jnp.exp(jnp.sum(lq1_ref[...] * lk1_ref[...], axis=-1, keepdims=True))
               - jnp.exp(jnp.sum(lq2_ref[...] * lk2_ref[...], axis=-1, keepdims=True)) + lam_init)
        od = o[:tq] - lam * o[tq:]
        o_ref[0] = (_rms(od) * sub_ref[...] * (1.0 - lam_init)).astype(BF16)
    else:
        for g in range(group):
            o_ref[0, :, g * HEAD_DIM:(g + 1) * HEAD_DIM] = o[g * tq:(g + 1) * tq].astype(BF16)


def _cast_rows(a, n_steps):
    rows = a.shape[0] // n_steps
    return rows if rows * n_steps == a.shape[0] and rows % 16 == 0 else 0


def _attention(q, segs, *, group, tq, diff=None, casts=()):
    b, hq, lq, _ = q.shape
    hkv = hq // group
    nq = lq // tq
    out_heads = 1 if diff is not None else group
    kv_args, kv_specs, chunks = [], [], []
    for k, v in segs:
        s = k.shape[2]
        tk = _pick(s, (512, 256, 128) if diff is not None else (256, 128))
        chunks.append((s // tk, tk))
        kv_args += [k, v]
        kv_specs += [pl.BlockSpec((1, 1, s, a.shape[3]), lambda bi, h, i: (bi, h, 0, 0))
                     for a in (k, v)]
    extra_args, extra_specs, lam_init = [], [], 0.0
    if diff is not None:
        lq1, lk1, lq2, lk2, subln, lam_init = diff
        extra_args = [lq1, lk1, lq2, lk2, subln]
        extra_specs = [pl.BlockSpec((1, a.shape[1]), lambda bi, h, i: (0, 0)) for a in extra_args]
    cast_specs = [pl.BlockSpec((_cast_rows(a, b * hkv * nq), a.shape[1]),
                               lambda bi, h, i: ((bi * hkv + h) * nq + i, 0)) for a in casts]
    kern = functools.partial(_attn_kernel, n_seg=len(segs), group=group, tq=tq,
                             chunks=tuple(chunks), diff=diff is not None, lam_init=lam_init,
                             n_cast=len(casts))
    outs = pl.pallas_call(
        kern,
        out_shape=[jax.ShapeDtypeStruct((b, lq, hkv * out_heads * HEAD_DIM), BF16)]
                  + [jax.ShapeDtypeStruct(a.shape, BF16) for a in casts],
        grid=(b, hkv, nq),
        in_specs=[pl.BlockSpec((1, group, tq, HEAD_DIM), lambda bi, h, i: (bi, h, i, 0))]
                 + kv_specs + extra_specs + cast_specs,
        out_specs=[pl.BlockSpec((1, tq, out_heads * HEAD_DIM), lambda bi, h, i: (bi, i, h))] + cast_specs,
        compiler_params=_cparams(("parallel", "parallel", "parallel")),
        name="diff_attn" if diff is not None else "gqa_attn",
    )(q, *kv_args, *extra_args, *casts)
    return outs[0] if not casts else outs


NA_ROWS_PER_TRIP = 32


def _na_kernel(q_ref, k_ref, v_ref, kc_ref, vc_ref, bias_ref, *rest, rows, n_cast):
    cast_in, o_ref, cast_out = rest[:n_cast], rest[n_cast], rest[n_cast + 1:]
    band = NA_ROWS * GRID_W
    kc = kc_ref[0, 0]
    vc = vc_ref[0, 0]

    def body(r):
        rs = jnp.clip(r - NA_ROWS // 2, 0, rows - NA_ROWS)
        off = rs - r + (NA_ROWS - 1)
        qrows = pl.ds(pl.multiple_of(r * GRID_W, GRID_W), GRID_W)
        krows = pl.ds(pl.multiple_of(rs * GRID_W, GRID_W), band)
        q = q_ref[0, 0, qrows, :]
        bias = jnp.concatenate([bias_ref[0, off + 2 * j] for j in range(NA_ROWS // 2)], axis=1)
        s = lax.dot_general(q, k_ref[0, 0, krows, :], NT_DIMS, preferred_element_type=F32) + bias
        sc = lax.dot_general(q, kc, NT_DIMS, preferred_element_type=F32)
        m = jnp.maximum(jnp.max(s, axis=-1, keepdims=True), jnp.max(sc, axis=-1, keepdims=True))
        p = jnp.exp2((s - m).astype(BF16))
        pc = jnp.exp2((sc - m).astype(BF16))
        o = (jnp.dot(p, v_ref[0, 0, krows, :], preferred_element_type=F32)
             + jnp.dot(pc, vc, preferred_element_type=F32))
        o_ref[0, qrows, :] = (o[:, :HEAD_DIM] / o[:, HEAD_DIM:HEAD_DIM + 1]).astype(BF16)

    per_trip = min(NA_ROWS_PER_TRIP, rows)
    n_trips = rows // per_trip

    def trip(t, _):
        for src_ref, dst_ref in zip(cast_in, cast_out):
            n = src_ref.shape[0] // n_trips
            part = pl.ds(pl.multiple_of(t * n, n), n)
            dst_ref[part, :] = src_ref[part, :].astype(BF16)
        for i in range(per_trip):
            body(t * per_trip + i)
        return 0

    lax.fori_loop(0, n_trips, trip, 0)


def _na_cast_rows(a, n_steps, rows):
    n_trips = rows // min(NA_ROWS_PER_TRIP, rows)
    per_step = _cast_rows(a, n_steps)
    return per_step if per_step and per_step % (16 * n_trips) == 0 else 0


def _na_attention(q, k, v, kc, vc, bias, casts=()):
    b, nh, l, _ = q.shape
    rows = l // GRID_W
    cast_specs = [pl.BlockSpec((_na_cast_rows(a, b * nh, rows), a.shape[1]),
                               lambda bi, h: (bi * nh + h, 0)) for a in casts]

    def full(a):
        return pl.BlockSpec((1, 1) + a.shape[2:], lambda bi, h: (bi, h, 0, 0))

    outs = pl.pallas_call(
        functools.partial(_na_kernel, rows=rows, n_cast=len(casts)),
        out_shape=[jax.ShapeDtypeStruct((b, l, nh * HEAD_DIM), BF16)]
                  + [jax.ShapeDtypeStruct(a.shape, BF16) for a in casts],
        grid=(b, nh),
        in_specs=[full(q), full(k), full(v), full(kc), full(vc),
                  pl.BlockSpec((1,) + bias.shape[1:], lambda bi, h: (h, 0, 0, 0))] + cast_specs,
        out_specs=[pl.BlockSpec((1, l, HEAD_DIM), lambda bi, h: (bi, 0, h))] + cast_specs,
        compiler_params=_cparams(("parallel", "parallel")),
        name="na_attn",
    )(q, k, v, kc, vc, bias, *casts)
    return outs[0] if not casts else outs


def _out_proj_kernel(*refs, n_o, routed):
    o_refs = refs[:n_o]
    (w_ref, x_ref, gpost_ref, gt_ref, gpre_ref, sh_ref, sc_ref) = refs[n_o:n_o + 7]
    rest = refs[n_o + 7:]
    if routed:
        router_ref, x1_ref, h2_ref, route_ref = rest
    else:
        x1_ref, h2_ref = rest
    tm = x_ref.shape[1]
    sub = min(tm, 256)
    for r0 in range(0, tm, sub):
        rows = slice(r0, r0 + sub)
        acc = None
        c0 = 0
        for o_ref in o_refs:
            width = o_ref.shape[-1]
            part = jnp.dot(o_ref[0, rows, :], w_ref[c0:c0 + width, :], preferred_element_type=F32)
            acc = part if acc is None else acc + part
            c0 += width
        x1 = x_ref[0, rows, :] + gt_ref[0] * (_rms(acc) * gpost_ref[...])
        x1_ref[0, rows, :] = x1
        h2 = (_rms(x1) * gpre_ref[...]) * (1.0 + sc_ref[0]) + sh_ref[0]
        h2_ref[0, rows, :] = h2.astype(h2_ref.dtype)
        if routed:
            h_hi = h2.astype(BF16)
            h_lo = (h2 - h_hi.astype(F32)).astype(BF16)
            r = router_ref[...]
            r_hi = r.astype(BF16)
            r_lo = (r - r_hi.astype(F32)).astype(BF16)
            logits = (jnp.dot(h_hi, r_hi, preferred_element_type=F32)
                      + jnp.dot(h_hi, r_lo, preferred_element_type=F32)
                      + jnp.dot(h_lo, r_hi, preferred_element_type=F32))
            lane = lax.broadcasted_iota(jnp.int32, logits.shape, 1)
            logits = jnp.where(lane < N_EXPERTS, logits, NEG_BIG)
            v1 = jnp.max(logits, axis=-1, keepdims=True)
            i1 = jnp.min(jnp.where(logits == v1, lane, LANES), axis=-1, keepdims=True)
            rest_l = jnp.where(lane == i1, NEG_BIG, logits)
            v2 = jnp.max(rest_l, axis=-1, keepdims=True)
            i2 = jnp.min(jnp.where(rest_l == v2, lane, LANES), axis=-1, keepdims=True)
            e2 = jnp.exp(v2 - v1)
            g1 = 1.0 / (1.0 + e2)
            g2 = e2 / (1.0 + e2)
            route_ref[0, rows, :] = jnp.where(lane == 0, i1.astype(F32),
                                    jnp.where(lane == 1, i2.astype(F32),
                                    jnp.where(lane == 2, g1, jnp.where(lane == 3, g2, 0.0))))


def _out_proj(o_list, w_out, x, gpost, gt, gpre, sh, sc, router=None):
    b, l, d = x.shape
    tm = _pick(l, (512, 256, 128))
    routed = router is not None
    vec = pl.BlockSpec((1, d), lambda bi, i: (0, 0))
    mod = pl.BlockSpec((1, 1, d), lambda bi, i: (bi, 0, 0))
    row = pl.BlockSpec((1, tm, d), lambda bi, i: (bi, i, 0))
    in_specs = [pl.BlockSpec((1, tm, o.shape[-1]), lambda bi, i: (bi, i, 0)) for o in o_list]
    in_specs += [pl.BlockSpec(w_out.shape, lambda bi, i: (0, 0), pipeline_mode=pl.Buffered(1)),
                 row, vec, mod, vec, mod, mod]
    args = list(o_list) + [w_out, x, gpost, gt, gpre, sh, sc]
    out_shape = [jax.ShapeDtypeStruct((b, l, d), F32),
                 jax.ShapeDtypeStruct((b, l, d), F32 if routed else BF16)]
    out_specs = [row, row]
    if routed:
        in_specs.append(pl.BlockSpec(router.shape, lambda bi, i: (0, 0)))
        args.append(router)
        out_shape.append(jax.ShapeDtypeStruct((b, l, LANES), F32))
        out_specs.append(pl.BlockSpec((1, tm, LANES), lambda bi, i: (bi, i, 0)))
    return pl.pallas_call(
        functools.partial(_out_proj_kernel, n_o=len(o_list), routed=routed),
        out_shape=out_shape,
        grid=(b, l // tm),
        in_specs=in_specs,
        out_specs=out_specs,
        compiler_params=_cparams(("parallel", "parallel")),
        name="out_proj",
    )(*args)


def _swiglu_step(hb, w1_ref, w3_ref, w2_ref):
    out = None
    for c0 in range(0, w1_ref.shape[-1], MXU_WIDTH):
        cols = slice(c0, c0 + MXU_WIDTH)
        a = jnp.dot(hb, w1_ref[:, cols], preferred_element_type=F32)
        g = jnp.dot(hb, w3_ref[:, cols], preferred_element_type=F32)
        act = (a / (1.0 + jnp.exp(-a))) * g
        part = jnp.dot(act.astype(BF16), w2_ref[cols, :], preferred_element_type=F32)
        out = part if out is None else out + part
    return out


def _ffn_dense_kernel(h_ref, w1_ref, w3_ref, w2_ref, x_ref, gpost_ref, gt_ref, o_ref, acc_ref):
    f = pl.program_id(2)

    @pl.when(f == 0)
    def _():
        acc_ref[...] = jnp.zeros_like(acc_ref)

    acc_ref[...] += _swiglu_step(h_ref[0], w1_ref, w3_ref, w2_ref)

    @pl.when(f == pl.num_programs(2) - 1)
    def _():
        o_ref[0] = x_ref[0] + gt_ref[0] * (_rms(acc_ref[...]) * gpost_ref[...])


def _ff_tile(dff):
    return _pick(dff, (512, 256))


def _ffn_dense(h2, w1, w3, w2, x1, gpost, gt):
    b, l, d = x1.shape
    dff = w1.shape[1]
    tm = _pick(l, (512, 256, 128))
    tf = _ff_tile(dff)
    row = pl.BlockSpec((1, tm, d), lambda bi, i, f: (bi, i, 0))
    return pl.pallas_call(
        _ffn_dense_kernel,
        out_shape=jax.ShapeDtypeStruct((b, l, d), F32),
        grid=(b, l // tm, dff // tf),
        in_specs=[row,
                  pl.BlockSpec((d, tf), lambda bi, i, f: (0, f)),
                  pl.BlockSpec((d, tf), lambda bi, i, f: (0, f)),
                  pl.BlockSpec((tf, d), lambda bi, i, f: (f, 0)),
                  row,
                  pl.BlockSpec((1, d), lambda bi, i, f: (0, 0)),
                  pl.BlockSpec((1, 1, d), lambda bi, i, f: (bi, 0, 0))],
        out_specs=row,
        scratch_shapes=[pltpu.VMEM((tm, d), F32)],
        compiler_params=_cparams(("parallel", "parallel", "arbitrary")),
        name="ffn_dense",
    )(h2, w1, w3, w2, x1, gpost, gt)


def _ffn_expert_kernel(tile_e_ref, nact_ref, src0_ref, srcn_ref, h_hbm, w1_ref, w3_ref, w2_ref,
                       o_ref, xbuf_ref, hb_ref, acc_ref, sem, *, tm):
    j = pl.program_id(0)
    f = pl.program_id(1)
    nact = nact_ref[0]
    slot = j % 2

    def gather(idx_ref, dst_slot):
        def issue(i, _):
            pltpu.make_async_copy(h_hbm.at[pl.ds(idx_ref[i], 1)],
                                  xbuf_ref.at[dst_slot, pl.ds(i, 1)], sem.at[dst_slot]).start()
            return 0
        lax.fori_loop(0, tm, issue, 0, unroll=8)

    @pl.when(j < nact)
    def _():
        @pl.when(f == 0)
        def _():
            @pl.when(j == 0)
            def _():
                gather(src0_ref, 0)

            pltpu.make_async_copy(h_hbm.at[pl.ds(0, tm)], xbuf_ref.at[slot], sem.at[slot]).wait()
            hb_ref[...] = xbuf_ref[slot].astype(BF16)
            acc_ref[...] = jnp.zeros_like(acc_ref)

        @pl.when((f == 1) & (j + 1 < nact))
        def _():
            gather(srcn_ref, 1 - slot)

        acc_ref[...] += _swiglu_step(hb_ref[...], w1_ref.at[0], w3_ref.at[0], w2_ref.at[0])

        @pl.when(f == pl.num_programs(1) - 1)
        def _():
            o_ref[...] = acc_ref[...]

    @pl.when((j >= nact_ref[0]) & (f == pl.num_programs(1) - 1))
    def _():
        o_ref[...] = jnp.zeros_like(o_ref)


def _ffn_expert(h2, src, w1, w3, w2, tile_e, nact, tm):
    d = h2.shape[1]
    p = src.shape[0]
    n_tiles = p // tm
    tf = _ff_tile(w1.shape[2])
    nf = w1.shape[2] // tf
    assert nf >= 2

    def fsel(j, f, na):
        return jnp.where(j < na[0], f, nf - 1)

    return pl.pallas_call(
        functools.partial(_ffn_expert_kernel, tm=tm),
        out_shape=jax.ShapeDtypeStruct((p, d), F32),
        grid_spec=pltpu.PrefetchScalarGridSpec(
            num_scalar_prefetch=2,
            grid=(n_tiles, nf),
            in_specs=[pl.BlockSpec((tm,), lambda j, f, te, na: (0,), memory_space=pltpu.SMEM),
                      pl.BlockSpec((tm,), lambda j, f, te, na: (jnp.minimum(j + 1, n_tiles - 1),),
                                   memory_space=pltpu.SMEM),
                      pl.BlockSpec(memory_space=pl.ANY),
                      pl.BlockSpec((1, d, tf), lambda j, f, te, na: (te[j], 0, fsel(j, f, na))),
                      pl.BlockSpec((1, d, tf), lambda j, f, te, na: (te[j], 0, fsel(j, f, na))),
                      pl.BlockSpec((1, tf, d), lambda j, f, te, na: (te[j], fsel(j, f, na), 0))],
            out_specs=pl.BlockSpec((tm, d), lambda j, f, te, na: (j, 0)),
            scratch_shapes=[pltpu.VMEM((2, tm, d), F32), pltpu.VMEM((tm, d), BF16),
                            pltpu.VMEM((tm, d), F32), pltpu.SemaphoreType.DMA((2,))]),
        compiler_params=_cparams(("arbitrary", "arbitrary")),
        name="ffn_expert",
    )(tile_e, nact, src, src, h2, w1, w3, w2)


def _rank_kernel(route_ref, rank_ref, cnt_ref, carry_ref):
    i = pl.program_id(0)

    @pl.when(i == 0)
    def _():
        carry_ref[...] = jnp.zeros_like(carry_ref)

    r = route_ref[...]
    t = r.shape[0]
    lane = lax.broadcasted_iota(jnp.int32, r.shape, 1)
    e1 = r[:, 0:1].astype(jnp.int32)
    e2 = r[:, 1:2].astype(jnp.int32)
    oh1 = (lane == e1).astype(F32)
    oh2 = (lane == e2).astype(F32)
    c = (oh1 + oh2).astype(BF16)
    row = lax.broadcasted_iota(jnp.int32, (t, t), 0)
    col = lax.broadcasted_iota(jnp.int32, (t, t), 1)
    tri = (col < row).astype(F32).astype(BF16)
    prefix = jnp.dot(tri, c, preferred_element_type=F32) + carry_ref[0:1, :]
    r1 = jnp.sum(prefix * oh1, axis=-1, keepdims=True)
    r2 = jnp.sum(prefix * oh2, axis=-1, keepdims=True)
    rank_ref[...] = jnp.where(lane == 0, r1, jnp.where(lane == 1, r2, 0.0))
    total = carry_ref[0:1, :] + jnp.sum(oh1 + oh2, axis=0, keepdims=True)
    carry_ref[...] = jnp.broadcast_to(total, carry_ref.shape)
    cnt_ref[...] = jnp.broadcast_to(total, cnt_ref.shape)


def _expert_ranks(route):
    n = route.shape[0]
    t = _pick(n, (512, 256, 128))
    return pl.pallas_call(
        _rank_kernel,
        out_shape=[jax.ShapeDtypeStruct((n, LANES), F32), jax.ShapeDtypeStruct((8, LANES), F32)],
        grid=(n // t,),
        in_specs=[pl.BlockSpec((t, LANES), lambda i: (i, 0))],
        out_specs=[pl.BlockSpec((t, LANES), lambda i: (i, 0)),
                   pl.BlockSpec((8, LANES), lambda i: (0, 0))],
        scratch_shapes=[pltpu.VMEM((8, LANES), F32)],
        compiler_params=_cparams(("arbitrary",)),
        name="expert_ranks",
    )(route)


def _combine_kernel(pos0_ref, posn_ref, y_hbm, route_ref, x_ref, gpost_ref, gt_ref, o_ref,
                    buf_ref, sem, *, tt):
    n_steps = pl.num_programs(0) * pl.num_programs(1)
    step = pl.program_id(0) * pl.num_programs(1) + pl.program_id(1)
    slot = step % 2

    def gather(p_ref, dst_slot):
        def issue(i, _):
            for k in range(TOP_K):
                pltpu.make_async_copy(y_hbm.at[pl.ds(p_ref[TOP_K * i + k], 1)],
                                      buf_ref.at[dst_slot, pl.ds(k * tt + i, 1)],
                                      sem.at[dst_slot]).start()
            return 0
        lax.fori_loop(0, tt, issue, 0, unroll=4)

    @pl.when(step == 0)
    def _():
        gather(pos0_ref, 0)

    @pl.when(step + 1 < n_steps)
    def _():
        gather(posn_ref, 1 - slot)

    pltpu.make_async_copy(y_hbm.at[pl.ds(0, TOP_K * tt)], buf_ref.at[slot], sem.at[slot]).wait()
    r = route_ref[0]
    y = r[:, 2:3] * buf_ref[slot, 0:tt, :] + r[:, 3:4] * buf_ref[slot, tt:2 * tt, :]
    o_ref[0] = x_ref[0] + gt_ref[0] * (_rms(y) * gpost_ref[...])


def _combine(y, pos_flat, route, x1, gpost, gt):
    b, l, d = x1.shape
    tt = _pick(l, (256, 128))
    nt = l // tt
    n_steps = b * nt
    row = pl.BlockSpec((1, tt, d), lambda bi, i: (bi, i, 0))
    return pl.pallas_call(
        functools.partial(_combine_kernel, tt=tt),
        out_shape=jax.ShapeDtypeStruct((b, l, d), F32),
        grid=(b, nt),
        in_specs=[pl.BlockSpec((TOP_K * tt,), lambda bi, i: (0,), memory_space=pltpu.SMEM),
                  pl.BlockSpec((TOP_K * tt,),
                               lambda bi, i: (jnp.minimum(bi * nt + i + 1, n_steps - 1),),
                               memory_space=pltpu.SMEM),
                  pl.BlockSpec(memory_space=pl.ANY),
                  pl.BlockSpec((1, tt, LANES), lambda bi, i: (bi, i, 0)),
                  row,
                  pl.BlockSpec((1, d), lambda bi, i: (0, 0)),
                  pl.BlockSpec((1, 1, d), lambda bi, i: (bi, 0, 0))],
        out_specs=row,
        scratch_shapes=[pltpu.VMEM((2, TOP_K * tt, d), F32), pltpu.SemaphoreType.DMA((2,))],
        compiler_params=_cparams(("arbitrary", "arbitrary")),
        name="moe_combine",
    )(pos_flat, pos_flat, y, route, x1, gpost, gt)


def _moe_ffn(h2, route, w1, w3, w2, x1, gpost, gt):
    b, l, d = x1.shape
    n = b * l
    tm = 512
    route2 = route.reshape(n, LANES)
    rank, counts = _expert_ranks(route2)
    cnt = counts[0, :N_EXPERTS].astype(jnp.int32)
    tiles = (cnt + tm - 1) // tm
    tile_end = jnp.cumsum(tiles)
    row_off = (tile_end - tiles) * tm
    e_idx = route2[:, :TOP_K].astype(jnp.int32)
    pos = (row_off[e_idx] + rank[:, :TOP_K].astype(jnp.int32)).reshape(-1)
    n_tiles = (n * TOP_K) // tm + N_EXPERTS
    nact = tile_end[-1:]
    tile_ids = jnp.minimum(jnp.arange(n_tiles, dtype=jnp.int32), nact[0] - 1)
    tile_e = jnp.sum((tile_ids[:, None] >= tile_end[None, :]).astype(jnp.int32), axis=1)
    token = jnp.arange(n * TOP_K, dtype=jnp.int32) // TOP_K
    src = jnp.zeros((n_tiles * tm,), jnp.int32).at[pos].set(
        token, unique_indices=True, mode="promise_in_bounds")
    y = _ffn_expert(h2.reshape(n, d), src, w1, w3, w2, tile_e, nact.astype(jnp.int32), tm)
    return _combine(y, pos, route, x1, gpost, gt)


def _rope_tables(l, dims_per_map):
    lane = jnp.arange(LANES)
    d = lane % dims_per_map
    part_w = dims_per_map // 2
    half = part_w // 2
    is_col = (d // part_w) == 1
    first = (d % part_w) < half
    inv_freq = ROPE_THETA ** (-jnp.arange(half, dtype=F32) / half)
    freq = inv_freq[d % half]
    pos = jnp.arange(l)
    p = jnp.where(is_col[None, :], (pos % GRID_W)[:, None], (pos // GRID_W)[:, None]).astype(F32)
    ang = p * freq[None, :]
    cos, sin = jnp.cos(ang), jnp.sin(ang)
    zero = jnp.zeros_like(sin)
    return cos, jnp.where(first[None, :], -sin, zero), jnp.where(first[None, :], zero, sin)


def _identity_tables(l):
    return jnp.ones((l, LANES), F32), jnp.zeros((l, LANES), F32), jnp.zeros((l, LANES), F32)


def kernel(x, c, ctx, c_ctx, w_mod, b_mod, g_mix_pre, g_mix_post, g_ffn_pre, g_ffn_post,
           w_in, w_out, na_rpb, gqa_q_norm, gqa_k_norm, diff_lambda_q1, diff_lambda_k1,
           diff_lambda_q2, diff_lambda_k2, diff_subln, ffn_w1, ffn_w3, ffn_w2,
           moe_router, moe_w1, moe_w3, moe_w2):
    b, l, d = x.shape
    cn = ctx.shape[1]
    depth = w_mod.shape[0]
    n_heads = d // HEAD_DIM
    n_na, n_gq, n_diff = n_heads // 4, n_heads // 2, n_heads // 4
    n_gkv = n_gq // 4
    group = n_gq // n_gkv
    dims = (n_na, n_gq, n_gkv, n_diff)

    rows_pad = -(-(b + 1) // 8) * 8
    cvec = jnp.zeros((rows_pad, d), F32).at[:b].set(c).at[b].set(c_ctx)
    mods = _modulation(cvec, w_mod, b_mod)
    bias_tab = _na_bias_table(na_rpb.reshape((depth * n_na,) + na_rpb.shape[2:]))
    bias_tab = bias_tab.reshape((depth, n_na) + bias_tab.shape[1:])

    tabs_g = _rope_tables(l, HEAD_DIM)
    tabs_d = _rope_tables(l, DIFF_QK_DIM)
    tabs_id = _identity_tables(cn)

    x_ctx = ctx
    moe_bf16 = {}
    for li in range(depth):
        last = li == depth - 1
        lam_init = 0.8 - 0.6 * math.exp(-0.3 * li)
        m6 = mods[li].reshape(rows_pad, 6, d)
        sh1, sc1, gt1, sh2, sc2, gt2 = [m6[:b, k][:, None, :] for k in range(6)]
        csh1, csc1, cgt1, csh2, csc2, cgt2 = [
            jnp.broadcast_to(m6[b, k][None, None, :], (b, 1, d)) for k in range(6)]
        vec = lambda a: a[li].reshape(1, -1)
        w_in_b = w_in[li].astype(BF16)
        w_out_b = w_out[li].astype(BF16)
        qn, kn = vec(gqa_q_norm), vec(gqa_k_norm)
        diff_args = (vec(diff_lambda_q1), vec(diff_lambda_k1), vec(diff_lambda_q2),
                     vec(diff_lambda_k2), vec(diff_subln), lam_init)

        (naq, nak, nav, gq, gk, gv, dq, dk, dv) = _in_proj(
            x, vec(g_mix_pre), sh1, sc1, w_in_b, qn, kn, tabs_g, tabs_d, dims)
        (naqc, nakc, navc, gqc, gkc, gvc, dqc, dkc, dvc) = _in_proj(
            x_ctx, vec(g_mix_pre), csh1, csc1, w_in_b, qn, kn, tabs_id, tabs_id, dims)

        casts_na = []
        if li % 2 == 0:
            casts_na = [ffn_w1[li // 2], ffn_w3[li // 2], ffn_w2[li // 2]]
            if not all(_na_cast_rows(a, b * n_na, l // GRID_W) for a in casts_na):
                casts_na = []
        oa = _na_attention(naq, nak, nav, nakc, navc, bias_tab[li], casts=casts_na)
        if casts_na:
            oa, *dense_bf16 = oa
        tq_g, tq_d = _pick(l, (256, 128)), _pick(l, (256, 128))
        casts_g, casts_d = [], []
        if (li + 1) < depth and (li + 1) % 2 == 1:
            mi = (li + 1) // 2
            e_, d_, f_ = moe_w1.shape[1:]
            flat = [moe_w1[mi].reshape(e_ * d_, f_), moe_w3[mi].reshape(e_ * d_, f_),
                    moe_w2[mi].reshape(e_ * f_, d_)]
            steps_g, steps_d = b * n_gkv * (l // tq_g), b * n_diff * (l // tq_d)
            if _cast_rows(flat[0], steps_g) and _cast_rows(flat[1], steps_d) and _cast_rows(flat[2], steps_d):
                casts_g, casts_d = flat[:1], flat[1:]
        ob = _attention(gq, [(gk, gv), (gkc, gvc)], group=group, tq=tq_g, casts=casts_g)
        oc = _attention(dq, [(dk, dv), (dkc, dvc)], group=2, tq=tq_d, diff=diff_args, casts=casts_d)
        if casts_g:
            ob, w1b = ob
            oc, w3b, w2b = oc
            moe_bf16[li + 1] = (w1b.reshape(e_, d_, f_), w3b.reshape(e_, d_, f_), w2b.reshape(e_, f_, d_))

        routed = li % 2 == 1
        if routed:
            router = jnp.zeros((d, LANES), F32).at[:, :N_EXPERTS].set(moe_router[li // 2])
            x1, h2, route = _out_proj([oa, ob, oc], w_out_b, x, vec(g_mix_post), gt1,
                                      vec(g_ffn_pre), sh2, sc2, router=router)
        else:
            x1, h2 = _out_proj([oa, ob, oc], w_out_b, x, vec(g_mix_post), gt1,
                               vec(g_ffn_pre), sh2, sc2)

        if not last:
            tqc = _pick(cn, (256, 128))
            oac = _attention(naqc, [(nakc, navc)], group=1, tq=tqc)
            obc = _attention(gqc, [(gkc, gvc)], group=group, tq=_pick(cn, (128,)))
            occ = _attention(dqc, [(dkc, dvc)], group=2, tq=tqc, diff=diff_args)
            if routed:
                router = jnp.zeros((d, LANES), F32).at[:, :N_EXPERTS].set(moe_router[li // 2])
                xc1, h2c, route_c = _out_proj([oac, obc, occ], w_out_b, x_ctx, vec(g_mix_post),
                                              cgt1, vec(g_ffn_pre), csh2, csc2, router=router)
            else:
                xc1, h2c = _out_proj([oac, obc, occ], w_out_b, x_ctx, vec(g_mix_post), cgt1,
                                     vec(g_ffn_pre), csh2, csc2)

        if routed:
            if li in moe_bf16:
                w1, w3, w2 = moe_bf16.pop(li)
            else:
                w1, w3, w2 = [w[li // 2].astype(BF16) for w in (moe_w1, moe_w3, moe_w2)]
            x = _moe_ffn(h2, route, w1, w3, w2, x1, vec(g_ffn_post), gt2)
            if not last:
                x_ctx = _moe_ffn(h2c, route_c, w1, w3, w2, xc1, vec(g_ffn_post), cgt2)
        else:
            if casts_na:
                w1, w3, w2 = dense_bf16
            else:
                w1, w3, w2 = [w[li // 2].astype(BF16) for w in (ffn_w1, ffn_w3, ffn_w2)]
            x = _ffn_dense(h2, w1, w3, w2, x1, vec(g_ffn_post), gt2)
            if not last:
                x_ctx = _ffn_dense(h2c.reshape(1, b * cn, d), w1, w3, w2, xc1.reshape(1, b * cn, d),
                                   vec(g_ffn_post), cgt2[:1]).reshape(b, cn, d)
    return x
```

```python
import functools
import math

import jax
import jax.numpy as jnp
from jax import lax
from jax.experimental import pallas as pl
from jax.experimental.pallas import tpu as pltpu

GRID_W = 64
HEAD_DIM = 128
DIFF_QK_DIM = HEAD_DIM // 2
NA_ROWS = 8
NA_COLS = 16
ROPE_THETA = 10000.0
N_EXPERTS = 8
TOP_K = 2
EPS = 1e-6

LANES = 128
MXU_WIDTH = 256
NEG_BIG = -1e30
LOG2E = math.log2(math.e)
VMEM_LIMIT = 56 * 1024 * 1024

BF16 = jnp.bfloat16
F32 = jnp.float32
NT_DIMS = (((1,), (1,)), ((), ()))


def _cparams(sem):
    return pltpu.CompilerParams(dimension_semantics=sem, vmem_limit_bytes=VMEM_LIMIT)


def _pick(n, cands):
    for c in cands:
        if n % c == 0:
            return c
    raise ValueError(f"no tile in {cands} divides {n}")


def _rms(y):
    return y * lax.rsqrt(jnp.mean(y * y, axis=-1, keepdims=True) + EPS)


def _mod_kernel(c_ref, w_ref, b_ref, o_ref):
    c = c_ref[...]
    s = c / (1.0 + jnp.exp(-c))
    w = w_ref[0]
    s_hi = s.astype(BF16)
    s_lo = (s - s_hi.astype(F32)).astype(BF16)
    w_hi = w.astype(BF16)
    w_lo = (w - w_hi.astype(F32)).astype(BF16)
    o_ref[0] = (jnp.dot(s_hi, w_hi, preferred_element_type=F32)
                + jnp.dot(s_hi, w_lo, preferred_element_type=F32)
                + jnp.dot(s_lo, w_hi, preferred_element_type=F32)) + b_ref[0]


def _modulation(cvec, w_mod, b_mod):
    depth, d, n = w_mod.shape
    r = cvec.shape[0]
    tn = _pick(n, (1024, 512, 256, 128))
    return pl.pallas_call(
        _mod_kernel,
        out_shape=jax.ShapeDtypeStruct((depth, r, n), F32),
        grid=(depth, n // tn),
        in_specs=[pl.BlockSpec((r, d), lambda l, j: (0, 0)),
                  pl.BlockSpec((1, d, tn), lambda l, j: (l, 0, j)),
                  pl.BlockSpec((1, 1, tn), lambda l, j: (l, 0, j))],
        out_specs=pl.BlockSpec((1, r, tn), lambda l, j: (l, 0, j)),
        compiler_params=_cparams(("parallel", "parallel")),
        name="adaln_mod",
    )(cvec, w_mod, b_mod.reshape(depth, 1, n))


def _na_bias_kernel(rpb_ref, o_ref):
    n_dc = 2 * NA_COLS - 1
    base = pl.program_id(0) * ((2 * NA_ROWS - 1) * n_dc)
    shape = (GRID_W, LANES)
    qc = lax.broadcasted_iota(jnp.int32, shape, 0)
    lane = lax.broadcasted_iota(jnp.int32, shape, 1)
    half = lane >> 6
    kc = lane & (GRID_W - 1)
    dc = jnp.clip(kc - qc, -(NA_COLS - 1), NA_COLS - 1) + (NA_COLS - 1)
    cs = jnp.clip(qc - NA_COLS // 2, 0, GRID_W - NA_COLS)
    ok = (kc >= cs) & (kc < cs + NA_COLS)
    code = jnp.where(ok, half * 32 + dc, -1)
    for d in range(2 * NA_ROWS - 2):
        acc = jnp.full(shape, NEG_BIG, F32)
        for hf in range(2):
            for j in range(n_dc):
                val = rpb_ref[base + (d + hf) * n_dc + j] * LOG2E
                acc = jnp.where(code == hf * 32 + j, val, acc)
        o_ref[0, d] = acc


def _na_bias_table(rpb):
    nh = rpb.shape[0]
    return pl.pallas_call(
        _na_bias_kernel,
        out_shape=jax.ShapeDtypeStruct((nh, 2 * NA_ROWS - 2, GRID_W, LANES), F32),
        grid=(nh,),
        in_specs=[pl.BlockSpec(memory_space=pltpu.SMEM)],
        out_specs=pl.BlockSpec((1, 2 * NA_ROWS - 2, GRID_W, LANES), lambda h: (h, 0, 0, 0)),
        compiler_params=_cparams(("parallel",)),
        name="na_bias_table",
    )(rpb.reshape(-1))


def _rope(y, cos, sin_hi, sin_lo, dist):
    return (y * cos + pltpu.roll(y, LANES - dist, 1) * sin_hi
            + pltpu.roll(y, dist, 1) * sin_lo)


def _in_proj_kernel(x_ref, g_ref, sh_ref, sc_ref, w_ref, qn_ref, kn_ref,
                    cg_ref, sga_ref, sgb_ref, cd_ref, sda_ref, sdb_ref,
                    naq_ref, nak_ref, nav_ref, gq_ref, gk_ref, gv_ref,
                    dq_ref, dk_ref, dv_ref, *, n_na, n_gq, n_gkv, n_diff):
    x = x_ref[0]
    h = _rms(x) * g_ref[...]
    h = h * (1.0 + sc_ref[0]) + sh_ref[0]
    hb = h.astype(BF16)

    def proj(c0, nheads):
        return jnp.dot(hb, w_ref[:, c0:c0 + nheads * HEAD_DIM], preferred_element_type=F32)

    def heads(y, nheads):
        return [y[:, i * HEAD_DIM:(i + 1) * HEAD_DIM] for i in range(nheads)]

    na_scale = HEAD_DIM ** -0.5 * LOG2E
    diff_scale = DIFF_QK_DIM ** -0.5 * LOG2E
    c0 = 0
    for i, y in enumerate(heads(proj(c0, n_na), n_na)):
        naq_ref[0, i] = (y * na_scale).astype(BF16)
    c0 += n_na * HEAD_DIM
    for i, y in enumerate(heads(proj(c0, n_na), n_na)):
        nak_ref[0, i] = y.astype(BF16)
    c0 += n_na * HEAD_DIM
    lane = lax.broadcasted_iota(jnp.int32, (x.shape[0], LANES), 1)
    ones_col = jnp.where(lane == 0, 1.0, 0.0).astype(BF16)
    for i, y in enumerate(heads(proj(c0, n_na), n_na)):
        nav_ref[0, i, :, :HEAD_DIM] = y.astype(BF16)
        nav_ref[0, i, :, HEAD_DIM:] = ones_col
    c0 += n_na * HEAD_DIM

    cg, sga, sgb = cg_ref[...], sga_ref[...], sgb_ref[...]
    for i, y in enumerate(heads(proj(c0, n_gq), n_gq)):
        y = _rope(_rms(y) * qn_ref[...], cg, sga, sgb, 32)
        gq_ref[0, i] = (y * na_scale).astype(BF16)
    c0 += n_gq * HEAD_DIM
    for i, y in enumerate(heads(proj(c0, n_gkv), n_gkv)):
        y = _rope(_rms(y) * kn_ref[...], cg, sga, sgb, 32)
        gk_ref[0, i] = y.astype(BF16)
    c0 += n_gkv * HEAD_DIM
    for i, y in enumerate(heads(proj(c0, n_gkv), n_gkv)):
        gv_ref[0, i, :, :HEAD_DIM] = y.astype(BF16)
        gv_ref[0, i, :, HEAD_DIM:] = ones_col
    c0 += n_gkv * HEAD_DIM

    cd, sda, sdb = cd_ref[...], sda_ref[...], sdb_ref[...]
    for i, y in enumerate(heads(proj(c0, n_diff), n_diff)):
        y = _rope(y, cd, sda, sdb, 16) * diff_scale
        dq_ref[0, 2 * i] = jnp.where(lane < DIFF_QK_DIM, y, 0.0).astype(BF16)
        dq_ref[0, 2 * i + 1] = jnp.where(lane >= DIFF_QK_DIM, y, 0.0).astype(BF16)
    c0 += n_diff * HEAD_DIM
    for i, y in enumerate(heads(proj(c0, n_diff), n_diff)):
        dk_ref[0, i] = _rope(y, cd, sda, sdb, 16).astype(BF16)
    c0 += n_diff * HEAD_DIM
    for i, y in enumerate(heads(proj(c0, n_diff), n_diff)):
        dv_ref[0, i, :, :HEAD_DIM] = y.astype(BF16)
        dv_ref[0, i, :, HEAD_DIM:] = ones_col


def _in_proj(x, g, sh, sc, w_in, qn, kn, tabs_g, tabs_d, dims):
    b, l, d = x.shape
    n_na, n_gq, n_gkv, n_diff = dims
    n_in = w_in.shape[1]
    tm = _pick(l, (256, 128))
    vec = pl.BlockSpec((1, d), lambda bi, i: (0, 0))
    mod = pl.BlockSpec((1, 1, d), lambda bi, i: (bi, 0, 0))
    hvec = pl.BlockSpec((1, HEAD_DIM), lambda bi, i: (0, 0))
    tab = pl.BlockSpec((tm, LANES), lambda bi, i: (i, 0))

    def out(nh, width=HEAD_DIM):
        return (jax.ShapeDtypeStruct((b, nh, l, width), BF16),
                pl.BlockSpec((1, nh, tm, width), lambda bi, i: (bi, 0, i, 0)))

    outs = [out(n_na), out(n_na), out(n_na, 2 * HEAD_DIM), out(n_gq), out(n_gkv),
            out(n_gkv, 2 * HEAD_DIM), out(2 * n_diff), out(n_diff), out(n_diff, 2 * HEAD_DIM)]
    kern = functools.partial(_in_proj_kernel, n_na=n_na, n_gq=n_gq, n_gkv=n_gkv, n_diff=n_diff)
    return pl.pallas_call(
        kern,
        out_shape=[o[0] for o in outs],
        grid=(b, l // tm),
        in_specs=[pl.BlockSpec((1, tm, d), lambda bi, i: (bi, i, 0)), vec, mod, mod,
                  pl.BlockSpec((d, n_in), lambda bi, i: (0, 0), pipeline_mode=pl.Buffered(1)),
                  hvec, hvec, tab, tab, tab, tab, tab, tab],
        out_specs=[o[1] for o in outs],
        compiler_params=_cparams(("parallel", "parallel")),
        name="in_proj",
    )(x, g, sh, sc, w_in, qn, kn, *tabs_g, *tabs_d)


def _attn_kernel(*refs, n_seg, group, tq, chunks, diff, lam_init, n_cast):
    q_ref = refs[0]
    kv_refs = refs[1:1 + 2 * n_seg]
    rest = list(refs[1 + 2 * n_seg:])
    cast_out = [rest.pop() for _ in range(n_cast)][::-1]
    o_ref = rest.pop()
    cast_in = [rest.pop() for _ in range(n_cast)][::-1]
    for src_ref, dst_ref in zip(cast_in, cast_out):
        dst_ref[...] = src_ref[...].astype(BF16)
    if diff:
        lq1_ref, lk1_ref, lq2_ref, lk2_ref, sub_ref = rest
    m_rows = group * tq

    def step(k, v, carry):
        m, acc = carry
        q = q_ref[0].reshape(m_rows, HEAD_DIM)
        s = lax.dot_general(q, k, NT_DIMS, preferred_element_type=F32)
        m_new = jnp.maximum(m, jnp.max(s, axis=-1, keepdims=True))
        p = jnp.exp2((s - m_new).astype(BF16))
        acc = jnp.exp2(m - m_new) * acc + jnp.dot(p, v, preferred_element_type=F32)
        return m_new, acc

    carry = (jnp.full((m_rows, 1), NEG_BIG, F32), jnp.zeros((m_rows, 2 * HEAD_DIM), F32))
    for si in range(n_seg):
        k_ref, v_ref = kv_refs[2 * si], kv_refs[2 * si + 1]
        n_chunks, tk = chunks[si]
        for c in range(n_chunks):
            rows = slice(c * tk, (c + 1) * tk)
            carry = step(k_ref[0, 0, rows, :], v_ref[0, 0, rows, :], carry)
    _, acc = carry
    o = acc[:, :HEAD_DIM] / acc[:, HEAD_DIM:HEAD_DIM + 1]
    if diff:
        lam = (jnp.exp(jnp.sum(lq1_ref[...] * lk1_ref[...], axis=-1, keepdims=True))
               - jnp.exp(jnp.sum(lq2_ref[...] * lk2_ref[...], axis=-1, keepdims=True)) + lam_init)
        od = o[:tq] - lam * o[tq:]
        o_ref[0] = (_rms(od) * sub_ref[...] * (1.0 - lam_init)).astype(BF16)
    else:
        for g in range(group):
            o_ref[0, :, g * HEAD_DIM:(g + 1) * HEAD_DIM] = o[g * tq:(g + 1) * tq].astype(BF16)


def _cast_rows(a, n_steps):
    rows = a.shape[0] // n_steps
    return rows if rows * n_steps == a.shape[0] and rows % 16 == 0 else 0


def _attention(q, segs, *, group, tq, diff=None, casts=()):
    b, hq, lq, _ = q.shape
    hkv = hq // group
    nq = lq // tq
    out_heads = 1 if diff is not None else group
    kv_args, kv_specs, chunks = [], [], []
    for k, v in segs:
        s = k.shape[2]
        tk = _pick(s, (512, 256, 128) if diff is not None else (256, 128))
        chunks.append((s // tk, tk))
        kv_args += [k, v]
        kv_specs += [pl.BlockSpec((1, 1, s, a.shape[3]), lambda bi, h, i: (bi, h, 0, 0))
                     for a in (k, v)]
    extra_args, extra_specs, lam_init = [], [], 0.0
    if diff is not None:
        lq1, lk1, lq2, lk2, subln, lam_init = diff
        extra_args = [lq1, lk1, lq2, lk2, subln]
        extra_specs = [pl.BlockSpec((1, a.shape[1]), lambda bi, h, i: (0, 0)) for a in extra_args]
    cast_specs = [pl.BlockSpec((_cast_rows(a, b * hkv * nq), a.shape[1]),
                               lambda bi, h, i: ((bi * hkv + h) * nq + i, 0)) for a in casts]
    kern = functools.partial(_attn_kernel, n_seg=len(segs), group=group, tq=tq,
                             chunks=tuple(chunks), diff=diff is not None, lam_init=lam_init,
                             n_cast=len(casts))
    outs = pl.pallas_call(
        kern,
        out_shape=[jax.ShapeDtypeStruct((b, lq, hkv * out_heads * HEAD_DIM), BF16)]
                  + [jax.ShapeDtypeStruct(a.shape, BF16) for a in casts],
        grid=(b, hkv, nq),
        in_specs=[pl.BlockSpec((1, group, tq, HEAD_DIM), lambda bi, h, i: (bi, h, i, 0))]
                 + kv_specs + extra_specs + cast_specs,
        out_specs=[pl.BlockSpec((1, tq, out_heads * HEAD_DIM), lambda bi, h, i: (bi, i, h))] + cast_specs,
        compiler_params=_cparams(("parallel", "parallel", "parallel")),
        name="diff_attn" if diff is not None else "gqa_attn",
    )(q, *kv_args, *extra_args, *casts)
    return outs[0] if not casts else outs


NA_ROWS_PER_TRIP = 64


def _na_kernel(q_ref, k_ref, v_ref, kc_ref, vc_ref, bias_ref, *rest, rows, n_cast):
    cast_in, o_ref, cast_out = rest[:n_cast], rest[n_cast], rest[n_cast + 1:]
    band = NA_ROWS * GRID_W
    kc = kc_ref[0, 0]
    vc = vc_ref[0, 0]

    def body(r):
        rs = jnp.clip(r - NA_ROWS // 2, 0, rows - NA_ROWS)
        off = rs - r + (NA_ROWS - 1)
        qrows = pl.ds(pl.multiple_of(r * GRID_W, GRID_W), GRID_W)
        krows = pl.ds(pl.multiple_of(rs * GRID_W, GRID_W), band)
        q = q_ref[0, 0, qrows, :]
        bias = jnp.concatenate([bias_ref[0, off + 2 * j] for j in range(NA_ROWS // 2)], axis=1)
        s = lax.dot_general(q, k_ref[0, 0, krows, :], NT_DIMS, preferred_element_type=F32) + bias
        sc = lax.dot_general(q, kc, NT_DIMS, preferred_element_type=F32)
        m = jnp.maximum(jnp.max(s, axis=-1, keepdims=True), jnp.max(sc, axis=-1, keepdims=True))
        p = jnp.exp2((s - m).astype(BF16))
        pc = jnp.exp2((sc - m).astype(BF16))
        o = (jnp.dot(p, v_ref[0, 0, krows, :], preferred_element_type=F32)
             + jnp.dot(pc, vc, preferred_element_type=F32))
        o_ref[0, qrows, :] = (o[:, :HEAD_DIM] / o[:, HEAD_DIM:HEAD_DIM + 1]).astype(BF16)

    per_trip = min(NA_ROWS_PER_TRIP, rows)
    n_trips = rows // per_trip

    def trip(t, _):
        for src_ref, dst_ref in zip(cast_in, cast_out):
            n = src_ref.shape[0] // n_trips
            part = pl.ds(pl.multiple_of(t * n, n), n)
            dst_ref[part, :] = src_ref[part, :].astype(BF16)
        for i in range(per_trip):
            body(t * per_trip + i)
        return 0

    lax.fori_loop(0, n_trips, trip, 0)


def _na_cast_rows(a, n_steps, rows):
    n_trips = rows // min(NA_ROWS_PER_TRIP, rows)
    per_step = _cast_rows(a, n_steps)
    return per_step if per_step and per_step % (16 * n_trips) == 0 else 0


def _na_attention(q, k, v, kc, vc, bias, casts=()):
    b, nh, l, _ = q.shape
    rows = l // GRID_W
    cast_specs = [pl.BlockSpec((_na_cast_rows(a, b * nh, rows), a.shape[1]),
                               lambda bi, h: (bi * nh + h, 0)) for a in casts]

    def full(a):
        return pl.BlockSpec((1, 1) + a.shape[2:], lambda bi, h: (bi, h, 0, 0))

    outs = pl.pallas_call(
        functools.partial(_na_kernel, rows=rows, n_cast=len(casts)),
        out_shape=[jax.ShapeDtypeStruct((b, l, nh * HEAD_DIM), BF16)]
                  + [jax.ShapeDtypeStruct(a.shape, BF16) for a in casts],
        grid=(b, nh),
        in_specs=[full(q), full(k), full(v), full(kc), full(vc),
                  pl.BlockSpec((1,) + bias.shape[1:], lambda bi, h: (h, 0, 0, 0))] + cast_specs,
        out_specs=[pl.BlockSpec((1, l, HEAD_DIM), lambda bi, h: (bi, 0, h))] + cast_specs,
        compiler_params=_cparams(("parallel", "parallel")),
        name="na_attn",
    )(q, k, v, kc, vc, bias, *casts)
    return outs[0] if not casts else outs


def _out_proj_kernel(*refs, n_o, routed):
    o_refs = refs[:n_o]
    (w_ref, x_ref, gpost_ref, gt_ref, gpre_ref, sh_ref, sc_ref) = refs[n_o:n_o + 7]
    rest = refs[n_o + 7:]
    if routed:
        router_ref, x1_ref, h2_ref, route_ref = rest
    else:
        x1_ref, h2_ref = rest
    tm = x_ref.shape[1]
    sub = min(tm, 256)
    for r0 in range(0, tm, sub):
        rows = slice(r0, r0 + sub)
        acc = None
        c0 = 0
        for o_ref in o_refs:
            width = o_ref.shape[-1]
            part = jnp.dot(o_ref[0, rows, :], w_ref[c0:c0 + width, :], preferred_element_type=F32)
            acc = part if acc is None else acc + part
            c0 += width
        x1 = x_ref[0, rows, :] + gt_ref[0] * (_rms(acc) * gpost_ref[...])
        x1_ref[0, rows, :] = x1
        h2 = (_rms(x1) * gpre_ref[...]) * (1.0 + sc_ref[0]) + sh_ref[0]
        h2_ref[0, rows, :] = h2.astype(h2_ref.dtype)
        if routed:
            h_hi = h2.astype(BF16)
            h_lo = (h2 - h_hi.astype(F32)).astype(BF16)
            r = router_ref[...]
            r_hi = r.astype(BF16)
            r_lo = (r - r_hi.astype(F32)).astype(BF16)
            logits = (jnp.dot(h_hi, r_hi, preferred_element_type=F32)
                      + jnp.dot(h_hi, r_lo, preferred_element_type=F32)
                      + jnp.dot(h_lo, r_hi, preferred_element_type=F32))
            lane = lax.broadcasted_iota(jnp.int32, logits.shape, 1)
            logits = jnp.where(lane < N_EXPERTS, logits, NEG_BIG)
            v1 = jnp.max(logits, axis=-1, keepdims=True)
            i1 = jnp.min(jnp.where(logits == v1, lane, LANES), axis=-1, keepdims=True)
            rest_l = jnp.where(lane == i1, NEG_BIG, logits)
            v2 = jnp.max(rest_l, axis=-1, keepdims=True)
            i2 = jnp.min(jnp.where(rest_l == v2, lane, LANES), axis=-1, keepdims=True)
            e2 = jnp.exp(v2 - v1)
            g1 = 1.0 / (1.0 + e2)
            g2 = e2 / (1.0 + e2)
            route_ref[0, rows, :] = jnp.where(lane == 0, i1.astype(F32),
                                    jnp.where(lane == 1, i2.astype(F32),
                                    jnp.where(lane == 2, g1, jnp.where(lane == 3, g2, 0.0))))


def _out_proj(o_list, w_out, x, gpost, gt, gpre, sh, sc, router=None):
    b, l, d = x.shape
    tm = _pick(l, (512, 256, 128))
    routed = router is not None
    vec = pl.BlockSpec((1, d), lambda bi, i: (0, 0))
    mod = pl.BlockSpec((1, 1, d), lambda bi, i: (bi, 0, 0))
    row = pl.BlockSpec((1, tm, d), lambda bi, i: (bi, i, 0))
    in_specs = [pl.BlockSpec((1, tm, o.shape[-1]), lambda bi, i: (bi, i, 0)) for o in o_list]
    in_specs += [pl.BlockSpec(w_out.shape, lambda bi, i: (0, 0), pipeline_mode=pl.Buffered(1)),
                 row, vec, mod, vec, mod, mod]
    args = list(o_list) + [w_out, x, gpost, gt, gpre, sh, sc]
    out_shape = [jax.ShapeDtypeStruct((b, l, d), F32),
                 jax.ShapeDtypeStruct((b, l, d), F32 if routed else BF16)]
    out_specs = [row, row]
    if routed:
        in_specs.append(pl.BlockSpec(router.shape, lambda bi, i: (0, 0)))
        args.append(router)
        out_shape.append(jax.ShapeDtypeStruct((b, l, LANES), F32))
        out_specs.append(pl.BlockSpec((1, tm, LANES), lambda bi, i: (bi, i, 0)))
    return pl.pallas_call(
        functools.partial(_out_proj_kernel, n_o=len(o_list), routed=routed),
        out_shape=out_shape,
        grid=(b, l // tm),
        in_specs=in_specs,
        out_specs=out_specs,
        compiler_params=_cparams(("parallel", "parallel")),
        name="out_proj",
    )(*args)


def _swiglu_step(hb, w1_ref, w3_ref, w2_ref):
    out = None
    for c0 in range(0, w1_ref.shape[-1], MXU_WIDTH):
        cols = slice(c0, c0 + MXU_WIDTH)
        a = jnp.dot(hb, w1_ref[:, cols], preferred_element_type=F32)
        g = jnp.dot(hb, w3_ref[:, cols], preferred_element_type=F32)
        act = (a / (1.0 + jnp.exp(-a))) * g
        part = jnp.dot(act.astype(BF16), w2_ref[cols, :], preferred_element_type=F32)
        out = part if out is None else out + part
    return out


def _ffn_dense_kernel(h_ref, w1_ref, w3_ref, w2_ref, x_ref, gpost_ref, gt_ref, o_ref, acc_ref):
    f = pl.program_id(2)

    @pl.when(f == 0)
    def _():
        acc_ref[...] = jnp.zeros_like(acc_ref)

    acc_ref[...] += _swiglu_step(h_ref[0], w1_ref, w3_ref, w2_ref)

    @pl.when(f == pl.num_programs(2) - 1)
    def _():
        o_ref[0] = x_ref[0] + gt_ref[0] * (_rms(acc_ref[...]) * gpost_ref[...])


def _ff_tile(dff):
    return _pick(dff, (512, 256))


def _ffn_dense(h2, w1, w3, w2, x1, gpost, gt):
    b, l, d = x1.shape
    dff = w1.shape[1]
    tm = _pick(l, (512, 256, 128))
    tf = _ff_tile(dff)
    row = pl.BlockSpec((1, tm, d), lambda bi, i, f: (bi, i, 0))
    return pl.pallas_call(
        _ffn_dense_kernel,
        out_shape=jax.ShapeDtypeStruct((b, l, d), F32),
        grid=(b, l // tm, dff // tf),
        in_specs=[row,
                  pl.BlockSpec((d, tf), lambda bi, i, f: (0, f)),
                  pl.BlockSpec((d, tf), lambda bi, i, f: (0, f)),
                  pl.BlockSpec((tf, d), lambda bi, i, f: (f, 0)),
                  row,
                  pl.BlockSpec((1, d), lambda bi, i, f: (0, 0)),
                  pl.BlockSpec((1, 1, d), lambda bi, i, f: (bi, 0, 0))],
        out_specs=row,
        scratch_shapes=[pltpu.VMEM((tm, d), F32)],
        compiler_params=_cparams(("parallel", "parallel", "arbitrary")),
        name="ffn_dense",
    )(h2, w1, w3, w2, x1, gpost, gt)


def _ffn_expert_kernel(tile_e_ref, nact_ref, src0_ref, srcn_ref, h_hbm, w1_ref, w3_ref, w2_ref,
                       o_ref, xbuf_ref, hb_ref, acc_ref, sem, *, tm):
    j = pl.program_id(0)
    f = pl.program_id(1)
    nact = nact_ref[0]
    slot = j % 2

    def gather(idx_ref, dst_slot):
        def issue(i, _):
            pltpu.make_async_copy(h_hbm.at[pl.ds(idx_ref[i], 1)],
                                  xbuf_ref.at[dst_slot, pl.ds(i, 1)], sem.at[dst_slot]).start()
            return 0
        lax.fori_loop(0, tm, issue, 0, unroll=8)

    @pl.when(j < nact)
    def _():
        @pl.when(f == 0)
        def _():
            @pl.when(j == 0)
            def _():
                gather(src0_ref, 0)

            pltpu.make_async_copy(h_hbm.at[pl.ds(0, tm)], xbuf_ref.at[slot], sem.at[slot]).wait()
            hb_ref[...] = xbuf_ref[slot].astype(BF16)
            acc_ref[...] = jnp.zeros_like(acc_ref)

        @pl.when((f == 1) & (j + 1 < nact))
        def _():
            gather(srcn_ref, 1 - slot)

        acc_ref[...] += _swiglu_step(hb_ref[...], w1_ref.at[0], w3_ref.at[0], w2_ref.at[0])

        @pl.when(f == pl.num_programs(1) - 1)
        def _():
            o_ref[...] = acc_ref[...]

    @pl.when((j >= nact_ref[0]) & (f == pl.num_programs(1) - 1))
    def _():
        o_ref[...] = jnp.zeros_like(o_ref)


def _ffn_expert(h2, src, w1, w3, w2, tile_e, nact, tm):
    d = h2.shape[1]
    p = src.shape[0]
    n_tiles = p // tm
    tf = _ff_tile(w1.shape[2])
    nf = w1.shape[2] // tf
    assert nf >= 2

    def fsel(j, f, na):
        return jnp.where(j < na[0], f, nf - 1)

    return pl.pallas_call(
        functools.partial(_ffn_expert_kernel, tm=tm),
        out_shape=jax.ShapeDtypeStruct((p, d), F32),
        grid_spec=pltpu.PrefetchScalarGridSpec(
            num_scalar_prefetch=2,
            grid=(n_tiles, nf),
            in_specs=[pl.BlockSpec((tm,), lambda j, f, te, na: (0,), memory_space=pltpu.SMEM),
                      pl.BlockSpec((tm,), lambda j, f, te, na: (jnp.minimum(j + 1, n_tiles - 1),),
                                   memory_space=pltpu.SMEM),
                      pl.BlockSpec(memory_space=pl.ANY),
                      pl.BlockSpec((1, d, tf), lambda j, f, te, na: (te[j], 0, fsel(j, f, na))),
                      pl.BlockSpec((1, d, tf), lambda j, f, te, na: (te[j], 0, fsel(j, f, na))),
                      pl.BlockSpec((1, tf, d), lambda j, f, te, na: (te[j], fsel(j, f, na), 0))],
            out_specs=pl.BlockSpec((tm, d), lambda j, f, te, na: (j, 0)),
            scratch_shapes=[pltpu.VMEM((2, tm, d), F32), pltpu.VMEM((tm, d), BF16),
                            pltpu.VMEM((tm, d), F32), pltpu.SemaphoreType.DMA((2,))]),
        compiler_params=_cparams(("arbitrary", "arbitrary")),
        name="ffn_expert",
    )(tile_e, nact, src, src, h2, w1, w3, w2)


def _rank_kernel(route_ref, rank_ref, cnt_ref, carry_ref):
    i = pl.program_id(0)

    @pl.when(i == 0)
    def _():
        carry_ref[...] = jnp.zeros_like(carry_ref)

    r = route_ref[...]
    t = r.shape[0]
    lane = lax.broadcasted_iota(jnp.int32, r.shape, 1)
    e1 = r[:, 0:1].astype(jnp.int32)
    e2 = r[:, 1:2].astype(jnp.int32)
    oh1 = (lane == e1).astype(F32)
    oh2 = (lane == e2).astype(F32)
    c = (oh1 + oh2).astype(BF16)
    row = lax.broadcasted_iota(jnp.int32, (t, t), 0)
    col = lax.broadcasted_iota(jnp.int32, (t, t), 1)
    tri = (col < row).astype(F32).astype(BF16)
    prefix = jnp.dot(tri, c, preferred_element_type=F32) + carry_ref[0:1, :]
    r1 = jnp.sum(prefix * oh1, axis=-1, keepdims=True)
    r2 = jnp.sum(prefix * oh2, axis=-1, keepdims=True)
    rank_ref[...] = jnp.where(lane == 0, r1, jnp.where(lane == 1, r2, 0.0))
    total = carry_ref[0:1, :] + jnp.sum(oh1 + oh2, axis=0, keepdims=True)
    carry_ref[...] = jnp.broadcast_to(total, carry_ref.shape)
    cnt_ref[...] = jnp.broadcast_to(total, cnt_ref.shape)


def _expert_ranks(route):
    n = route.shape[0]
    t = _pick(n, (512, 256, 128))
    return pl.pallas_call(
        _rank_kernel,
        out_shape=[jax.ShapeDtypeStruct((n, LANES), F32), jax.ShapeDtypeStruct((8, LANES), F32)],
        grid=(n // t,),
        in_specs=[pl.BlockSpec((t, LANES), lambda i: (i, 0))],
        out_specs=[pl.BlockSpec((t, LANES), lambda i: (i, 0)),
                   pl.BlockSpec((8, LANES), lambda i: (0, 0))],
        scratch_shapes=[pltpu.VMEM((8, LANES), F32)],
        compiler_params=_cparams(("arbitrary",)),
        name="expert_ranks",
    )(route)


def _combine_kernel(pos0_ref, posn_ref, y_hbm, route_ref, x_ref, gpost_ref, gt_ref, o_ref,
                    buf_ref, sem, *, tt):
    n_steps = pl.num_programs(0) * pl.num_programs(1)
    step = pl.program_id(0) * pl.num_programs(1) + pl.program_id(1)
    slot = step % 2

    def gather(p_ref, dst_slot):
        def issue(i, _):
            for k in range(TOP_K):
                pltpu.make_async_copy(y_hbm.at[pl.ds(p_ref[TOP_K * i + k], 1)],
                                      buf_ref.at[dst_slot, pl.ds(k * tt + i, 1)],
                                      sem.at[dst_slot]).start()
            return 0
        lax.fori_loop(0, tt, issue, 0, unroll=4)

    @pl.when(step == 0)
    def _():
        gather(pos0_ref, 0)

    @pl.when(step + 1 < n_steps)
    def _():
        gather(posn_ref, 1 - slot)

    pltpu.make_async_copy(y_hbm.at[pl.ds(0, TOP_K * tt)], buf_ref.at[slot], sem.at[slot]).wait()
    r = route_ref[0]
    y = r[:, 2:3] * buf_ref[slot, 0:tt, :] + r[:, 3:4] * buf_ref[slot, tt:2 * tt, :]
    o_ref[0] = x_ref[0] + gt_ref[0] * (_rms(y) * gpost_ref[...])


def _combine(y, pos_flat, route, x1, gpost, gt):
    b, l, d = x1.shape
    tt = _pick(l, (256, 128))
    nt = l // tt
    n_steps = b * nt
    row = pl.BlockSpec((1, tt, d), lambda bi, i: (bi, i, 0))
    return pl.pallas_call(
        functools.partial(_combine_kernel, tt=tt),
        out_shape=jax.ShapeDtypeStruct((b, l, d), F32),
        grid=(b, nt),
        in_specs=[pl.BlockSpec((TOP_K * tt,), lambda bi, i: (0,), memory_space=pltpu.SMEM),
                  pl.BlockSpec((TOP_K * tt,),
                               lambda bi, i: (jnp.minimum(bi * nt + i + 1, n_steps - 1),),
                               memory_space=pltpu.SMEM),
                  pl.BlockSpec(memory_space=pl.ANY),
                  pl.BlockSpec((1, tt, LANES), lambda bi, i: (bi, i, 0)),
                  row,
                  pl.BlockSpec((1, d), lambda bi, i: (0, 0)),
                  pl.BlockSpec((1, 1, d), lambda bi, i: (bi, 0, 0))],
        out_specs=row,
        scratch_shapes=[pltpu.VMEM((2, TOP_K * tt, d), F32), pltpu.SemaphoreType.DMA((2,))],
        compiler_params=_cparams(("arbitrary", "arbitrary")),
        name="moe_combine",
    )(pos_flat, pos_flat, y, route, x1, gpost, gt)


def _moe_ffn(h2, route, w1, w3, w2, x1, gpost, gt):
    b, l, d = x1.shape
    n = b * l
    tm = 512
    route2 = route.reshape(n, LANES)
    rank, counts = _expert_ranks(route2)
    cnt = counts[0, :N_EXPERTS].astype(jnp.int32)
    tiles = (cnt + tm - 1) // tm
    tile_end = jnp.cumsum(tiles)
    row_off = (tile_end - tiles) * tm
    e_idx = route2[:, :TOP_K].astype(jnp.int32)
    pos = (row_off[e_idx] + rank[:, :TOP_K].astype(jnp.int32)).reshape(-1)
    n_tiles = (n * TOP_K) // tm + N_EXPERTS
    nact = tile_end[-1:]
    tile_ids = jnp.minimum(jnp.arange(n_tiles, dtype=jnp.int32), nact[0] - 1)
    tile_e = jnp.sum((tile_ids[:, None] >= tile_end[None, :]).astype(jnp.int32), axis=1)
    token = jnp.arange(n * TOP_K, dtype=jnp.int32) // TOP_K
    src = jnp.zeros((n_tiles * tm,), jnp.int32).at[pos].set(
        token, unique_indices=True, mode="promise_in_bounds")
    y = _ffn_expert(h2.reshape(n, d), src, w1, w3, w2, tile_e, nact.astype(jnp.int32), tm)
    return _combine(y, pos, route, x1, gpost, gt)


def _rope_tables(l, dims_per_map):
    lane = jnp.arange(LANES)
    d = lane % dims_per_map
    part_w = dims_per_map // 2
    half = part_w // 2
    is_col = (d // part_w) == 1
    first = (d % part_w) < half
    inv_freq = ROPE_THETA ** (-jnp.arange(half, dtype=F32) / half)
    freq = inv_freq[d % half]
    pos = jnp.arange(l)
    p = jnp.where(is_col[None, :], (pos % GRID_W)[:, None], (pos // GRID_W)[:, None]).astype(F32)
    ang = p * freq[None, :]
    cos, sin = jnp.cos(ang), jnp.sin(ang)
    zero = jnp.zeros_like(sin)
    return cos, jnp.where(first[None, :], -sin, zero), jnp.where(first[None, :], zero, sin)


def _identity_tables(l):
    return jnp.ones((l, LANES), F32), jnp.zeros((l, LANES), F32), jnp.zeros((l, LANES), F32)


def kernel(x, c, ctx, c_ctx, w_mod, b_mod, g_mix_pre, g_mix_post, g_ffn_pre, g_ffn_post,
           w_in, w_out, na_rpb, gqa_q_norm, gqa_k_norm, diff_lambda_q1, diff_lambda_k1,
           diff_lambda_q2, diff_lambda_k2, diff_subln, ffn_w1, ffn_w3, ffn_w2,
           moe_router, moe_w1, moe_w3, moe_w2):
    b, l, d = x.shape
    cn = ctx.shape[1]
    depth = w_mod.shape[0]
    n_heads = d // HEAD_DIM
    n_na, n_gq, n_diff = n_heads // 4, n_heads // 2, n_heads // 4
    n_gkv = n_gq // 4
    group = n_gq // n_gkv
    dims = (n_na, n_gq, n_gkv, n_diff)

    rows_pad = -(-(b + 1) // 8) * 8
    cvec = jnp.zeros((rows_pad, d), F32).at[:b].set(c).at[b].set(c_ctx)
    mods = _modulation(cvec, w_mod, b_mod)
    bias_tab = _na_bias_table(na_rpb.reshape((depth * n_na,) + na_rpb.shape[2:]))
    bias_tab = bias_tab.reshape((depth, n_na) + bias_tab.shape[1:])

    tabs_g = _rope_tables(l, HEAD_DIM)
    tabs_d = _rope_tables(l, DIFF_QK_DIM)
    tabs_id = _identity_tables(cn)

    x_ctx = ctx
    moe_bf16, proj_bf16 = {}, {}
    for li in range(depth):
        last = li == depth - 1
        lam_init = 0.8 - 0.6 * math.exp(-0.3 * li)
        m6 = mods[li].reshape(rows_pad, 6, d)
        sh1, sc1, gt1, sh2, sc2, gt2 = [m6[:b, k][:, None, :] for k in range(6)]
        csh1, csc1, cgt1, csh2, csc2, cgt2 = [
            jnp.broadcast_to(m6[b, k][None, None, :], (b, 1, d)) for k in range(6)]
        vec = lambda a: a[li].reshape(1, -1)
        if li in proj_bf16:
            w_in_b, w_out_b = proj_bf16.pop(li)
        else:
            w_in_b, w_out_b = w_in[li].astype(BF16), w_out[li].astype(BF16)
        qn, kn = vec(gqa_q_norm), vec(gqa_k_norm)
        diff_args = (vec(diff_lambda_q1), vec(diff_lambda_k1), vec(diff_lambda_q2),
                     vec(diff_lambda_k2), vec(diff_subln), lam_init)

        (naq, nak, nav, gq, gk, gv, dq, dk, dv) = _in_proj(
            x, vec(g_mix_pre), sh1, sc1, w_in_b, qn, kn, tabs_g, tabs_d, dims)
        (naqc, nakc, navc, gqc, gkc, gvc, dqc, dkc, dvc) = _in_proj(
            x_ctx, vec(g_mix_pre), csh1, csc1, w_in_b, qn, kn, tabs_id, tabs_id, dims)

        casts_na, cast_names = [], []
        if li % 2 == 0:
            casts_na += [ffn_w1[li // 2], ffn_w3[li // 2], ffn_w2[li // 2]]
            cast_names += ["w1", "w3", "w2"]
        if li + 1 < depth:
            casts_na += [w_in[li + 1], w_out[li + 1]]
            cast_names += ["w_in", "w_out"]
        if not all(_na_cast_rows(a, b * n_na, l // GRID_W) for a in casts_na):
            casts_na, cast_names = [], []
        oa = _na_attention(naq, nak, nav, nakc, navc, bias_tab[li], casts=casts_na)
        na_bf16 = {}
        if casts_na:
            oa, *converted = oa
            na_bf16 = dict(zip(cast_names, converted))
        if "w_in" in na_bf16:
            proj_bf16[li + 1] = (na_bf16["w_in"], na_bf16["w_out"])
        tq_g, tq_d = _pick(l, (256, 128)), _pick(l, (256, 128))
        casts_g, casts_d = [], []
        if (li + 1) < depth and (li + 1) % 2 == 1:
            mi = (li + 1) // 2
            e_, d_, f_ = moe_w1.shape[1:]
            flat = [moe_w1[mi].reshape(e_ * d_, f_), moe_w3[mi].reshape(e_ * d_, f_),
                    moe_w2[mi].reshape(e_ * f_, d_)]
            steps_g, steps_d = b * n_gkv * (l // tq_g), b * n_diff * (l // tq_d)
            if _cast_rows(flat[0], steps_g) and _cast_rows(flat[1], steps_d) and _cast_rows(flat[2], steps_d):
                casts_g, casts_d = flat[:1], flat[1:]
        ob = _attention(gq, [(gk, gv), (gkc, gvc)], group=group, tq=tq_g, casts=casts_g)
        oc = _attention(dq, [(dk, dv), (dkc, dvc)], group=2, tq=tq_d, diff=diff_args, casts=casts_d)
        if casts_g:
            ob, w1b = ob
            oc, w3b, w2b = oc
            moe_bf16[li + 1] = (w1b.reshape(e_, d_, f_), w3b.reshape(e_, d_, f_), w2b.reshape(e_, f_, d_))

        routed = li % 2 == 1
        if routed:
            router = jnp.zeros((d, LANES), F32).at[:, :N_EXPERTS].set(moe_router[li // 2])
            x1, h2, route = _out_proj([oa, ob, oc], w_out_b, x, vec(g_mix_post), gt1,
                                      vec(g_ffn_pre), sh2, sc2, router=router)
        else:
            x1, h2 = _out_proj([oa, ob, oc], w_out_b, x, vec(g_mix_post), gt1,
                               vec(g_ffn_pre), sh2, sc2)

        if not last:
            tqc = _pick(cn, (256, 128))
            oac = _attention(naqc, [(nakc, navc)], group=1, tq=tqc)
            obc = _attention(gqc, [(gkc, gvc)], group=group, tq=_pick(cn, (128,)))
            occ = _attention(dqc, [(dkc, dvc)], group=2, tq=tqc, diff=diff_args)
            if routed:
                router = jnp.zeros((d, LANES), F32).at[:, :N_EXPERTS].set(moe_router[li // 2])
                xc1, h2c, route_c = _out_proj([oac, obc, occ], w_out_b, x_ctx, vec(g_mix_post),
                                              cgt1, vec(g_ffn_pre), csh2, csc2, router=router)
            else:
                xc1, h2c = _out_proj([oac, obc, occ], w_out_b, x_ctx, vec(g_mix_post), cgt1,
                                     vec(g_ffn_pre), csh2, csc2)

        if routed:
            if li in moe_bf16:
                w1, w3, w2 = moe_bf16.pop(li)
            else:
                w1, w3, w2 = [w[li // 2].astype(BF16) for w in (moe_w1, moe_w3, moe_w2)]
            x = _moe_ffn(h2, route, w1, w3, w2, x1, vec(g_ffn_post), gt2)
            if not last:
                x_ctx = _moe_ffn(h2c, route_c, w1, w3, w2, xc1, vec(g_ffn_post), cgt2)
        else:
            if "w1" in na_bf16:
                w1, w3, w2 = na_bf16["w1"], na_bf16["w3"], na_bf16["w2"]
            else:
                w1, w3, w2 = [w[li // 2].astype(BF16) for w in (ffn_w1, ffn_w3, ffn_w2)]
            x = _ffn_dense(h2, w1, w3, w2, x1, vec(g_ffn_post), gt2)
            if not last:
                x_ctx = _ffn_dense(h2c.reshape(1, b * cn, d), w1, w3, w2, xc1.reshape(1, b * cn, d),
                                   vec(g_ffn_post), cgt2[:1]).reshape(b, cn, d)
    return x
```

```python
import functools
import math

import jax
import jax.numpy as jnp
from jax import lax
from jax.experimental import pallas as pl
from jax.experimental.pallas import tpu as pltpu

GRID_W = 64
HEAD_DIM = 128
DIFF_QK_DIM = HEAD_DIM // 2
NA_ROWS = 8
NA_COLS = 16
ROPE_THETA = 10000.0
N_EXPERTS = 8
TOP_K = 2
EPS = 1e-6

LANES = 128
MXU_WIDTH = 256
NEG_BIG = -1e30
LOG2E = math.log2(math.e)
VMEM_LIMIT = 56 * 1024 * 1024

BF16 = jnp.bfloat16
F32 = jnp.float32
NT_DIMS = (((1,), (1,)), ((), ()))


def _cparams(sem):
    return pltpu.CompilerParams(dimension_semantics=sem, vmem_limit_bytes=VMEM_LIMIT)


def _pick(n, cands):
    for c in cands:
        if n % c == 0:
            return c
    raise ValueError(f"no tile in {cands} divides {n}")


def _rms(y):
    return y * lax.rsqrt(jnp.mean(y * y, axis=-1, keepdims=True) + EPS)


def _mod_kernel(c_ref, w_ref, b_ref, o_ref):
    c = c_ref[...]
    s = c / (1.0 + jnp.exp(-c))
    w = w_ref[0]
    s_hi = s.astype(BF16)
    s_lo = (s - s_hi.astype(F32)).astype(BF16)
    w_hi = w.astype(BF16)
    w_lo = (w - w_hi.astype(F32)).astype(BF16)
    o_ref[0] = (jnp.dot(s_hi, w_hi, preferred_element_type=F32)
                + jnp.dot(s_hi, w_lo, preferred_element_type=F32)
                + jnp.dot(s_lo, w_hi, preferred_element_type=F32)) + b_ref[0]


def _modulation(cvec, w_mod, b_mod):
    depth, d, n = w_mod.shape
    r = cvec.shape[0]
    tn = _pick(n, (1024, 512, 256, 128))
    return pl.pallas_call(
        _mod_kernel,
        out_shape=jax.ShapeDtypeStruct((depth, r, n), F32),
        grid=(depth, n // tn),
        in_specs=[pl.BlockSpec((r, d), lambda l, j: (0, 0)),
                  pl.BlockSpec((1, d, tn), lambda l, j: (l, 0, j)),
                  pl.BlockSpec((1, 1, tn), lambda l, j: (l, 0, j))],
        out_specs=pl.BlockSpec((1, r, tn), lambda l, j: (l, 0, j)),
        compiler_params=_cparams(("parallel", "parallel")),
        name="adaln_mod",
    )(cvec, w_mod, b_mod.reshape(depth, 1, n))


def _na_bias_kernel(rpb_ref, o_ref):
    n_dc = 2 * NA_COLS - 1
    base = pl.program_id(0) * ((2 * NA_ROWS - 1) * n_dc)
    shape = (GRID_W, LANES)
    qc = lax.broadcasted_iota(jnp.int32, shape, 0)
    lane = lax.broadcasted_iota(jnp.int32, shape, 1)
    half = lane >> 6
    kc = lane & (GRID_W - 1)
    dc = jnp.clip(kc - qc, -(NA_COLS - 1), NA_COLS - 1) + (NA_COLS - 1)
    cs = jnp.clip(qc - NA_COLS // 2, 0, GRID_W - NA_COLS)
    ok = (kc >= cs) & (kc < cs + NA_COLS)
    code = jnp.where(ok, half * 32 + dc, -1)
    for d in range(2 * NA_ROWS - 2):
        acc = jnp.full(shape, NEG_BIG, F32)
        for hf in range(2):
            for j in range(n_dc):
                val = rpb_ref[base + (d + hf) * n_dc + j] * LOG2E
                acc = jnp.where(code == hf * 32 + j, val, acc)
        o_ref[0, d] = acc


def _na_bias_table(rpb):
    nh = rpb.shape[0]
    return pl.pallas_call(
        _na_bias_kernel,
        out_shape=jax.ShapeDtypeStruct((nh, 2 * NA_ROWS - 2, GRID_W, LANES), F32),
        grid=(nh,),
        in_specs=[pl.BlockSpec(memory_space=pltpu.SMEM)],
        out_specs=pl.BlockSpec((1, 2 * NA_ROWS - 2, GRID_W, LANES), lambda h: (h, 0, 0, 0)),
        compiler_params=_cparams(("parallel",)),
        name="na_bias_table",
    )(rpb.reshape(-1))


def _rope(y, cos, sin_hi, sin_lo, dist):
    return (y * cos + pltpu.roll(y, LANES - dist, 1) * sin_hi
            + pltpu.roll(y, dist, 1) * sin_lo)


def _in_proj_kernel(x_ref, g_ref, sh_ref, sc_ref, w_ref, qn_ref, kn_ref,
                    cg_ref, sga_ref, sgb_ref, cd_ref, sda_ref, sdb_ref,
                    naq_ref, nak_ref, nav_ref, gq_ref, gk_ref, gv_ref,
                    dq_ref, dk_ref, dv_ref, *, n_na, n_gq, n_gkv, n_diff):
    x = x_ref[0]
    h = _rms(x) * g_ref[...]
    h = h * (1.0 + sc_ref[0]) + sh_ref[0]
    hb = h.astype(BF16)

    def proj(c0, nheads):
        return jnp.dot(hb, w_ref[:, c0:c0 + nheads * HEAD_DIM], preferred_element_type=F32)

    def heads(y, nheads):
        return [y[:, i * HEAD_DIM:(i + 1) * HEAD_DIM] for i in range(nheads)]

    na_scale = HEAD_DIM ** -0.5 * LOG2E
    diff_scale = DIFF_QK_DIM ** -0.5 * LOG2E
    c0 = 0
    for i, y in enumerate(heads(proj(c0, n_na), n_na)):
        naq_ref[0, i] = (y * na_scale).astype(BF16)
    c0 += n_na * HEAD_DIM
    for i, y in enumerate(heads(proj(c0, n_na), n_na)):
        nak_ref[0, i] = y.astype(BF16)
    c0 += n_na * HEAD_DIM
    lane = lax.broadcasted_iota(jnp.int32, (x.shape[0], LANES), 1)
    ones_col = jnp.where(lane == 0, 1.0, 0.0).astype(BF16)
    for i, y in enumerate(heads(proj(c0, n_na), n_na)):
        nav_ref[0, i, :, :HEAD_DIM] = y.astype(BF16)
        nav_ref[0, i, :, HEAD_DIM:] = ones_col
    c0 += n_na * HEAD_DIM

    cg, sga, sgb = cg_ref[...], sga_ref[...], sgb_ref[...]
    for i, y in enumerate(heads(proj(c0, n_gq), n_gq)):
        y = _rope(_rms(y) * qn_ref[...], cg, sga, sgb, 32)
        gq_ref[0, i] = (y * na_scale).astype(BF16)
    c0 += n_gq * HEAD_DIM
    for i, y in enumerate(heads(proj(c0, n_gkv), n_gkv)):
        y = _rope(_rms(y) * kn_ref[...], cg, sga, sgb, 32)
        gk_ref[0, i] = y.astype(BF16)
    c0 += n_gkv * HEAD_DIM
    for i, y in enumerate(heads(proj(c0, n_gkv), n_gkv)):
        gv_ref[0, i, :, :HEAD_DIM] = y.astype(BF16)
        gv_ref[0, i, :, HEAD_DIM:] = ones_col
    c0 += n_gkv * HEAD_DIM

    cd, sda, sdb = cd_ref[...], sda_ref[...], sdb_ref[...]
    for i, y in enumerate(heads(proj(c0, n_diff), n_diff)):
        y = _rope(y, cd, sda, sdb, 16) * diff_scale
        dq_ref[0, 2 * i] = jnp.where(lane < DIFF_QK_DIM, y, 0.0).astype(BF16)
        dq_ref[0, 2 * i + 1] = jnp.where(lane >= DIFF_QK_DIM, y, 0.0).astype(BF16)
    c0 += n_diff * HEAD_DIM
    for i, y in enumerate(heads(proj(c0, n_diff), n_diff)):
        dk_ref[0, i] = _rope(y, cd, sda, sdb, 16).astype(BF16)
    c0 += n_diff * HEAD_DIM
    for i, y in enumerate(heads(proj(c0, n_diff), n_diff)):
        dv_ref[0, i, :, :HEAD_DIM] = y.astype(BF16)
        dv_ref[0, i, :, HEAD_DIM:] = ones_col


def _in_proj(x, g, sh, sc, w_in, qn, kn, tabs_g, tabs_d, dims):
    b, l, d = x.shape
    n_na, n_gq, n_gkv, n_diff = dims
    n_in = w_in.shape[1]
    tm = _pick(l, (256, 128))
    vec = pl.BlockSpec((1, d), lambda bi, i: (0, 0))
    mod = pl.BlockSpec((1, 1, d), lambda bi, i: (bi, 0, 0))
    hvec = pl.BlockSpec((1, HEAD_DIM), lambda bi, i: (0, 0))
    tab = pl.BlockSpec((tm, LANES), lambda bi, i: (i, 0))

    def out(nh, width=HEAD_DIM):
        return (jax.ShapeDtypeStruct((b, nh, l, width), BF16),
                pl.BlockSpec((1, nh, tm, width), lambda bi, i: (bi, 0, i, 0)))

    outs = [out(n_na), out(n_na), out(n_na, 2 * HEAD_DIM), out(n_gq), out(n_gkv),
            out(n_gkv, 2 * HEAD_DIM), out(2 * n_diff), out(n_diff), out(n_diff, 2 * HEAD_DIM)]
    kern = functools.partial(_in_proj_kernel, n_na=n_na, n_gq=n_gq, n_gkv=n_gkv, n_diff=n_diff)
    return pl.pallas_call(
        kern,
        out_shape=[o[0] for o in outs],
        grid=(b, l // tm),
        in_specs=[pl.BlockSpec((1, tm, d), lambda bi, i: (bi, i, 0)), vec, mod, mod,
                  pl.BlockSpec((d, n_in), lambda bi, i: (0, 0), pipeline_mode=pl.Buffered(1)),
                  hvec, hvec, tab, tab, tab, tab, tab, tab],
        out_specs=[o[1] for o in outs],
        compiler_params=_cparams(("parallel", "parallel")),
        name="in_proj",
    )(x, g, sh, sc, w_in, qn, kn, *tabs_g, *tabs_d)


def _attn_kernel(*refs, n_seg, group, tq, chunks, diff, lam_init, n_cast):
    q_ref = refs[0]
    kv_refs = refs[1:1 + 2 * n_seg]
    rest = list(refs[1 + 2 * n_seg:])
    cast_out = [rest.pop() for _ in range(n_cast)][::-1]
    o_ref = rest.pop()
    cast_in = [rest.pop() for _ in range(n_cast)][::-1]
    for src_ref, dst_ref in zip(cast_in, cast_out):
        dst_ref[...] = src_ref[...].astype(BF16)
    if diff:
        lq1_ref, lk1_ref, lq2_ref, lk2_ref, sub_ref = rest
    m_rows = group * tq

    def step(k, v, carry):
        m, acc = carry
        q = q_ref[0].reshape(m_rows, HEAD_DIM)
        s = lax.dot_general(q, k, NT_DIMS, preferred_element_type=F32)
        m_new = jnp.maximum(m, jnp.max(s, axis=-1, keepdims=True))
        p = jnp.exp2((s - m_new).astype(BF16))
        acc = jnp.exp2(m - m_new) * acc + jnp.dot(p, v, preferred_element_type=F32)
        return m_new, acc

    carry = (jnp.full((m_rows, 1), NEG_BIG, F32), jnp.zeros((m_rows, 2 * HEAD_DIM), F32))
    for si in range(n_seg):
        k_ref, v_ref = kv_refs[2 * si], kv_refs[2 * si + 1]
        n_chunks, tk = chunks[si]
        for c in range(n_chunks):
            rows = slice(c * tk, (c + 1) * tk)
            carry = step(k_ref[0, 0, rows, :], v_ref[0, 0, rows, :], carry)
    _, acc = carry
    o = acc[:, :HEAD_DIM] / acc[:, HEAD_DIM:HEAD_DIM + 1]
    if diff:
        lam = (jnp.exp(jnp.sum(lq1_ref[...] * lk1_ref[...], axis=-1, keepdims=True))
               - jnp.exp(jnp.sum(lq2_ref[...] * lk2_ref[...], axis=-1, keepdims=True)) + lam_init)
        od = o[:tq] - lam * o[tq:]
        o_ref[0] = (_rms(od) * sub_ref[...] * (1.0 - lam_init)).astype(BF16)
    else:
        for g in range(group):
            o_ref[0, :, g * HEAD_DIM:(g + 1) * HEAD_DIM] = o[g * tq:(g + 1) * tq].astype(BF16)


def _cast_rows(a, n_steps):
    rows = a.shape[0] // n_steps
    return rows if rows * n_steps == a.shape[0] and rows % 16 == 0 else 0


def _attention(q, segs, *, group, tq, diff=None, casts=()):
    b, hq, lq, _ = q.shape
    hkv = hq // group
    nq = lq // tq
    out_heads = 1 if diff is not None else group
    kv_args, kv_specs, chunks = [], [], []
    for k, v in segs:
        s = k.shape[2]
        tk = _pick(s, (512, 256, 128) if diff is not None else (256, 128))
        chunks.append((s // tk, tk))
        kv_args += [k, v]
        kv_specs += [pl.BlockSpec((1, 1, s, a.shape[3]), lambda bi, h, i: (bi, h, 0, 0))
                     for a in (k, v)]
    extra_args, extra_specs, lam_init = [], [], 0.0
    if diff is not None:
        lq1, lk1, lq2, lk2, subln, lam_init = diff
        extra_args = [lq1, lk1, lq2, lk2, subln]
        extra_specs = [pl.BlockSpec((1, a.shape[1]), lambda bi, h, i: (0, 0)) for a in extra_args]
    cast_specs = [pl.BlockSpec((_cast_rows(a, b * hkv * nq), a.shape[1]),
                               lambda bi, h, i: ((bi * hkv + h) * nq + i, 0)) for a in casts]
    kern = functools.partial(_attn_kernel, n_seg=len(segs), group=group, tq=tq,
                             chunks=tuple(chunks), diff=diff is not None, lam_init=lam_init,
                             n_cast=len(casts))
    outs = pl.pallas_call(
        kern,
        out_shape=[jax.ShapeDtypeStruct((b, lq, hkv * out_heads * HEAD_DIM), BF16)]
                  + [jax.ShapeDtypeStruct(a.shape, BF16) for a in casts],
        grid=(b, hkv, nq),
        in_specs=[pl.BlockSpec((1, group, tq, HEAD_DIM), lambda bi, h, i: (bi, h, i, 0))]
                 + kv_specs + extra_specs + cast_specs,
        out_specs=[pl.BlockSpec((1, tq, out_heads * HEAD_DIM), lambda bi, h, i: (bi, i, h))] + cast_specs,
        compiler_params=_cparams(("parallel", "parallel", "parallel")),
        name="diff_attn" if diff is not None else "gqa_attn",
    )(q, *kv_args, *extra_args, *casts)
    return outs[0] if not casts else outs


NA_ROWS_PER_TRIP = 64


def _na_kernel(q_ref, k_ref, v_ref, kc_ref, vc_ref, bias_ref, *rest, rows, n_cast):
    cast_in, o_ref, cast_out = rest[:n_cast], rest[n_cast], rest[n_cast + 1:]
    band = NA_ROWS * GRID_W
    kc = kc_ref[0, 0]
    vc = vc_ref[0, 0]

    def body(r):
        rs = jnp.clip(r - NA_ROWS // 2, 0, rows - NA_ROWS)
        off = rs - r + (NA_ROWS - 1)
        qrows = pl.ds(pl.multiple_of(r * GRID_W, GRID_W), GRID_W)
        krows = pl.ds(pl.multiple_of(rs * GRID_W, GRID_W), band)
        q = q_ref[0, 0, qrows, :]
        bias = jnp.concatenate([bias_ref[0, off + 2 * j] for j in range(NA_ROWS // 2)], axis=1)
        s = lax.dot_general(q, k_ref[0, 0, krows, :], NT_DIMS, preferred_element_type=F32) + bias
        sc = lax.dot_general(q, kc, NT_DIMS, preferred_element_type=F32)
        m = jnp.maximum(jnp.max(s, axis=-1, keepdims=True), jnp.max(sc, axis=-1, keepdims=True))
        p = jnp.exp2((s - m).astype(BF16))
        pc = jnp.exp2((sc - m).astype(BF16))
        o = (jnp.dot(p, v_ref[0, 0, krows, :], preferred_element_type=F32)
             + jnp.dot(pc, vc, preferred_element_type=F32))
        o_ref[0, qrows, :] = (o[:, :HEAD_DIM] / o[:, HEAD_DIM:HEAD_DIM + 1]).astype(BF16)

    per_trip = min(NA_ROWS_PER_TRIP, rows)
    n_trips = rows // per_trip

    def trip(t, _):
        for src_ref, dst_ref in zip(cast_in, cast_out):
            n = src_ref.shape[0] // n_trips
            part = pl.ds(pl.multiple_of(t * n, n), n)
            dst_ref[part, :] = src_ref[part, :].astype(BF16)
        for i in range(per_trip):
            body(t * per_trip + i)
        return 0

    lax.fori_loop(0, n_trips, trip, 0)


def _na_cast_rows(a, n_steps, rows):
    n_trips = rows // min(NA_ROWS_PER_TRIP, rows)
    per_step = _cast_rows(a, n_steps)
    return per_step if per_step and per_step % (16 * n_trips) == 0 else 0


def _na_attention(q, k, v, kc, vc, bias, casts=()):
    b, nh, l, _ = q.shape
    rows = l // GRID_W
    cast_specs = [pl.BlockSpec((_na_cast_rows(a, b * nh, rows), a.shape[1]),
                               lambda bi, h: (bi * nh + h, 0)) for a in casts]

    def full(a):
        return pl.BlockSpec((1, 1) + a.shape[2:], lambda bi, h: (bi, h, 0, 0))

    outs = pl.pallas_call(
        functools.partial(_na_kernel, rows=rows, n_cast=len(casts)),
        out_shape=[jax.ShapeDtypeStruct((b, l, nh * HEAD_DIM), BF16)]
                  + [jax.ShapeDtypeStruct(a.shape, BF16) for a in casts],
        grid=(b, nh),
        in_specs=[full(q), full(k), full(v), full(kc), full(vc),
                  pl.BlockSpec((1,) + bias.shape[1:], lambda bi, h: (h, 0, 0, 0))] + cast_specs,
        out_specs=[pl.BlockSpec((1, l, HEAD_DIM), lambda bi, h: (bi, 0, h))] + cast_specs,
        compiler_params=_cparams(("parallel", "parallel")),
        name="na_attn",
    )(q, k, v, kc, vc, bias, *casts)
    return outs[0] if not casts else outs


def _out_proj_kernel(*refs, n_o, routed):
    o_refs = refs[:n_o]
    (w_ref, x_ref, gpost_ref, gt_ref, gpre_ref, sh_ref, sc_ref) = refs[n_o:n_o + 7]
    rest = refs[n_o + 7:]
    if routed:
        router_ref, x1_ref, h2_ref, route_ref = rest
    else:
        x1_ref, h2_ref = rest
    tm = x_ref.shape[1]
    sub = min(tm, 256)
    for r0 in range(0, tm, sub):
        rows = slice(r0, r0 + sub)
        acc = None
        c0 = 0
        for o_ref in o_refs:
            width = o_ref.shape[-1]
            part = jnp.dot(o_ref[0, rows, :], w_ref[c0:c0 + width, :], preferred_element_type=F32)
            acc = part if acc is None else acc + part
            c0 += width
        x1 = x_ref[0, rows, :] + gt_ref[0] * (_rms(acc) * gpost_ref[...])
        x1_ref[0, rows, :] = x1
        h2 = (_rms(x1) * gpre_ref[...]) * (1.0 + sc_ref[0]) + sh_ref[0]
        h2_ref[0, rows, :] = h2.astype(h2_ref.dtype)
        if routed:
            h_hi = h2.astype(BF16)
            h_lo = (h2 - h_hi.astype(F32)).astype(BF16)
            r = router_ref[...]
            r_hi = r.astype(BF16)
            r_lo = (r - r_hi.astype(F32)).astype(BF16)
            logits = (jnp.dot(h_hi, r_hi, preferred_element_type=F32)
                      + jnp.dot(h_hi, r_lo, preferred_element_type=F32)
                      + jnp.dot(h_lo, r_hi, preferred_element_type=F32))
            lane = lax.broadcasted_iota(jnp.int32, logits.shape, 1)
            logits = jnp.where(lane < N_EXPERTS, logits, NEG_BIG)
            v1 = jnp.max(logits, axis=-1, keepdims=True)
            i1 = jnp.min(jnp.where(logits == v1, lane, LANES), axis=-1, keepdims=True)
            rest_l = jnp.where(lane == i1, NEG_BIG, logits)
            v2 = jnp.max(rest_l, axis=-1, keepdims=True)
            i2 = jnp.min(jnp.where(rest_l == v2, lane, LANES), axis=-1, keepdims=True)
            e2 = jnp.exp(v2 - v1)
            g1 = 1.0 / (1.0 + e2)
            g2 = e2 / (1.0 + e2)
            route_ref[0, rows, :] = jnp.where(lane == 0, i1.astype(F32),
                                    jnp.where(lane == 1, i2.astype(F32),
                                    jnp.where(lane == 2, g1, jnp.where(lane == 3, g2, 0.0))))


def _out_proj(o_list, w_out, x, gpost, gt, gpre, sh, sc, router=None):
    b, l, d = x.shape
    tm = _pick(l, (512, 256, 128))
    routed = router is not None
    vec = pl.BlockSpec((1, d), lambda bi, i: (0, 0))
    mod = pl.BlockSpec((1, 1, d), lambda bi, i: (bi, 0, 0))
    row = pl.BlockSpec((1, tm, d), lambda bi, i: (bi, i, 0))
    in_specs = [pl.BlockSpec((1, tm, o.shape[-1]), lambda bi, i: (bi, i, 0)) for o in o_list]
    in_specs += [pl.BlockSpec(w_out.shape, lambda bi, i: (0, 0), pipeline_mode=pl.Buffered(1)),
                 row, vec, mod, vec, mod, mod]
    args = list(o_list) + [w_out, x, gpost, gt, gpre, sh, sc]
    out_shape = [jax.ShapeDtypeStruct((b, l, d), F32),
                 jax.ShapeDtypeStruct((b, l, d), F32 if routed else BF16)]
    out_specs = [row, row]
    if routed:
        in_specs.append(pl.BlockSpec(router.shape, lambda bi, i: (0, 0)))
        args.append(router)
        out_shape.append(jax.ShapeDtypeStruct((b, l, LANES), F32))
        out_specs.append(pl.BlockSpec((1, tm, LANES), lambda bi, i: (bi, i, 0)))
    return pl.pallas_call(
        functools.partial(_out_proj_kernel, n_o=len(o_list), routed=routed),
        out_shape=out_shape,
        grid=(b, l // tm),
        in_specs=in_specs,
        out_specs=out_specs,
        compiler_params=_cparams(("parallel", "parallel")),
        name="out_proj",
    )(*args)


def _swiglu_step(hb, w1_ref, w3_ref, w2_ref):
    out = None
    for c0 in range(0, w1_ref.shape[-1], MXU_WIDTH):
        cols = slice(c0, c0 + MXU_WIDTH)
        a = jnp.dot(hb, w1_ref[:, cols], preferred_element_type=F32)
        g = jnp.dot(hb, w3_ref[:, cols], preferred_element_type=F32)
        act = (a / (1.0 + jnp.exp(-a))) * g
        part = jnp.dot(act.astype(BF16), w2_ref[cols, :], preferred_element_type=F32)
        out = part if out is None else out + part
    return out


def _ffn_dense_kernel(h_ref, w1_ref, w3_ref, w2_ref, x_ref, gpost_ref, gt_ref, o_ref, acc_ref):
    f = pl.program_id(2)

    @pl.when(f == 0)
    def _():
        acc_ref[...] = jnp.zeros_like(acc_ref)

    acc_ref[...] += _swiglu_step(h_ref[0], w1_ref, w3_ref, w2_ref)

    @pl.when(f == pl.num_programs(2) - 1)
    def _():
        o_ref[0] = x_ref[0] + gt_ref[0] * (_rms(acc_ref[...]) * gpost_ref[...])


def _ff_tile(dff):
    return _pick(dff, (512, 256))


def _ffn_dense(h2, w1, w3, w2, x1, gpost, gt):
    b, l, d = x1.shape
    dff = w1.shape[1]
    tm = _pick(l, (512, 256, 128))
    tf = _ff_tile(dff)
    row = pl.BlockSpec((1, tm, d), lambda bi, i, f: (bi, i, 0))
    return pl.pallas_call(
        _ffn_dense_kernel,
        out_shape=jax.ShapeDtypeStruct((b, l, d), F32),
        grid=(b, l // tm, dff // tf),
        in_specs=[row,
                  pl.BlockSpec((d, tf), lambda bi, i, f: (0, f)),
                  pl.BlockSpec((d, tf), lambda bi, i, f: (0, f)),
                  pl.BlockSpec((tf, d), lambda bi, i, f: (f, 0)),
                  row,
                  pl.BlockSpec((1, d), lambda bi, i, f: (0, 0)),
                  pl.BlockSpec((1, 1, d), lambda bi, i, f: (bi, 0, 0))],
        out_specs=row,
        scratch_shapes=[pltpu.VMEM((tm, d), F32)],
        compiler_params=_cparams(("parallel", "parallel", "arbitrary")),
        name="ffn_dense",
    )(h2, w1, w3, w2, x1, gpost, gt)


def _ffn_expert_kernel(tile_e_ref, nact_ref, src0_ref, srcn_ref, h_hbm, w1_ref, w3_ref, w2_ref,
                       o_ref, xbuf_ref, hb_ref, acc_ref, sem, *, tm):
    j = pl.program_id(0)
    f = pl.program_id(1)
    nact = nact_ref[0]
    slot = j % 2

    def gather(idx_ref, dst_slot):
        def issue(pair, _):
            for prio in range(2):
                i = 2 * pair + prio
                pltpu.make_async_copy(h_hbm.at[pl.ds(idx_ref[i], 1)],
                                      xbuf_ref.at[dst_slot, pl.ds(i, 1)],
                                      sem.at[dst_slot]).start(priority=prio)
            return 0
        lax.fori_loop(0, tm // 2, issue, 0, unroll=4)

    @pl.when(j < nact)
    def _():
        @pl.when(f == 0)
        def _():
            @pl.when(j == 0)
            def _():
                gather(src0_ref, 0)

            pltpu.make_async_copy(h_hbm.at[pl.ds(0, tm)], xbuf_ref.at[slot], sem.at[slot]).wait()
            hb_ref[...] = xbuf_ref[slot].astype(BF16)
            acc_ref[...] = jnp.zeros_like(acc_ref)

        @pl.when((f == 1) & (j + 1 < nact))
        def _():
            gather(srcn_ref, 1 - slot)

        acc_ref[...] += _swiglu_step(hb_ref[...], w1_ref.at[0], w3_ref.at[0], w2_ref.at[0])

        @pl.when(f == pl.num_programs(1) - 1)
        def _():
            o_ref[...] = acc_ref[...]

    @pl.when((j >= nact_ref[0]) & (f == pl.num_programs(1) - 1))
    def _():
        o_ref[...] = jnp.zeros_like(o_ref)


def _ffn_expert(h2, src, w1, w3, w2, tile_e, nact, tm):
    d = h2.shape[1]
    p = src.shape[0]
    n_tiles = p // tm
    tf = _ff_tile(w1.shape[2])
    nf = w1.shape[2] // tf
    assert nf >= 2

    def fsel(j, f, na):
        return jnp.where(j < na[0], f, nf - 1)

    return pl.pallas_call(
        functools.partial(_ffn_expert_kernel, tm=tm),
        out_shape=jax.ShapeDtypeStruct((p, d), F32),
        grid_spec=pltpu.PrefetchScalarGridSpec(
            num_scalar_prefetch=2,
            grid=(n_tiles, nf),
            in_specs=[pl.BlockSpec((tm,), lambda j, f, te, na: (0,), memory_space=pltpu.SMEM),
                      pl.BlockSpec((tm,), lambda j, f, te, na: (jnp.minimum(j + 1, n_tiles - 1),),
                                   memory_space=pltpu.SMEM),
                      pl.BlockSpec(memory_space=pl.ANY),
                      pl.BlockSpec((1, d, tf), lambda j, f, te, na: (te[j], 0, fsel(j, f, na))),
                      pl.BlockSpec((1, d, tf), lambda j, f, te, na: (te[j], 0, fsel(j, f, na))),
                      pl.BlockSpec((1, tf, d), lambda j, f, te, na: (te[j], fsel(j, f, na), 0))],
            out_specs=pl.BlockSpec((tm, d), lambda j, f, te, na: (j, 0)),
            scratch_shapes=[pltpu.VMEM((2, tm, d), F32), pltpu.VMEM((tm, d), BF16),
                            pltpu.VMEM((tm, d), F32), pltpu.SemaphoreType.DMA((2,))]),
        compiler_params=_cparams(("arbitrary", "arbitrary")),
        name="ffn_expert",
    )(tile_e, nact, src, src, h2, w1, w3, w2)


def _rank_kernel(route_ref, rank_ref, cnt_ref, carry_ref):
    i = pl.program_id(0)

    @pl.when(i == 0)
    def _():
        carry_ref[...] = jnp.zeros_like(carry_ref)

    r = route_ref[...]
    t = r.shape[0]
    lane = lax.broadcasted_iota(jnp.int32, r.shape, 1)
    e1 = r[:, 0:1].astype(jnp.int32)
    e2 = r[:, 1:2].astype(jnp.int32)
    oh1 = (lane == e1).astype(F32)
    oh2 = (lane == e2).astype(F32)
    c = (oh1 + oh2).astype(BF16)
    row = lax.broadcasted_iota(jnp.int32, (t, t), 0)
    col = lax.broadcasted_iota(jnp.int32, (t, t), 1)
    tri = (col < row).astype(F32).astype(BF16)
    prefix = jnp.dot(tri, c, preferred_element_type=F32) + carry_ref[0:1, :]
    r1 = jnp.sum(prefix * oh1, axis=-1, keepdims=True)
    r2 = jnp.sum(prefix * oh2, axis=-1, keepdims=True)
    rank_ref[...] = jnp.where(lane == 0, r1, jnp.where(lane == 1, r2, 0.0))
    total = carry_ref[0:1, :] + jnp.sum(oh1 + oh2, axis=0, keepdims=True)
    carry_ref[...] = jnp.broadcast_to(total, carry_ref.shape)
    cnt_ref[...] = jnp.broadcast_to(total, cnt_ref.shape)


def _expert_ranks(route):
    n = route.shape[0]
    t = _pick(n, (512, 256, 128))
    return pl.pallas_call(
        _rank_kernel,
        out_shape=[jax.ShapeDtypeStruct((n, LANES), F32), jax.ShapeDtypeStruct((8, LANES), F32)],
        grid=(n // t,),
        in_specs=[pl.BlockSpec((t, LANES), lambda i: (i, 0))],
        out_specs=[pl.BlockSpec((t, LANES), lambda i: (i, 0)),
                   pl.BlockSpec((8, LANES), lambda i: (0, 0))],
        scratch_shapes=[pltpu.VMEM((8, LANES), F32)],
        compiler_params=_cparams(("arbitrary",)),
        name="expert_ranks",
    )(route)


def _combine_kernel(pos0_ref, posn_ref, y_hbm, route_ref, x_ref, gpost_ref, gt_ref, o_ref,
                    buf_ref, sem, *, tt):
    n_steps = pl.num_programs(0) * pl.num_programs(1)
    step = pl.program_id(0) * pl.num_programs(1) + pl.program_id(1)
    slot = step % 2

    def gather(p_ref, dst_slot):
        def issue(i, _):
            for k in range(TOP_K):
                pltpu.make_async_copy(y_hbm.at[pl.ds(p_ref[TOP_K * i + k], 1)],
                                      buf_ref.at[dst_slot, pl.ds(k * tt + i, 1)],
                                      sem.at[dst_slot]).start(priority=k)
            return 0
        lax.fori_loop(0, tt, issue, 0, unroll=4)

    @pl.when(step == 0)
    def _():
        gather(pos0_ref, 0)

    @pl.when(step + 1 < n_steps)
    def _():
        gather(posn_ref, 1 - slot)

    pltpu.make_async_copy(y_hbm.at[pl.ds(0, TOP_K * tt)], buf_ref.at[slot], sem.at[slot]).wait()
    r = route_ref[0]
    y = r[:, 2:3] * buf_ref[slot, 0:tt, :] + r[:, 3:4] * buf_ref[slot, tt:2 * tt, :]
    o_ref[0] = x_ref[0] + gt_ref[0] * (_rms(y) * gpost_ref[...])


def _combine(y, pos_flat, route, x1, gpost, gt):
    b, l, d = x1.shape
    tt = _pick(l, (256, 128))
    nt = l // tt
    n_steps = b * nt
    row = pl.BlockSpec((1, tt, d), lambda bi, i: (bi, i, 0))
    return pl.pallas_call(
        functools.partial(_combine_kernel, tt=tt),
        out_shape=jax.ShapeDtypeStruct((b, l, d), F32),
        grid=(b, nt),
        in_specs=[pl.BlockSpec((TOP_K * tt,), lambda bi, i: (0,), memory_space=pltpu.SMEM),
                  pl.BlockSpec((TOP_K * tt,),
                               lambda bi, i: (jnp.minimum(bi * nt + i + 1, n_steps - 1),),
                               memory_space=pltpu.SMEM),
                  pl.BlockSpec(memory_space=pl.ANY),
                  pl.BlockSpec((1, tt, LANES), lambda bi, i: (bi, i, 0)),
                  row,
                  pl.BlockSpec((1, d), lambda bi, i: (0, 0)),
                  pl.BlockSpec((1, 1, d), lambda bi, i: (bi, 0, 0))],
        out_specs=row,
        scratch_shapes=[pltpu.VMEM((2, TOP_K * tt, d), F32), pltpu.SemaphoreType.DMA((2,))],
        compiler_params=_cparams(("arbitrary", "arbitrary")),
        name="moe_combine",
    )(pos_flat, pos_flat, y, route, x1, gpost, gt)


def _moe_ffn(h2, route, w1, w3, w2, x1, gpost, gt):
    b, l, d = x1.shape
    n = b * l
    tm = 512
    route2 = route.reshape(n, LANES)
    rank, counts = _expert_ranks(route2)
    cnt = counts[0, :N_EXPERTS].astype(jnp.int32)
    tiles = (cnt + tm - 1) // tm
    tile_end = jnp.cumsum(tiles)
    row_off = (tile_end - tiles) * tm
    e_idx = route2[:, :TOP_K].astype(jnp.int32)
    pos = (row_off[e_idx] + rank[:, :TOP_K].astype(jnp.int32)).reshape(-1)
    n_tiles = (n * TOP_K) // tm + N_EXPERTS
    nact = tile_end[-1:]
    tile_ids = jnp.minimum(jnp.arange(n_tiles, dtype=jnp.int32), nact[0] - 1)
    tile_e = jnp.sum((tile_ids[:, None] >= tile_end[None, :]).astype(jnp.int32), axis=1)
    token = jnp.arange(n * TOP_K, dtype=jnp.int32) // TOP_K
    src = jnp.zeros((n_tiles * tm,), jnp.int32).at[pos].set(
        token, unique_indices=True, mode="promise_in_bounds")
    y = _ffn_expert(h2.reshape(n, d), src, w1, w3, w2, tile_e, nact.astype(jnp.int32), tm)
    return _combine(y, pos, route, x1, gpost, gt)


def _rope_tables(l, dims_per_map):
    lane = jnp.arange(LANES)
    d = lane % dims_per_map
    part_w = dims_per_map // 2
    half = part_w // 2
    is_col = (d // part_w) == 1
    first = (d % part_w) < half
    inv_freq = ROPE_THETA ** (-jnp.arange(half, dtype=F32) / half)
    freq = inv_freq[d % half]
    pos = jnp.arange(l)
    p = jnp.where(is_col[None, :], (pos % GRID_W)[:, None], (pos // GRID_W)[:, None]).astype(F32)
    ang = p * freq[None, :]
    cos, sin = jnp.cos(ang), jnp.sin(ang)
    zero = jnp.zeros_like(sin)
    return cos, jnp.where(first[None, :], -sin, zero), jnp.where(first[None, :], zero, sin)


def _identity_tables(l):
    return jnp.ones((l, LANES), F32), jnp.zeros((l, LANES), F32), jnp.zeros((l, LANES), F32)


def kernel(x, c, ctx, c_ctx, w_mod, b_mod, g_mix_pre, g_mix_post, g_ffn_pre, g_ffn_post,
           w_in, w_out, na_rpb, gqa_q_norm, gqa_k_norm, diff_lambda_q1, diff_lambda_k1,
           diff_lambda_q2, diff_lambda_k2, diff_subln, ffn_w1, ffn_w3, ffn_w2,
           moe_router, moe_w1, moe_w3, moe_w2):
    b, l, d = x.shape
    cn = ctx.shape[1]
    depth = w_mod.shape[0]
    n_heads = d // HEAD_DIM
    n_na, n_gq, n_diff = n_heads // 4, n_heads // 2, n_heads // 4
    n_gkv = n_gq // 4
    group = n_gq // n_gkv
    dims = (n_na, n_gq, n_gkv, n_diff)

    rows_pad = -(-(b + 1) // 8) * 8
    cvec = jnp.zeros((rows_pad, d), F32).at[:b].set(c).at[b].set(c_ctx)
    mods = _modulation(cvec, w_mod, b_mod)
    bias_tab = _na_bias_table(na_rpb.reshape((depth * n_na,) + na_rpb.shape[2:]))
    bias_tab = bias_tab.reshape((depth, n_na) + bias_tab.shape[1:])

    tabs_g = _rope_tables(l, HEAD_DIM)
    tabs_d = _rope_tables(l, DIFF_QK_DIM)
    tabs_id = _identity_tables(cn)

    x_ctx = ctx
    moe_bf16, proj_bf16 = {}, {}
    for li in range(depth):
        last = li == depth - 1
        lam_init = 0.8 - 0.6 * math.exp(-0.3 * li)
        m6 = mods[li].reshape(rows_pad, 6, d)
        sh1, sc1, gt1, sh2, sc2, gt2 = [m6[:b, k][:, None, :] for k in range(6)]
        csh1, csc1, cgt1, csh2, csc2, cgt2 = [
            jnp.broadcast_to(m6[b, k][None, None, :], (b, 1, d)) for k in range(6)]
        vec = lambda a: a[li].reshape(1, -1)
        if li in proj_bf16:
            w_in_b, w_out_b = proj_bf16.pop(li)
        else:
            w_in_b, w_out_b = w_in[li].astype(BF16), w_out[li].astype(BF16)
        qn, kn = vec(gqa_q_norm), vec(gqa_k_norm)
        diff_args = (vec(diff_lambda_q1), vec(diff_lambda_k1), vec(diff_lambda_q2),
                     vec(diff_lambda_k2), vec(diff_subln), lam_init)

        (naq, nak, nav, gq, gk, gv, dq, dk, dv) = _in_proj(
            x, vec(g_mix_pre), sh1, sc1, w_in_b, qn, kn, tabs_g, tabs_d, dims)
        (naqc, nakc, navc, gqc, gkc, gvc, dqc, dkc, dvc) = _in_proj(
            x_ctx, vec(g_mix_pre), csh1, csc1, w_in_b, qn, kn, tabs_id, tabs_id, dims)

        casts_na, cast_names = [], []
        if li % 2 == 0:
            casts_na += [ffn_w1[li // 2], ffn_w3[li // 2], ffn_w2[li // 2]]
            cast_names += ["w1", "w3", "w2"]
        if li + 1 < depth:
            casts_na += [w_in[li + 1], w_out[li + 1]]
            cast_names += ["w_in", "w_out"]
        if not all(_na_cast_rows(a, b * n_na, l // GRID_W) for a in casts_na):
            casts_na, cast_names = [], []
        oa = _na_attention(naq, nak, nav, nakc, navc, bias_tab[li], casts=casts_na)
        na_bf16 = {}
        if casts_na:
            oa, *converted = oa
            na_bf16 = dict(zip(cast_names, converted))
        if "w_in" in na_bf16:
            proj_bf16[li + 1] = (na_bf16["w_in"], na_bf16["w_out"])
        tq_g, tq_d = _pick(l, (256, 128)), _pick(l, (256, 128))
        casts_g, casts_d = [], []
        if (li + 1) < depth and (li + 1) % 2 == 1:
            mi = (li + 1) // 2
            e_, d_, f_ = moe_w1.shape[1:]
            flat = [moe_w1[mi].reshape(e_ * d_, f_), moe_w3[mi].reshape(e_ * d_, f_),
                    moe_w2[mi].reshape(e_ * f_, d_)]
            steps_g, steps_d = b * n_gkv * (l // tq_g), b * n_diff * (l // tq_d)
            if _cast_rows(flat[0], steps_g) and _cast_rows(flat[1], steps_d) and _cast_rows(flat[2], steps_d):
                casts_g, casts_d = flat[:1], flat[1:]
        ob = _attention(gq, [(gk, gv), (gkc, gvc)], group=group, tq=tq_g, casts=casts_g)
        oc = _attention(dq, [(dk, dv), (dkc, dvc)], group=2, tq=tq_d, diff=diff_args, casts=casts_d)
        if casts_g:
            ob, w1b = ob
            oc, w3b, w2b = oc
            moe_bf16[li + 1] = (w1b.reshape(e_, d_, f_), w3b.reshape(e_, d_, f_), w2b.reshape(e_, f_, d_))

        routed = li % 2 == 1
        if routed:
            router = jnp.zeros((d, LANES), F32).at[:, :N_EXPERTS].set(moe_router[li // 2])
            x1, h2, route = _out_proj([oa, ob, oc], w_out_b, x, vec(g_mix_post), gt1,
                                      vec(g_ffn_pre), sh2, sc2, router=router)
        else:
            x1, h2 = _out_proj([oa, ob, oc], w_out_b, x, vec(g_mix_post), gt1,
                               vec(g_ffn_pre), sh2, sc2)

        if not last:
            tqc = _pick(cn, (256, 128))
            oac = _attention(naqc, [(nakc, navc)], group=1, tq=tqc)
            obc = _attention(gqc, [(gkc, gvc)], group=group, tq=_pick(cn, (128,)))
            occ = _attention(dqc, [(dkc, dvc)], group=2, tq=tqc, diff=diff_args)
            if routed:
                router = jnp.zeros((d, LANES), F32).at[:, :N_EXPERTS].set(moe_router[li // 2])
                xc1, h2c, route_c = _out_proj([oac, obc, occ], w_out_b, x_ctx, vec(g_mix_post),
                                              cgt1, vec(g_ffn_pre), csh2, csc2, router=router)
            else:
                xc1, h2c = _out_proj([oac, obc, occ], w_out_b, x_ctx, vec(g_mix_post), cgt1,
                                     vec(g_ffn_pre), csh2, csc2)

        if routed:
            if li in moe_bf16:
                w1, w3, w2 = moe_bf16.pop(li)
            else:
                w1, w3, w2 = [w[li // 2].astype(BF16) for w in (moe_w1, moe_w3, moe_w2)]
            x = _moe_ffn(h2, route, w1, w3, w2, x1, vec(g_ffn_post), gt2)
            if not last:
                x_ctx = _moe_ffn(h2c, route_c, w1, w3, w2, xc1, vec(g_ffn_post), cgt2)
        else:
            if "w1" in na_bf16:
                w1, w3, w2 = na_bf16["w1"], na_bf16["w3"], na_bf16["w2"]
            else:
                w1, w3, w2 = [w[li // 2].astype(BF16) for w in (ffn_w1, ffn_w3, ffn_w2)]
            x = _ffn_dense(h2, w1, w3, w2, x1, vec(g_ffn_post), gt2)
            if not last:
                x_ctx = _ffn_dense(h2c.reshape(1, b * cn, d), w1, w3, w2, xc1.reshape(1, b * cn, d),
                                   vec(g_ffn_post), cgt2[:1]).reshape(b, cn, d)
    return x
```
